```python
import math
import jax, jax.numpy as jnp
from jax import lax
import numpy as np

D_MODEL = 1024
BATCH = 1
SEQ = 16384
DEPTH = 1
DEC_BATCH = 2
DEC_SEQ = 16384
PAST_LEN = 128

N_META = 16
EPS = 1e-6
S5_WIDTH = D_MODEL // 2
S5_GROUP = 16
S5_GROUPS = S5_WIDTH // S5_GROUP
S5_STATE = 64
S5_DT_MIN = 0.001
S5_DT_MAX = 0.1
HG_WIDTH = D_MODEL // 2
HG_HEAD_DIM = 128
HG_HEADS = HG_WIDTH // HG_HEAD_DIM
HG_CHUNK = 64
HG_PAD = HG_CHUNK - N_META
PEER_HEADS = 8
PEER_KEYS = 128
PEER_EXPERTS = PEER_KEYS * PEER_KEYS
PEER_QDIM = 256
PEER_HALF = PEER_QDIM // 2
PEER_TOPK = 16
PEER_BLOCK = 256
IN_SIZES = (S5_WIDTH, HG_WIDTH, HG_WIDTH, HG_WIDTH, HG_WIDTH, HG_WIDTH, D_MODEL, D_MODEL)
IN_COLS = S5_WIDTH + 5 * HG_WIDTH + 2 * D_MODEL

kernel_name = "hybrid_s5_hgrn2_peer_encoder"


def rmsnorm(x, g):
    xf = x.astype(jnp.float32)
    y = xf * lax.rsqrt(jnp.mean(xf * xf, axis=-1, keepdims=True) + EPS) * g.astype(jnp.float32)
    return y.astype(x.dtype)


def _s5_combine(e1, e2):
    a1, b1 = e1
    a2, b2 = e2
    return a1 * a2, a2 * b1 + b2


def s5_mixer(u, lam_re, lam_im, log_step, b_re, b_im, c_re, c_im, d_skip):
    bsz, t, _ = u.shape
    f32 = jnp.float32
    uf = u.astype(f32).reshape(bsz, t, S5_GROUPS, S5_GROUP)
    bu = lax.complex(jnp.einsum('gnc,btgc->btgn', b_re.astype(f32), uf),
                     jnp.einsum('gnc,btgc->btgn', b_im.astype(f32), uf))
    state = None
    for direction in (0, 1):
        lam = lax.complex(lam_re[direction].astype(f32), lam_im[direction].astype(f32))
        step = jnp.exp(log_step[direction].astype(f32))[:, None]
        lam_bar = jnp.exp(lam * step)
        x_in = bu * ((lam_bar - 1.0) / lam)
        a = jnp.broadcast_to(lam_bar, x_in.shape)
        _, xs = lax.associative_scan(_s5_combine, (a, x_in), axis=1, reverse=(direction == 1))
        state = xs if state is None else state + xs
    c_mat = lax.complex(c_re.astype(f32), c_im.astype(f32))
    y = jnp.real(jnp.einsum('gcn,btgn->btgc', c_mat, state)).reshape(bsz, t, S5_WIDTH)
    y = y + d_skip.astype(f32) * u.astype(f32)
    return y.astype(u.dtype)


def hgrn2_chunked(q, log_f, k, v):
    bsz, t, h, dk = q.shape
    dv = v.shape[-1]
    n = t // HG_CHUNK

    def to_chunks(z):
        return z.reshape(bsz, n, HG_CHUNK, h, z.shape[-1]).transpose(1, 0, 3, 2, 4)

    qc, gc, kc, vc = to_chunks(q), to_chunks(log_f), to_chunks(k), to_chunks(v)
    b = jnp.cumsum(gc, axis=3)
    b_last = b[:, :, :, -1:, :]
    q_dec = qc * jnp.exp(b)
    k_inv = kc * jnp.exp(-b)
    k_end = kc * jnp.exp(b_last - b)
    mask = jnp.tril(jnp.ones((HG_CHUNK, HG_CHUNK), dtype=bool))
    scores = jnp.where(mask, jnp.einsum('nbhtd,nbhsd->nbhts', q_dec, k_inv), 0.0)
    o_intra = jnp.einsum('nbhts,nbhsv->nbhtv', scores, vc)
    chunk_kv = jnp.einsum('nbhsd,nbhsv->nbhdv', k_end, vc)
    decay = jnp.exp(b_last[:, :, :, 0, :])

    def step(s_prev, xs):
        q_d, dec, kv = xs
        o = jnp.einsum('bhtd,bhdv->bhtv', q_d, s_prev)
        return dec[..., None] * s_prev + kv, o

    s0 = jnp.zeros((bsz, h, dk, dv), jnp.float32)
    _, o_inter = lax.scan(step, s0, (q_dec, decay, chunk_kv))
    o = o_intra + o_inter
    return o.transpose(1, 0, 3, 2, 4).reshape(bsz, t, h, dv)


def hgrn2_mixer(q_pre, f_fwd, f_bwd, i_in, lb, norm_g):
    bsz, t, _ = q_pre.shape

    def heads(z):
        return z.astype(jnp.float32).reshape(bsz, t, HG_HEADS, HG_HEAD_DIM)

    def pad(z):
        return jnp.pad(z, ((0, 0), (HG_PAD, 0), (0, 0), (0, 0)))

    def flip(z):
        return jnp.flip(z, axis=1)

    q = pad(jax.nn.silu(heads(q_pre)))
    v = pad(heads(i_in))
    lbh = lb.astype(jnp.float32).reshape(HG_HEADS, HG_HEAD_DIM)
    o = None
    for direction, f_pre in enumerate((f_fwd, f_bwd)):
        g = lbh + (1.0 - lbh) * jax.nn.sigmoid(heads(f_pre))
        log_f = pad(jnp.log(g))
        k = pad(1.0 - g)
        if direction == 0:
            od = hgrn2_chunked(q, log_f, k, v)
        else:
            od = flip(hgrn2_chunked(flip(q), flip(log_f), flip(k), flip(v)))
        o = od if o is None else o + od
    o = o[:, HG_PAD:]
    o = o * lax.rsqrt(jnp.mean(o * o, axis=-1, keepdims=True) + EPS)
    return o.reshape(bsz, t, HG_WIDTH) * norm_g.astype(jnp.float32)


def peer(h, w_q, keys, u_tab, v_tab):
    bsz, t, d = h.shape
    flat = h.reshape(-1, d)
    n = flat.shape[0]
    nblk = -(-n // PEER_BLOCK)
    flat = jnp.pad(flat, ((0, nblk * PEER_BLOCK - n), (0, 0))).reshape(nblk, PEER_BLOCK, d)

    def block(hb):
        q = (hb @ w_q).reshape(PEER_BLOCK, PEER_HEADS, 2, PEER_HALF)
        s = jnp.einsum('nhpd,hpkd->nhpk', q, keys).astype(jnp.float32)
        s_top, i_top = lax.top_k(s, PEER_TOPK)
        cand = s_top[:, :, 0, :, None] + s_top[:, :, 1, None, :]
        c_top, c_idx = lax.top_k(cand.reshape(PEER_BLOCK, PEER_HEADS, PEER_TOPK * PEER_TOPK), PEER_TOPK)
        i1 = jnp.take_along_axis(i_top[:, :, 0], c_idx // PEER_TOPK, axis=-1)
        i2 = jnp.take_along_axis(i_top[:, :, 1], c_idx % PEER_TOPK, axis=-1)
        expert = i1 * PEER_KEYS + i2
        gate = jax.nn.softmax(c_top, axis=-1)
        act = jax.nn.gelu(jnp.einsum('nd,nhkd->nhk', hb, u_tab[expert]).astype(jnp.float32), approximate=False)
        return jnp.einsum('nhk,nhkd->nd', (gate * act).astype(hb.dtype), v_tab[expert])

    out = lax.map(block, flat).reshape(-1, d)[:n]
    return out.reshape(bsz, t, d)


def encoder(x, meta, norm1_g, w_in, s5_lam_re, s5_lam_im, s5_log_step, s5_b_re, s5_b_im,
            s5_c_re, s5_c_im, s5_d, w_glu, hg_lb, hg_norm_g, w_hg_out, w_out, norm2_g,
            peer_wq, peer_keys, peer_u, peer_v, final_g):
    bsz = x.shape[0]
    meta_b = jnp.broadcast_to(meta.astype(x.dtype)[None], (bsz, N_META, D_MODEL))
    hs = jnp.concatenate([meta_b, x], axis=1)
    lbs = jnp.cumsum(jax.nn.softmax(hg_lb.astype(jnp.float32), axis=0), axis=0)
    split_at = [int(c) for c in np.cumsum(IN_SIZES)[:-1]]
    for l in range(DEPTH):
        h = rmsnorm(hs, norm1_g[l])
        z = h @ w_in[l]
        u_s5, q_pre, f_fwd, f_bwd, i_in, o_gate, gate_a, gate_b = jnp.split(z, split_at, axis=-1)
        y_s5 = s5_mixer(u_s5, s5_lam_re[l], s5_lam_im[l], s5_log_step[l], s5_b_re[l], s5_b_im[l],
                        s5_c_re[l], s5_c_im[l], s5_d[l])
        ga, gb = jnp.split(jax.nn.gelu(y_s5, approximate=False) @ w_glu[l], 2, axis=-1)
        y_a = ga * jax.nn.sigmoid(gb)
        y_hg = hgrn2_mixer(q_pre, f_fwd, f_bwd, i_in, lbs[l], hg_norm_g[l]) * jax.nn.silu(o_gate.astype(jnp.float32))
        y_b = y_hg.astype(hs.dtype) @ w_hg_out[l]
        mixed = jax.nn.sigmoid(gate_a) * y_a + jax.nn.sigmoid(gate_b) * y_b
        hs = hs + mixed @ w_out[l]
        hs = hs + peer(rmsnorm(hs, norm2_g[l]), peer_wq[l], peer_keys[l], peer_u[l], peer_v[l])
    return rmsnorm(hs, final_g)[:, N_META:]


def setup_inputs(seed: int = 0) -> dict:
    key = jax.random.key(seed)
    ks = jax.random.split(key, 26)
    f32 = jnp.float32

    def nrm(k, shape, s):
        return jax.random.normal(k, shape, f32) * s

    lam_shape = (DEPTH, 2, S5_GROUPS, S5_STATE)
    n_idx = jnp.arange(S5_STATE, dtype=f32)
    log_lo, log_hi = math.log(S5_DT_MIN), math.log(S5_DT_MAX)
    return {
        "x_prompt": nrm(ks[0], (BATCH, SEQ, D_MODEL), 1.0),
        "x_sample": nrm(ks[1], (DEC_BATCH, DEC_SEQ, D_MODEL), 1.0),
        "meta": nrm(ks[2], (N_META, D_MODEL), 1.0),
        "norm1_g": 1.0 + nrm(ks[3], (DEPTH, D_MODEL), 0.01),
        "w_in": nrm(ks[4], (DEPTH, D_MODEL, IN_COLS), D_MODEL ** -0.5),
        "s5_lam_re": -0.5 + nrm(ks[5], lam_shape, 0.01),
        "s5_lam_im": math.pi * n_idx + nrm(ks[6], lam_shape, 0.01),
        "s5_log_step": log_lo + jax.random.uniform(ks[7], (DEPTH, 2, S5_GROUPS), f32) * (log_hi - log_lo),
        "s5_b_re": nrm(ks[8], (DEPTH, S5_GROUPS, S5_STATE, S5_GROUP), (2 * S5_GROUP) ** -0.5),
        "s5_b_im": nrm(ks[9], (DEPTH, S5_GROUPS, S5_STATE, S5_GROUP), (2 * S5_GROUP) ** -0.5),
        "s5_c_re": nrm(ks[10], (DEPTH, S5_GROUPS, S5_GROUP, S5_STATE), S5_STATE ** -0.5),
        "s5_c_im": nrm(ks[11], (DEPTH, S5_GROUPS, S5_GROUP, S5_STATE), S5_STATE ** -0.5),
        "s5_d": nrm(ks[12], (DEPTH, S5_WIDTH), 1.0),
        "w_glu": nrm(ks[13], (DEPTH, S5_WIDTH, 2 * D_MODEL), S5_WIDTH ** -0.5),
        "hg_lb": nrm(ks[14], (DEPTH + 1, HG_WIDTH), 0.1),
        "hg_norm_g": 1.0 + nrm(ks[15], (DEPTH, HG_WIDTH), 0.01),
        "w_hg_out": nrm(ks[16], (DEPTH, HG_WIDTH, D_MODEL), HG_WIDTH ** -0.5),
        "w_out": nrm(ks[17], (DEPTH, D_MODEL, D_MODEL), D_MODEL ** -0.5),
        "norm2_g": 1.0 + nrm(ks[18], (DEPTH, D_MODEL), 0.01),
        "peer_wq": nrm(ks[19], (DEPTH, D_MODEL, PEER_HEADS * PEER_QDIM), D_MODEL ** -0.5),
        "peer_keys": nrm(ks[20], (DEPTH, PEER_HEADS, 2, PEER_KEYS, PEER_HALF), PEER_HALF ** -0.5),
        "peer_u": nrm(ks[21], (DEPTH, PEER_EXPERTS, D_MODEL), D_MODEL ** -0.5),
        "peer_v": nrm(ks[22], (DEPTH, PEER_EXPERTS, D_MODEL), 0.2),
        "final_g": 1.0 + nrm(ks[23], (D_MODEL,), 0.01),
    }


def reference(x_prompt, x_sample, meta, norm1_g, w_in, s5_lam_re, s5_lam_im, s5_log_step, s5_b_re,
              s5_b_im, s5_c_re, s5_c_im, s5_d, w_glu, hg_lb, hg_norm_g, w_hg_out, w_out, norm2_g,
              peer_wq, peer_keys, peer_u, peer_v, final_g):
    y_prompt = encoder(x_prompt, meta, norm1_g, w_in, s5_lam_re, s5_lam_im, s5_log_step, s5_b_re,
                       s5_b_im, s5_c_re, s5_c_im, s5_d, w_glu, hg_lb, hg_norm_g, w_hg_out, w_out,
                       norm2_g, peer_wq, peer_keys, peer_u, peer_v, final_g)
    y_sample = encoder(x_sample, meta, norm1_g, w_in, s5_lam_re, s5_lam_im, s5_log_step, s5_b_re,
                       s5_b_im, s5_c_re, s5_c_im, s5_d, w_glu, hg_lb, hg_norm_g, w_hg_out, w_out,
                       norm2_g, peer_wq, peer_keys, peer_u, peer_v, final_g)
    return (y_prompt, y_sample)
```

```python
import functools
import math

import jax
import jax.numpy as jnp
from jax import lax
from jax.experimental import pallas as pl
from jax.experimental.pallas import tpu as pltpu

F32 = jnp.float32
BF16 = jnp.bfloat16

D_MODEL = 1024
N_META = 16
EPS = 1e-6
TILE = 256
FRONT = TILE - N_META

S5_WIDTH = 512
S5_GROUP = 16
S5_GROUPS = 32
S5_STATE = 64
S5_CHUNK = 16
S5_FLAT = S5_CHUNK * S5_GROUP

HG_WIDTH = 512
HG_HEADS = 4
HG_DIM = 128
HG_CHUNK = 64

PEER_HEADS = 8
PEER_KEYS = 128
PEER_TOPK = 16
PEER_QDIM = 256
PEER_HALF = 128
PEER_ETILE = 1024
PEER_NCAND = 56

VMEM_LIMIT = 56 * 1024 * 1024

_NT = (((1,), (1,)), ((), ()))
_TN = (((0,), (0,)), ((), ()))


def _largest_divisor(n, target, multiple=1):
    best = None
    for d in range(multiple, min(n, target) + 1, multiple):
        if n % d == 0:
            best = d
    assert best is not None, (n, target, multiple)
    return best


def _params(*sem):
    return pltpu.CompilerParams(dimension_semantics=sem, vmem_limit_bytes=VMEM_LIMIT)


def _inproj_kernel(x_ref, metap_ref, g_ref, w_ref, u_ref, q_ref, ff_ref, fb_ref, v_ref, gate_ref):
    i = pl.program_id(1)
    hs = jnp.where(i == 0, metap_ref[...], x_ref[0])
    ms = jnp.mean(hs * hs, axis=-1, keepdims=True)
    h = (hs * lax.rsqrt(ms + EPS) * g_ref[...]).astype(BF16)
    row = lax.broadcasted_iota(jnp.int32, (TILE, 1), 0)
    valid = jnp.logical_or(i > 0, row >= FRONT)
    u = jnp.dot(h, w_ref[:, 0:S5_WIDTH], preferred_element_type=F32)
    u_ref[0] = jnp.where(valid, u, 0.0).astype(BF16)
    off = S5_WIDTH
    for ref in (q_ref, ff_ref, fb_ref, v_ref):
        ref[0] = jnp.dot(h, w_ref[:, off:off + HG_WIDTH], preferred_element_type=F32)
        off += HG_WIDTH
    gate_ref[0] = jnp.dot(h, w_ref[:, off:], preferred_element_type=F32)


def _inproj(x, metap, g, w_in):
    bsz, length, _ = x.shape
    nt = length // TILE + 1
    p = nt * TILE
    ncols = w_in.shape[1]
    ngate = ncols - S5_WIDTH - 4 * HG_WIDTH
    tok = lambda width: pl.BlockSpec((1, TILE, width), lambda b, i: (b, i, 0))
    full = lambda shape: pl.BlockSpec(shape, lambda b, i: (0,) * len(shape))
    hg_shape = jax.ShapeDtypeStruct((bsz, p, HG_WIDTH), F32)
    return pl.pallas_call(
        _inproj_kernel,
        grid=(bsz, nt),
        in_specs=[
            pl.BlockSpec((1, TILE, D_MODEL), lambda b, i: (b, jnp.maximum(i - 1, 0), 0)),
            full((TILE, D_MODEL)), full((1, D_MODEL)), full((D_MODEL, ncols)),
        ],
        out_specs=[tok(S5_WIDTH), tok(HG_WIDTH), tok(HG_WIDTH), tok(HG_WIDTH), tok(HG_WIDTH), tok(ngate)],
        out_shape=[jax.ShapeDtypeStruct((bsz, p, S5_WIDTH), BF16), hg_shape, hg_shape, hg_shape, hg_shape,
                   jax.ShapeDtypeStruct((bsz, p, ngate), F32)],
        compiler_params=_params("parallel", "arbitrary"),
        name="inproj",
    )(x, metap, g, w_in)


def _s5_weights(lam_re, lam_im, log_step, b_re, b_im, c_re, c_im, d_skip):
    hi = lax.Precision.HIGHEST
    f = lambda a: a.astype(F32)
    lam_re, lam_im, log_step = f(lam_re), f(lam_im), f(log_step)
    b_re, b_im, c_re, c_im, d_skip = f(b_re), f(b_im), f(c_re), f(c_im), f(d_skip)
    step = jnp.exp(log_step)[:, :, None]
    pw = jnp.arange(S5_CHUNK + 1, dtype=F32)[:, None, None, None]
    mag = jnp.exp(pw * lam_re[None] * step[None])
    ang = pw * lam_im[None] * step[None]
    a_re, a_im = mag * jnp.cos(ang), mag * jnp.sin(ang)
    den = lam_re * lam_re + lam_im * lam_im
    n_re, n_im = a_re[1] - 1.0, a_im[1]
    coef_re = (n_re * lam_re + n_im * lam_im) / den
    coef_im = (n_im * lam_re - n_re * lam_im) / den
    w_re = coef_re[None] * a_re - coef_im[None] * a_im
    w_im = coef_re[None] * a_im + coef_im[None] * a_re

    def conv_taps(direction):
        wr, wi = w_re[:S5_CHUNK, direction], w_im[:S5_CHUNK, direction]
        m_re = wr[..., None] * b_re[None] - wi[..., None] * b_im[None]
        m_im = wr[..., None] * b_im[None] + wi[..., None] * b_re[None]
        return (jnp.einsum('gon,dgni->dgoi', c_re, m_re, precision=hi)
                - jnp.einsum('gon,dgni->dgoi', c_im, m_im, precision=hi))

    k_f, k_b = conv_taps(0), conv_taps(1)
    center = k_f[0] + k_b[0] + jnp.eye(S5_GROUP, dtype=F32)[None] * d_skip.reshape(S5_GROUPS, S5_GROUP, 1)
    taps = jnp.concatenate([k_b[:0:-1], center[None], k_f[1:]], axis=0)
    t_idx = jnp.arange(S5_CHUNK)
    toe = taps[t_idx[None, :] - t_idx[:, None] + S5_CHUNK - 1]
    toe = toe.transpose(2, 0, 4, 1, 3).reshape(S5_GROUPS, S5_FLAT, S5_FLAT)

    def end_map(direction, powers):
        pr, pi = a_re[powers, direction], a_im[powers, direction]
        e_re = pr[..., None] * b_re[None] - pi[..., None] * b_im[None]
        e_im = pr[..., None] * b_im[None] + pi[..., None] * b_re[None]
        e = jnp.concatenate([e_re, e_im], axis=2)
        return e.transpose(1, 0, 3, 2).reshape(S5_GROUPS, S5_FLAT, 2 * S5_STATE)

    ends = jnp.concatenate([end_map(0, S5_CHUNK - 1 - t_idx), end_map(1, t_idx)], axis=2)
    w1 = jnp.concatenate([toe, ends], axis=2)

    def out_map(direction, powers):
        wr, wi = w_re[powers, direction], w_im[powers, direction]
        from_re = c_re[None] * wr[:, :, None, :] - c_im[None] * wi[:, :, None, :]
        from_im = -c_re[None] * wi[:, :, None, :] - c_im[None] * wr[:, :, None, :]
        m = jnp.concatenate([from_re, from_im], axis=3)
        return m.transpose(1, 3, 0, 2).reshape(S5_GROUPS, 2 * S5_STATE, S5_FLAT)

    q = jnp.concatenate([out_map(0, t_idx + 1), out_map(1, S5_CHUNK - t_idx)], axis=1)
    dec = jnp.stack([a_re[S5_CHUNK, 0], a_im[S5_CHUNK, 0], a_re[S5_CHUNK, 1], a_im[S5_CHUNK, 1]])
    return w1.astype(BF16), q.astype(BF16), dec.reshape(4, 16, 128)


def _s5_local_kernel(u_ref, w_ref, y_ref, e_ref):
    r = jnp.dot(u_ref[0, 0], w_ref[0], preferred_element_type=F32)
    y_ref[0, 0] = r[:, :S5_FLAT]
    e_ref[0, 0] = r[:, S5_FLAT:]


def _s5_local(ug, w1):
    bsz, _, nch, _ = ug.shape
    blk = lambda: pl.BlockSpec((1, 1, nch, S5_FLAT), lambda b, g: (b, g, 0, 0))
    shape = jax.ShapeDtypeStruct((bsz, S5_GROUPS, nch, S5_FLAT), F32)
    return pl.pallas_call(
        _s5_local_kernel,
        grid=(bsz, S5_GROUPS),
        in_specs=[blk(), pl.BlockSpec((1, S5_FLAT, 2 * S5_FLAT), lambda b, g: (g, 0, 0))],
        out_specs=[blk(), blk()],
        out_shape=[shape, shape],
        compiler_params=_params("parallel", "arbitrary"),
        name="s5_local",
    )(ug, w1)


def _s5_scan_kernel(dec_ref, ef_ref, eb_ref, xf_ref, xb_ref, st_ref, *, ct):
    j = pl.program_id(1)

    @pl.when(j == 0)
    def _():
        st_ref[...] = jnp.zeros_like(st_ref)

    a_fr, a_fi, a_br, a_bi = dec_ref[0], dec_ref[1], dec_ref[2], dec_ref[3]

    def body(c, carry):
        fr, fi, br, bi = carry
        xf_ref[0, 0, c] = fr
        xf_ref[0, 1, c] = fi
        fr, fi = (a_fr * fr - a_fi * fi + ef_ref[0, 0, c], a_fr * fi + a_fi * fr + ef_ref[0, 1, c])
        cb = ct - 1 - c
        xb_ref[0, 0, cb] = br
        xb_ref[0, 1, cb] = bi
        br, bi = (a_br * br - a_bi * bi + eb_ref[0, 0, cb], a_br * bi + a_bi * br + eb_ref[0, 1, cb])
        return fr, fi, br, bi

    fr, fi, br, bi = lax.fori_loop(0, ct, body, (st_ref[0], st_ref[1], st_ref[2], st_ref[3]))
    st_ref[0], st_ref[1], st_ref[2], st_ref[3] = fr, fi, br, bi


def _s5_scan(et, dec):
    bsz, _, nch, _, _ = et.shape
    ct = _largest_divisor(nch, 128)
    nj = nch // ct
    blk_f = pl.BlockSpec((1, 2, ct, 16, 128), lambda b, j: (b, 0, j, 0, 0))
    blk_b = pl.BlockSpec((1, 2, ct, 16, 128), lambda b, j: (b, 1, nj - 1 - j, 0, 0))
    out_f = pl.BlockSpec((1, 2, ct, 16, 128), lambda b, j: (b, 0, j, 0, 0))
    out_b = pl.BlockSpec((1, 2, ct, 16, 128), lambda b, j: (b, 0, nj - 1 - j, 0, 0))
    shape = jax.ShapeDtypeStruct((bsz, 2, nch, 16, 128), F32)
    return pl.pallas_call(
        functools.partial(_s5_scan_kernel, ct=ct),
        grid=(bsz, nj),
        in_specs=[pl.BlockSpec((4, 16, 128), lambda b, j: (0, 0, 0)), blk_f, blk_b],
        out_specs=[out_f, out_b],
        out_shape=[shape, shape],
        scratch_shapes=[pltpu.VMEM((4, 16, 128), F32)],
        compiler_params=_params("parallel", "arbitrary"),
        name="s5_scan",
    )(dec, et, et)


def _s5_out_kernel(y_ref, x_ref, q_ref, o_ref):
    o_ref[0, 0] = y_ref[0, 0] + jnp.dot(x_ref[0, 0].astype(BF16), q_ref[0], preferred_element_type=F32)


def _s5_out(y_intra, xg, q):
    bsz, _, nch, _ = y_intra.shape
    blk = lambda: pl.BlockSpec((1, 1, nch, S5_FLAT), lambda b, g: (b, g, 0, 0))
    return pl.pallas_call(
        _s5_out_kernel,
        grid=(bsz, S5_GROUPS),
        in_specs=[blk(), blk(), pl.BlockSpec((1, S5_FLAT, S5_FLAT), lambda b, g: (g, 0, 0))],
        out_specs=blk(),
        out_shape=jax.ShapeDtypeStruct(y_intra.shape, F32),
        compiler_params=_params("parallel", "arbitrary"),
        name="s5_out",
    )(y_intra, xg, q)


def _s5_mixer(u, w1, q, dec):
    bsz, p, _ = u.shape
    nch = p // S5_CHUNK
    ug = u.reshape(bsz, nch, S5_CHUNK, S5_GROUPS, S5_GROUP).transpose(0, 3, 1, 2, 4)
    ug = ug.reshape(bsz, S5_GROUPS, nch, S5_FLAT)
    y_intra, e = _s5_local(ug, w1)
    et = e.reshape(bsz, S5_GROUPS, nch, 4, S5_STATE).transpose(0, 3, 2, 1, 4).reshape(bsz, 4, nch, 16, 128)
    xf, xb = _s5_scan(et, dec)
    xg = jnp.concatenate([xf, xb], axis=1).reshape(bsz, 4, nch, S5_GROUPS, S5_STATE)
    xg = xg.transpose(0, 3, 2, 1, 4).reshape(bsz, S5_GROUPS, nch, S5_FLAT)
    y = _s5_out(y_intra, xg, q)
    y = y.reshape(bsz, S5_GROUPS, nch, S5_CHUNK, S5_GROUP).transpose(0, 2, 3, 1, 4)
    return y.reshape(bsz, p, S5_WIDTH)


def _hg_kernel(lb_ref, qf_ref, ff_ref, vf_ref, qb_ref, fb_ref, vb_ref, of_ref, ob_ref, st_ref):
    i = pl.program_id(1)
    n = pl.num_programs(1)

    @pl.when(i == 0)
    def _():
        st_ref[...] = jnp.zeros_like(st_ref)

    lb = lb_ref[...]
    r = lax.broadcasted_iota(jnp.int32, (HG_CHUNK, HG_CHUNK), 0)
    c = lax.broadcasted_iota(jnp.int32, (HG_CHUNK, HG_CHUNK), 1)
    row = lax.broadcasted_iota(jnp.int32, (HG_CHUNK, 1), 0)
    passes = (
        (0, qf_ref, ff_ref, vf_ref, of_ref, i, c <= r, range(TILE // HG_CHUNK), HG_CHUNK - 1),
        (1, qb_ref, fb_ref, vb_ref, ob_ref, n - 1 - i, c >= r, reversed(range(TILE // HG_CHUNK)), 0),
    )
    for d, q_ref, f_ref, v_ref, o_ref, tile, keep, chunks, tot_row in passes:
        cum = keep.astype(F32)
        for ch in chunks:
            rows = slice(ch * HG_CHUNK, (ch + 1) * HG_CHUNK)
            q_pre = q_ref[0, rows, :]
            q = q_pre * jax.nn.sigmoid(q_pre)
            g = lb + (1.0 - lb) * jax.nn.sigmoid(f_ref[0, rows, :])
            valid = tile * TILE + ch * HG_CHUNK + row >= FRONT
            log_f = jnp.where(valid, jnp.log(g), 0.0)
            k = jnp.where(valid, 1.0 - g, 0.0)
            b = jnp.dot(cum, log_f, preferred_element_type=F32, precision=lax.Precision.HIGHEST)
            b_tot = b[tot_row:tot_row + 1]
            q_dec = (q * jnp.exp(b)).astype(BF16)
            k_inv = (k * jnp.exp(-b)).astype(BF16)
            k_end = (k * jnp.exp(b_tot - b)).astype(BF16)
            decay = jnp.exp(b_tot)
            v = v_ref[0, rows, :].astype(BF16)
            for h in range(HG_HEADS):
                ls = slice(h * HG_DIM, (h + 1) * HG_DIM)
                scores = lax.dot_general(q_dec[:, ls], k_inv[:, ls], _NT, preferred_element_type=F32)
                scores = jnp.where(keep, scores, 0.0).astype(BF16)
                st = st_ref[d, h]
                o = jnp.dot(scores, v[:, ls], preferred_element_type=F32)
                o += lax.dot_general(q_dec[:, ls], st.astype(BF16), _NT, preferred_element_type=F32)
                kv = lax.dot_general(v[:, ls], k_end[:, ls], _TN, preferred_element_type=F32)
                st_ref[d, h] = decay[:, ls] * st + kv
                o_ref[0, rows, ls] = o


def _hgrn2(q_pre, f_fwd, f_bwd, i_in, lb):
    bsz, p, _ = q_pre.shape
    nt = p // TILE
    fwd = lambda: pl.BlockSpec((1, TILE, HG_WIDTH), lambda b, i: (b, i, 0))
    bwd = lambda: pl.BlockSpec((1, TILE, HG_WIDTH), lambda b, i: (b, nt - 1 - i, 0))
    shape = jax.ShapeDtypeStruct((bsz, p, HG_WIDTH), F32)
    return pl.pallas_call(
        _hg_kernel,
        grid=(bsz, nt),
        in_specs=[pl.BlockSpec((1, HG_WIDTH), lambda b, i: (0, 0)), fwd(), fwd(), fwd(), bwd(), bwd(), bwd()],
        out_specs=[fwd(), bwd()],
        out_shape=[shape, shape],
        scratch_shapes=[pltpu.VMEM((2, HG_HEADS, HG_DIM, HG_DIM), F32)],
        compiler_params=_params("parallel", "arbitrary"),
        name="hgrn2",
    )(lb, q_pre, f_fwd, i_in, q_pre, f_bwd, i_in)


def _gelu(x):
    return 0.5 * x * (1.0 + lax.erf(x * (1.0 / math.sqrt(2.0))))


def _merge_kernel(x_ref, metap_ref, ys_ref, of_ref, ob_ref, gate_ref, wglu_ref, hgn_ref, whg_ref, wout_ref,
                  n2_ref, hs_ref, h2_ref):
    i = pl.program_id(1)
    hs = jnp.where(i == 0, metap_ref[...], x_ref[0])
    glu = jnp.dot(_gelu(ys_ref[0]).astype(BF16), wglu_ref[...], preferred_element_type=F32)
    y_a = glu[:, :D_MODEL] * jax.nn.sigmoid(glu[:, D_MODEL:])
    o = of_ref[0] + ob_ref[0]
    normed = []
    for h in range(HG_HEADS):
        oh = o[:, h * HG_DIM:(h + 1) * HG_DIM]
        normed.append(oh * lax.rsqrt(jnp.mean(oh * oh, axis=-1, keepdims=True) + EPS))
    o_gate = gate_ref[0, :, 0:HG_WIDTH]
    y_hg = jnp.concatenate(normed, axis=-1) * hgn_ref[...] * (o_gate * jax.nn.sigmoid(o_gate))
    y_b = jnp.dot(y_hg.astype(BF16), whg_ref[...], preferred_element_type=F32)
    gate_a = gate_ref[0, :, HG_WIDTH:HG_WIDTH + D_MODEL]
    gate_b = gate_ref[0, :, HG_WIDTH + D_MODEL:]
    mixed = jax.nn.sigmoid(gate_a) * y_a + jax.nn.sigmoid(gate_b) * y_b
    hs = hs + jnp.dot(mixed.astype(BF16), wout_ref[...], preferred_element_type=F32)
    hs_ref[0] = hs
    ms = jnp.mean(hs * hs, axis=-1, keepdims=True)
    h2_ref[0] = (hs * lax.rsqrt(ms + EPS) * n2_ref[...]).astype(BF16)


def _merge(x, metap, y_s5, o_f, o_b, gates, w_glu, hg_norm_g, w_hg_out, w_out, norm2_g):
    bsz, p, _ = y_s5.shape
    nt = p // TILE
    tok = lambda width: pl.BlockSpec((1, TILE, width), lambda b, i: (b, i, 0))
    full = lambda a: pl.BlockSpec(a.shape, lambda b, i: (0,) * a.ndim)
    return pl.pallas_call(
        _merge_kernel,
        grid=(bsz, nt),
        in_specs=[
            pl.BlockSpec((1, TILE, D_MODEL), lambda b, i: (b, jnp.maximum(i - 1, 0), 0)),
            full(metap), tok(S5_WIDTH), tok(HG_WIDTH), tok(HG_WIDTH), tok(gates.shape[-1]),
            full(w_glu), full(hg_norm_g), full(w_hg_out), full(w_out), full(norm2_g),
        ],
        out_specs=[tok(D_MODEL), tok(D_MODEL)],
        out_shape=[jax.ShapeDtypeStruct((bsz, p, D_MODEL), F32), jax.ShapeDtypeStruct((bsz, p, D_MODEL), BF16)],
        compiler_params=_params("parallel", "arbitrary"),
        name="merge",
    )(x, metap, y_s5, o_f, o_b, gates, w_glu, hg_norm_g, w_hg_out, w_out, norm2_g)


def _staircase():
    return [(i, PEER_TOPK // (i + 1)) for i in range(PEER_TOPK)]


def _top16(s):
    work = s
    rank = jnp.full(s.shape, float(PEER_TOPK), F32)
    tops = []
    for it in range(PEER_TOPK):
        m = jnp.max(work, axis=0, keepdims=True)
        hit = work == m
        rank = jnp.where(hit, float(it), rank)
        work = jnp.where(hit, -jnp.inf, work)
        tops.append(m)
    return tops, rank


def _score_kernel(h2_ref, wq_ref, keys_ref, r2_ref, p2_ref, cnt_ref, p1_ref, cand_ref):
    h2 = h2_ref[0]
    for h in range(PEER_HEADS):
        qh = jnp.dot(h2, wq_ref[:, h * PEER_QDIM:(h + 1) * PEER_QDIM], preferred_element_type=F32).astype(BF16)
        s1 = lax.dot_general(keys_ref[h, 0], qh[:, :PEER_HALF], _NT, preferred_element_type=F32)
        s2 = lax.dot_general(keys_ref[h, 1], qh[:, PEER_HALF:], _NT, preferred_element_type=F32)
        t1, rank1 = _top16(s1)
        t2, rank2 = _top16(s2)
        t2_all = jnp.concatenate(t2, axis=0)
        cand_ref[...] = jnp.full(cand_ref.shape, -jnp.inf, F32)
        off = 0
        for i, n_i in _staircase():
            cand_ref[off:off + n_i, :] = t1[i] + t2_all[0:n_i]
            off += n_i
        cand = cand_ref[...]
        c_max = t1[0] + t2[0]
        work = cand
        z = jnp.zeros_like(c_max)
        tau = c_max
        for it in range(PEER_TOPK):
            tau = jnp.max(work, axis=0, keepdims=True)
            z = z + jnp.exp(tau - c_max)
            work = jnp.where(work == tau, -jnp.inf, work)
        chosen = (cand >= tau).astype(F32)
        cnt = jnp.zeros_like(s1)
        off = 0
        for i, n_i in _staircase():
            cnt_i = jnp.sum(chosen[off:off + n_i], axis=0, keepdims=True)
            cnt = jnp.where(rank1 == float(i), cnt_i, cnt)
            off += n_i
        r2_ref[0, h] = rank2
        p2_ref[0, h] = jnp.exp(s2 - t2[0])
        cnt_ref[0, h] = cnt
        p1_ref[0, h] = jnp.exp(s1 - t1[0]) / z


def _peer_scores(h2, w_q, keys):
    bsz, p, _ = h2.shape
    nt = p // TILE
    blk = lambda: pl.BlockSpec((1, PEER_HEADS, PEER_KEYS, TILE), lambda b, i: (b, 0, 0, i))
    shape = jax.ShapeDtypeStruct((bsz, PEER_HEADS, PEER_KEYS, p), F32)
    return pl.pallas_call(
        _score_kernel,
        grid=(bsz, nt),
        in_specs=[
            pl.BlockSpec((1, TILE, D_MODEL), lambda b, i: (b, i, 0)),
            pl.BlockSpec(w_q.shape, lambda b, i: (0, 0)),
            pl.BlockSpec(keys.shape, lambda b, i: (0, 0, 0, 0)),
        ],
        out_specs=[blk(), blk(), blk(), blk()],
        out_shape=[shape, shape, shape, shape],
        scratch_shapes=[pltpu.VMEM((PEER_NCAND, TILE), F32)],
        compiler_params=_params("parallel", "arbitrary"),
        name="peer_scores",
    )(h2, w_q, keys)


def _dense_kernel(h2_ref, u_ref, vt_ref, r2_ref, p2_ref, cnt_ref, p1_ref, hs_ref, g_ref, y_ref,
                  act_ref, wa_ref, acc_ref, *, tn):
    e = pl.program_id(2)

    @pl.when(e == 0)
    def _():
        acc_ref[...] = jnp.zeros_like(acc_ref)

    act_ref[...] = lax.dot_general(u_ref[...], h2_ref[0], _NT, preferred_element_type=F32)
    first = pl.multiple_of(e * (PEER_ETILE // PEER_KEYS), 8)
    for lc in range(tn // 128):
        lanes = slice(lc * 128, (lc + 1) * 128)
        for j in range(PEER_ETILE // PEER_KEYS):
            w = jnp.zeros((PEER_KEYS, 128), F32)
            for h in range(PEER_HEADS):
                cnt = cnt_ref[0, h, pl.ds(first, 8), lanes][j:j + 1]
                p1 = p1_ref[0, h, pl.ds(first, 8), lanes][j:j + 1]
                w += jnp.where(r2_ref[0, h, :, lanes] < cnt, p2_ref[0, h, :, lanes], 0.0) * p1
            rows = slice(j * PEER_KEYS, (j + 1) * PEER_KEYS)
            wa_ref[rows, lanes] = (w * _gelu(act_ref[rows, lanes])).astype(BF16)
    acc_ref[...] += jnp.dot(vt_ref[...], wa_ref[...], preferred_element_type=F32)

    @pl.when(e == pl.num_programs(2) - 1)
    def _():
        hs = hs_ref[0] + acc_ref[...].T
        ms = jnp.mean(hs * hs, axis=-1, keepdims=True)
        y_ref[0] = hs * lax.rsqrt(ms + EPS) * g_ref[...]


def _peer_dense(h2, u_tab, vt_tab, r2, p2, cnt, p1, hs, final_g):
    bsz, p, _ = h2.shape
    tn = _largest_divisor(p, 640, 128)
    nexp = u_tab.shape[0]
    tok = lambda: pl.BlockSpec((1, tn, D_MODEL), lambda b, t, e: (b, t, 0))
    sel = lambda: pl.BlockSpec((1, PEER_HEADS, PEER_KEYS, tn), lambda b, t, e: (b, 0, 0, t))
    return pl.pallas_call(
        functools.partial(_dense_kernel, tn=tn),
        grid=(bsz, p // tn, nexp // PEER_ETILE),
        in_specs=[
            tok(),
            pl.BlockSpec((PEER_ETILE, D_MODEL), lambda b, t, e: (e, 0)),
            pl.BlockSpec((D_MODEL, PEER_ETILE), lambda b, t, e: (0, e)),
            sel(), sel(), sel(), sel(), tok(),
            pl.BlockSpec((1, D_MODEL), lambda b, t, e: (0, 0)),
        ],
        out_specs=tok(),
        out_shape=jax.ShapeDtypeStruct((bsz, p, D_MODEL), F32),
        scratch_shapes=[pltpu.VMEM((PEER_ETILE, tn), F32), pltpu.VMEM((PEER_ETILE, tn), BF16),
                        pltpu.VMEM((D_MODEL, tn), F32)],
        compiler_params=_params("parallel", "arbitrary", "arbitrary"),
        name="peer_dense",
    )(h2, u_tab, vt_tab, r2, p2, cnt, p1, hs, final_g)


def kernel(x_prompt, x_sample, meta, norm1_g, w_in, s5_lam_re, s5_lam_im, s5_log_step, s5_b_re, s5_b_im,
           s5_c_re, s5_c_im, s5_d, w_glu, hg_lb, hg_norm_g, w_hg_out, w_out, norm2_g, peer_wq, peer_keys,
           peer_u, peer_v, final_g):
    assert x_prompt.shape[1] == x_sample.shape[1] and x_prompt.shape[1] % TILE == 0
    assert norm1_g.shape[0] == 1, "single-layer trunk"
    n_prompt = x_prompt.shape[0]
    x = jnp.concatenate([x_prompt, x_sample], axis=0).astype(F32)
    metap = jnp.concatenate([jnp.zeros((FRONT, D_MODEL), F32), meta.astype(F32)], axis=0)
    row = lambda a: a.astype(F32).reshape(1, -1)

    u, q_pre, f_fwd, f_bwd, i_in, gates = _inproj(x, metap, row(norm1_g[0]), w_in[0].astype(BF16))

    w1, q_s5, dec = _s5_weights(s5_lam_re[0], s5_lam_im[0], s5_log_step[0], s5_b_re[0], s5_b_im[0],
                                s5_c_re[0], s5_c_im[0], s5_d[0])
    y_s5 = _s5_mixer(u, w1, q_s5, dec)

    lbs = jnp.cumsum(jax.nn.softmax(hg_lb.astype(F32), axis=0), axis=0)
    o_f, o_b = _hgrn2(q_pre, f_fwd, f_bwd, i_in, row(lbs[0]))

    hs, h2 = _merge(x, metap, y_s5, o_f, o_b, gates, w_glu[0].astype(BF16), row(hg_norm_g[0]),
                    w_hg_out[0].astype(BF16), w_out[0].astype(BF16), row(norm2_g[0]))

    r2, p2, cnt, p1 = _peer_scores(h2, peer_wq[0].astype(BF16), peer_keys[0].astype(BF16))
    y = _peer_dense(h2, peer_u[0].astype(BF16), peer_v[0].astype(BF16).T, r2, p2, cnt, p1, hs, row(final_g))
    y = y[:, TILE:]
    return (y[:n_prompt], y[n_prompt:])
```

```python
import functools
import math

import jax
import jax.numpy as jnp
from jax import lax
from jax.experimental import pallas as pl
from jax.experimental.pallas import tpu as pltpu

F32 = jnp.float32
BF16 = jnp.bfloat16

D_MODEL = 1024
N_META = 16
EPS = 1e-6
TILE = 256
FRONT = TILE - N_META

S5_WIDTH = 512
S5_GROUP = 16
S5_GROUPS = 32
S5_STATE = 64
S5_CHUNK = 16
S5_FLAT = S5_CHUNK * S5_GROUP

HG_WIDTH = 512
HG_HEADS = 4
HG_DIM = 128
HG_CHUNK = 64

PEER_HEADS = 8
PEER_KEYS = 128
PEER_TOPK = 16
PEER_QDIM = 256
PEER_HALF = 128
PEER_ETILE = 1024
PEER_NCAND = 56

VMEM_LIMIT = 56 * 1024 * 1024

_NT = (((1,), (1,)), ((), ()))
_TN = (((0,), (0,)), ((), ()))


def _largest_divisor(n, target, multiple=1):
    best = None
    for d in range(multiple, min(n, target) + 1, multiple):
        if n % d == 0:
            best = d
    assert best is not None, (n, target, multiple)
    return best


def _pack(x):
    return pltpu.bitcast(x.astype(BF16), jnp.uint32)


def _unpack(x):
    return pltpu.bitcast(x, BF16)


def _pack_rows(w):
    rows, cols = w.shape
    pairs = w.astype(BF16).reshape(rows // 2, 2, cols).swapaxes(-1, -2)
    return lax.bitcast_convert_type(pairs, jnp.uint32)


def _params(*sem):
    return pltpu.CompilerParams(dimension_semantics=sem, vmem_limit_bytes=VMEM_LIMIT)


def _inproj_kernel(x_ref, metap_ref, g_ref, w_ref, u_ref, q_ref, ff_ref, fb_ref, v_ref, gate_ref):
    i = pl.program_id(1)
    hs = jnp.where(i == 0, metap_ref[...], x_ref[0])
    ms = jnp.mean(hs * hs, axis=-1, keepdims=True)
    h = (hs * lax.rsqrt(ms + EPS) * g_ref[...]).astype(BF16)
    row = lax.broadcasted_iota(jnp.int32, (TILE, 1), 0)
    valid = jnp.logical_or(i > 0, row >= FRONT)
    u = jnp.dot(h, w_ref[:, 0:S5_WIDTH], preferred_element_type=F32)
    u_ref[0] = jnp.where(valid, u, 0.0).astype(BF16)
    off = S5_WIDTH
    for ref in (q_ref, ff_ref, fb_ref, v_ref):
        ref[0] = jnp.dot(h, w_ref[:, off:off + HG_WIDTH], preferred_element_type=F32)
        off += HG_WIDTH
    gate_ref[0] = jnp.dot(h, w_ref[:, off:], preferred_element_type=F32)


def _inproj(x, metap, g, w_in):
    bsz, length, _ = x.shape
    nt = length // TILE + 1
    p = nt * TILE
    ncols = w_in.shape[1]
    ngate = ncols - S5_WIDTH - 4 * HG_WIDTH
    tok = lambda width: pl.BlockSpec((1, TILE, width), lambda b, i: (b, i, 0))
    full = lambda shape: pl.BlockSpec(shape, lambda b, i: (0,) * len(shape))
    hg_shape = jax.ShapeDtypeStruct((bsz, p, HG_WIDTH), F32)
    return pl.pallas_call(
        _inproj_kernel,
        grid=(bsz, nt),
        in_specs=[
            pl.BlockSpec((1, TILE, D_MODEL), lambda b, i: (b, jnp.maximum(i - 1, 0), 0)),
            full((TILE, D_MODEL)), full((1, D_MODEL)), full((D_MODEL, ncols)),
        ],
        out_specs=[tok(S5_WIDTH), tok(HG_WIDTH), tok(HG_WIDTH), tok(HG_WIDTH), tok(HG_WIDTH), tok(ngate)],
        out_shape=[jax.ShapeDtypeStruct((bsz, p, S5_WIDTH), BF16), hg_shape, hg_shape, hg_shape, hg_shape,
                   jax.ShapeDtypeStruct((bsz, p, ngate), F32)],
        compiler_params=_params("parallel", "arbitrary"),
        name="inproj",
    )(x, metap, g, w_in)


def _s5_weights(lam_re, lam_im, log_step, b_re, b_im, c_re, c_im, d_skip):
    hi = lax.Precision.HIGHEST
    f = lambda a: a.astype(F32)
    lam_re, lam_im, log_step = f(lam_re), f(lam_im), f(log_step)
    b_re, b_im, c_re, c_im, d_skip = f(b_re), f(b_im), f(c_re), f(c_im), f(d_skip)
    step = jnp.exp(log_step)[:, :, None]
    pw = jnp.arange(S5_CHUNK + 1, dtype=F32)[:, None, None, None]
    mag = jnp.exp(pw * lam_re[None] * step[None])
    ang = pw * lam_im[None] * step[None]
    a_re, a_im = mag * jnp.cos(ang), mag * jnp.sin(ang)
    den = lam_re * lam_re + lam_im * lam_im
    n_re, n_im = a_re[1] - 1.0, a_im[1]
    coef_re = (n_re * lam_re + n_im * lam_im) / den
    coef_im = (n_im * lam_re - n_re * lam_im) / den
    w_re = coef_re[None] * a_re - coef_im[None] * a_im
    w_im = coef_re[None] * a_im + coef_im[None] * a_re

    def conv_taps(direction):
        wr, wi = w_re[:S5_CHUNK, direction], w_im[:S5_CHUNK, direction]
        m_re = wr[..., None] * b_re[None] - wi[..., None] * b_im[None]
        m_im = wr[..., None] * b_im[None] + wi[..., None] * b_re[None]
        return (jnp.einsum('gon,dgni->dgoi', c_re, m_re, precision=hi)
                - jnp.einsum('gon,dgni->dgoi', c_im, m_im, precision=hi))

    k_f, k_b = conv_taps(0), conv_taps(1)
    center = k_f[0] + k_b[0] + jnp.eye(S5_GROUP, dtype=F32)[None] * d_skip.reshape(S5_GROUPS, S5_GROUP, 1)
    taps = jnp.concatenate([k_b[:0:-1], center[None], k_f[1:]], axis=0)
    t_idx = jnp.arange(S5_CHUNK)
    toe = taps[t_idx[None, :] - t_idx[:, None] + S5_CHUNK - 1]
    toe = toe.transpose(2, 0, 4, 1, 3).reshape(S5_GROUPS, S5_FLAT, S5_FLAT)

    def end_map(direction, powers):
        pr, pi = a_re[powers, direction], a_im[powers, direction]
        e_re = pr[..., None] * b_re[None] - pi[..., None] * b_im[None]
        e_im = pr[..., None] * b_im[None] + pi[..., None] * b_re[None]
        e = jnp.concatenate([e_re, e_im], axis=2)
        return e.transpose(1, 0, 3, 2).reshape(S5_GROUPS, S5_FLAT, 2 * S5_STATE)

    ends = jnp.concatenate([end_map(0, S5_CHUNK - 1 - t_idx), end_map(1, t_idx)], axis=2)
    w1 = jnp.concatenate([toe, ends], axis=2)

    def out_map(direction, powers):
        wr, wi = w_re[powers, direction], w_im[powers, direction]
        from_re = c_re[None] * wr[:, :, None, :] - c_im[None] * wi[:, :, None, :]
        from_im = -c_re[None] * wi[:, :, None, :] - c_im[None] * wr[:, :, None, :]
        m = jnp.concatenate([from_re, from_im], axis=3)
        return m.transpose(1, 3, 0, 2).reshape(S5_GROUPS, 2 * S5_STATE, S5_FLAT)

    q = jnp.concatenate([out_map(0, t_idx + 1), out_map(1, S5_CHUNK - t_idx)], axis=1)
    dec = jnp.stack([a_re[S5_CHUNK, 0], a_im[S5_CHUNK, 0], a_re[S5_CHUNK, 1], a_im[S5_CHUNK, 1]])
    return w1.astype(BF16), q.astype(BF16), dec.reshape(4, 16, 128)


def _s5_local_kernel(u_ref, w_ref, y_ref, e_ref):
    r = jnp.dot(u_ref[0, 0], w_ref[0], preferred_element_type=F32)
    y_ref[0, 0] = r[:, :S5_FLAT]
    e_ref[0, 0] = r[:, S5_FLAT:]


def _s5_local(ug, w1):
    bsz, _, nch, _ = ug.shape
    blk = lambda: pl.BlockSpec((1, 1, nch, S5_FLAT), lambda b, g: (b, g, 0, 0))
    shape = jax.ShapeDtypeStruct((bsz, S5_GROUPS, nch, S5_FLAT), F32)
    return pl.pallas_call(
        _s5_local_kernel,
        grid=(bsz, S5_GROUPS),
        in_specs=[blk(), pl.BlockSpec((1, S5_FLAT, 2 * S5_FLAT), lambda b, g: (g, 0, 0))],
        out_specs=[blk(), blk()],
        out_shape=[shape, shape],
        compiler_params=_params("parallel", "arbitrary"),
        name="s5_local",
    )(ug, w1)


def _s5_scan_kernel(dec_ref, ef_ref, eb_ref, xf_ref, xb_ref, st_ref, *, ct):
    j = pl.program_id(1)

    @pl.when(j == 0)
    def _():
        st_ref[...] = jnp.zeros_like(st_ref)

    a_fr, a_fi, a_br, a_bi = dec_ref[0], dec_ref[1], dec_ref[2], dec_ref[3]

    def body(c, carry):
        fr, fi, br, bi = carry
        xf_ref[0, 0, c] = fr
        xf_ref[0, 1, c] = fi
        fr, fi = (a_fr * fr - a_fi * fi + ef_ref[0, 0, c], a_fr * fi + a_fi * fr + ef_ref[0, 1, c])
        cb = ct - 1 - c
        xb_ref[0, 0, cb] = br
        xb_ref[0, 1, cb] = bi
        br, bi = (a_br * br - a_bi * bi + eb_ref[0, 0, cb], a_br * bi + a_bi * br + eb_ref[0, 1, cb])
        return fr, fi, br, bi

    fr, fi, br, bi = lax.fori_loop(0, ct, body, (st_ref[0], st_ref[1], st_ref[2], st_ref[3]))
    st_ref[0], st_ref[1], st_ref[2], st_ref[3] = fr, fi, br, bi


def _s5_scan(et, dec):
    bsz, _, nch, _, _ = et.shape
    ct = _largest_divisor(nch, 128)
    nj = nch // ct
    blk_f = pl.BlockSpec((1, 2, ct, 16, 128), lambda b, j: (b, 0, j, 0, 0))
    blk_b = pl.BlockSpec((1, 2, ct, 16, 128), lambda b, j: (b, 1, nj - 1 - j, 0, 0))
    out_f = pl.BlockSpec((1, 2, ct, 16, 128), lambda b, j: (b, 0, j, 0, 0))
    out_b = pl.BlockSpec((1, 2, ct, 16, 128), lambda b, j: (b, 0, nj - 1 - j, 0, 0))
    shape = jax.ShapeDtypeStruct((bsz, 2, nch, 16, 128), F32)
    return pl.pallas_call(
        functools.partial(_s5_scan_kernel, ct=ct),
        grid=(bsz, nj),
        in_specs=[pl.BlockSpec((4, 16, 128), lambda b, j: (0, 0, 0)), blk_f, blk_b],
        out_specs=[out_f, out_b],
        out_shape=[shape, shape],
        scratch_shapes=[pltpu.VMEM((4, 16, 128), F32)],
        compiler_params=_params("parallel", "arbitrary"),
        name="s5_scan",
    )(dec, et, et)


def _s5_out_kernel(y_ref, x_ref, q_ref, o_ref):
    o_ref[0, 0] = y_ref[0, 0] + jnp.dot(x_ref[0, 0].astype(BF16), q_ref[0], preferred_element_type=F32)


def _s5_out(y_intra, xg, q):
    bsz, _, nch, _ = y_intra.shape
    blk = lambda: pl.BlockSpec((1, 1, nch, S5_FLAT), lambda b, g: (b, g, 0, 0))
    return pl.pallas_call(
        _s5_out_kernel,
        grid=(bsz, S5_GROUPS),
        in_specs=[blk(), blk(), pl.BlockSpec((1, S5_FLAT, S5_FLAT), lambda b, g: (g, 0, 0))],
        out_specs=blk(),
        out_shape=jax.ShapeDtypeStruct(y_intra.shape, F32),
        compiler_params=_params("parallel", "arbitrary"),
        name="s5_out",
    )(y_intra, xg, q)


def _s5_mixer(u, w1, q, dec):
    bsz, p, _ = u.shape
    nch = p // S5_CHUNK
    ug = u.reshape(bsz, nch, S5_CHUNK, S5_GROUPS, S5_GROUP).transpose(0, 3, 1, 2, 4)
    ug = ug.reshape(bsz, S5_GROUPS, nch, S5_FLAT)
    y_intra, e = _s5_local(ug, w1)
    et = e.reshape(bsz, S5_GROUPS, nch, 4, S5_STATE).transpose(0, 3, 2, 1, 4).reshape(bsz, 4, nch, 16, 128)
    xf, xb = _s5_scan(et, dec)
    xg = jnp.concatenate([xf, xb], axis=1).reshape(bsz, 4, nch, S5_GROUPS, S5_STATE)
    xg = xg.transpose(0, 3, 2, 1, 4).reshape(bsz, S5_GROUPS, nch, S5_FLAT)
    y = _s5_out(y_intra, xg, q)
    y = y.reshape(bsz, S5_GROUPS, nch, S5_CHUNK, S5_GROUP).transpose(0, 2, 3, 1, 4)
    return y.reshape(bsz, p, S5_WIDTH)


def _hg_kernel(lb_ref, qf_ref, ff_ref, vf_ref, qb_ref, fb_ref, vb_ref, of_ref, ob_ref, st_ref):
    i = pl.program_id(1)
    n = pl.num_programs(1)

    @pl.when(i == 0)
    def _():
        st_ref[...] = jnp.zeros_like(st_ref)

    lb = lb_ref[...]
    r = lax.broadcasted_iota(jnp.int32, (HG_CHUNK, HG_CHUNK), 0)
    c = lax.broadcasted_iota(jnp.int32, (HG_CHUNK, HG_CHUNK), 1)
    row = lax.broadcasted_iota(jnp.int32, (HG_CHUNK, 1), 0)
    passes = (
        (0, qf_ref, ff_ref, vf_ref, of_ref, i, c <= r, range(TILE // HG_CHUNK), HG_CHUNK - 1),
        (1, qb_ref, fb_ref, vb_ref, ob_ref, n - 1 - i, c >= r, reversed(range(TILE // HG_CHUNK)), 0),
    )
    for d, q_ref, f_ref, v_ref, o_ref, tile, keep, chunks, tot_row in passes:
        cum = keep.astype(F32)
        for ch in chunks:
            rows = slice(ch * HG_CHUNK, (ch + 1) * HG_CHUNK)
            q_pre = q_ref[0, rows, :]
            q = q_pre * jax.nn.sigmoid(q_pre)
            g = lb + (1.0 - lb) * jax.nn.sigmoid(f_ref[0, rows, :])
            valid = tile * TILE + ch * HG_CHUNK + row >= FRONT
            log_f = jnp.where(valid, jnp.log(g), 0.0)
            k = jnp.where(valid, 1.0 - g, 0.0)
            b = jnp.dot(cum, log_f, preferred_element_type=F32, precision=lax.Precision.HIGHEST)
            b_tot = b[tot_row:tot_row + 1]
            q_dec = (q * jnp.exp(b)).astype(BF16)
            k_inv = (k * jnp.exp(-b)).astype(BF16)
            k_end = (k * jnp.exp(b_tot - b)).astype(BF16)
            decay = jnp.exp(b_tot)
            v = v_ref[0, rows, :].astype(BF16)
            for h in range(HG_HEADS):
                ls = slice(h * HG_DIM, (h + 1) * HG_DIM)
                scores = lax.dot_general(q_dec[:, ls], k_inv[:, ls], _NT, preferred_element_type=F32)
                scores = jnp.where(keep, scores, 0.0).astype(BF16)
                st = st_ref[d, h]
                o = jnp.dot(scores, v[:, ls], preferred_element_type=F32)
                o += lax.dot_general(q_dec[:, ls], st.astype(BF16), _NT, preferred_element_type=F32)
                kv = lax.dot_general(v[:, ls], k_end[:, ls], _TN, preferred_element_type=F32)
                st_ref[d, h] = decay[:, ls] * st + kv
                o_ref[0, rows, ls] = o


def _hgrn2(q_pre, f_fwd, f_bwd, i_in, lb):
    bsz, p, _ = q_pre.shape
    nt = p // TILE
    fwd = lambda: pl.BlockSpec((1, TILE, HG_WIDTH), lambda b, i: (b, i, 0))
    bwd = lambda: pl.BlockSpec((1, TILE, HG_WIDTH), lambda b, i: (b, nt - 1 - i, 0))
    shape = jax.ShapeDtypeStruct((bsz, p, HG_WIDTH), F32)
    return pl.pallas_call(
        _hg_kernel,
        grid=(bsz, nt),
        in_specs=[pl.BlockSpec((1, HG_WIDTH), lambda b, i: (0, 0)), fwd(), fwd(), fwd(), bwd(), bwd(), bwd()],
        out_specs=[fwd(), bwd()],
        out_shape=[shape, shape],
        scratch_shapes=[pltpu.VMEM((2, HG_HEADS, HG_DIM, HG_DIM), F32)],
        compiler_params=_params("parallel", "arbitrary"),
        name="hgrn2",
    )(lb, q_pre, f_fwd, i_in, q_pre, f_bwd, i_in)


def _gelu(x):
    return 0.5 * x * (1.0 + lax.erf(x * (1.0 / math.sqrt(2.0))))


def _merge_kernel(x_ref, metap_ref, ys_ref, of_ref, ob_ref, gate_ref, wglu_ref, hgn_ref, whg_ref, wout_ref,
                  n2_ref, hs_ref, h2_ref):
    i = pl.program_id(1)
    hs = jnp.where(i == 0, metap_ref[...], x_ref[0])
    glu = jnp.dot(_gelu(ys_ref[0]).astype(BF16), wglu_ref[...], preferred_element_type=F32)
    y_a = glu[:, :D_MODEL] * jax.nn.sigmoid(glu[:, D_MODEL:])
    o = of_ref[0] + ob_ref[0]
    normed = []
    for h in range(HG_HEADS):
        oh = o[:, h * HG_DIM:(h + 1) * HG_DIM]
        normed.append(oh * lax.rsqrt(jnp.mean(oh * oh, axis=-1, keepdims=True) + EPS))
    o_gate = gate_ref[0, :, 0:HG_WIDTH]
    y_hg = jnp.concatenate(normed, axis=-1) * hgn_ref[...] * (o_gate * jax.nn.sigmoid(o_gate))
    y_b = jnp.dot(y_hg.astype(BF16), whg_ref[...], preferred_element_type=F32)
    gate_a = gate_ref[0, :, HG_WIDTH:HG_WIDTH + D_MODEL]
    gate_b = gate_ref[0, :, HG_WIDTH + D_MODEL:]
    mixed = jax.nn.sigmoid(gate_a) * y_a + jax.nn.sigmoid(gate_b) * y_b
    hs = hs + jnp.dot(mixed.astype(BF16), wout_ref[...], preferred_element_type=F32)
    hs_ref[0] = hs
    ms = jnp.mean(hs * hs, axis=-1, keepdims=True)
    h2_ref[0] = _pack(hs * lax.rsqrt(ms + EPS) * n2_ref[...])


def _merge(x, metap, y_s5, o_f, o_b, gates, w_glu, hg_norm_g, w_hg_out, w_out, norm2_g):
    bsz, p, _ = y_s5.shape
    nt = p // TILE
    tok = lambda width: pl.BlockSpec((1, TILE, width), lambda b, i: (b, i, 0))
    full = lambda a: pl.BlockSpec(a.shape, lambda b, i: (0,) * a.ndim)
    return pl.pallas_call(
        _merge_kernel,
        grid=(bsz, nt),
        in_specs=[
            pl.BlockSpec((1, TILE, D_MODEL), lambda b, i: (b, jnp.maximum(i - 1, 0), 0)),
            full(metap), tok(S5_WIDTH), tok(HG_WIDTH), tok(HG_WIDTH), tok(gates.shape[-1]),
            full(w_glu), full(hg_norm_g), full(w_hg_out), full(w_out), full(norm2_g),
        ],
        out_specs=[tok(D_MODEL), pl.BlockSpec((1, TILE // 2, D_MODEL), lambda b, i: (b, i, 0))],
        out_shape=[jax.ShapeDtypeStruct((bsz, p, D_MODEL), F32),
                   jax.ShapeDtypeStruct((bsz, p // 2, D_MODEL), jnp.uint32)],
        compiler_params=_params("parallel", "arbitrary"),
        name="merge",
    )(x, metap, y_s5, o_f, o_b, gates, w_glu, hg_norm_g, w_hg_out, w_out, norm2_g)


def _staircase():
    return [(i, PEER_TOPK // (i + 1)) for i in range(PEER_TOPK)]


def _top16(s):
    work = s
    rank = jnp.full(s.shape, float(PEER_TOPK), F32)
    tops = []
    for it in range(PEER_TOPK):
        m = jnp.max(work, axis=0, keepdims=True)
        hit = work == m
        rank = jnp.where(hit, float(it), rank)
        work = jnp.where(hit, -jnp.inf, work)
        tops.append(m)
    return tops, rank


def _paired_bf16_words(x):
    bits = pltpu.bitcast(x.astype(BF16).astype(F32), jnp.uint32)
    high = bits & jnp.uint32(0xFFFF0000)
    return high | (high >> 16)


def _score_kernel(h2_ref, wq_ref, keys_ref, r2_ref, p2_ref, cnt_ref, p1_ref, cand_ref):
    h2 = _unpack(h2_ref[...])
    for h in range(PEER_HEADS):
        qh = jnp.dot(h2, wq_ref[:, h * PEER_QDIM:(h + 1) * PEER_QDIM], preferred_element_type=F32).astype(BF16)
        s1 = lax.dot_general(keys_ref[h, 0], qh[:, :PEER_HALF], _NT, preferred_element_type=F32)
        s2 = lax.dot_general(keys_ref[h, 1], qh[:, PEER_HALF:], _NT, preferred_element_type=F32)
        t1, rank1 = _top16(s1)
        t2, rank2 = _top16(s2)
        t2_all = jnp.concatenate(t2, axis=0)
        cand_ref[...] = jnp.full(cand_ref.shape, -jnp.inf, F32)
        off = 0
        for i, n_i in _staircase():
            cand_ref[off:off + n_i, :] = t1[i] + t2_all[0:n_i]
            off += n_i
        cand = cand_ref[...]
        c_max = t1[0] + t2[0]
        work = cand
        z = jnp.zeros_like(c_max)
        tau = c_max
        for it in range(PEER_TOPK):
            tau = jnp.max(work, axis=0, keepdims=True)
            z = z + jnp.exp(tau - c_max)
            work = jnp.where(work == tau, -jnp.inf, work)
        chosen = (cand >= tau).astype(F32)
        cnt = jnp.zeros_like(s1)
        off = 0
        for i, n_i in _staircase():
            cnt_i = jnp.sum(chosen[off:off + n_i], axis=0, keepdims=True)
            cnt = jnp.where(rank1 == float(i), cnt_i, cnt)
            off += n_i
        r2_ref[h] = _pack(rank2)
        p2_ref[h] = _pack(jnp.exp(s2 - t2[0]))
        cnt_ref[h] = _paired_bf16_words(cnt)
        p1_ref[h] = _paired_bf16_words(jnp.exp(s1 - t1[0]) / z)


def _peer_scores(h2, w_q, keys):
    ntok = 2 * h2.shape[0]
    blk = lambda rows: pl.BlockSpec((PEER_HEADS, rows, TILE), lambda i: (0, 0, i))
    shape = lambda rows: jax.ShapeDtypeStruct((PEER_HEADS, rows, ntok), jnp.uint32)
    return pl.pallas_call(
        _score_kernel,
        grid=(ntok // TILE,),
        in_specs=[
            pl.BlockSpec((TILE // 2, D_MODEL), lambda i: (i, 0)),
            pl.BlockSpec(w_q.shape, lambda i: (0, 0)),
            pl.BlockSpec(keys.shape, lambda i: (0, 0, 0, 0)),
        ],
        out_specs=[blk(PEER_KEYS // 2), blk(PEER_KEYS // 2), blk(PEER_KEYS), blk(PEER_KEYS)],
        out_shape=[shape(PEER_KEYS // 2), shape(PEER_KEYS // 2), shape(PEER_KEYS), shape(PEER_KEYS)],
        scratch_shapes=[pltpu.VMEM((PEER_NCAND, TILE), F32)],
        compiler_params=_params("parallel"),
        name="peer_scores",
    )(h2, w_q, keys)


def _dense_kernel(h2_ref, u_ref, vt_ref, r2_ref, p2_ref, cnt_ref, p1_ref, o_ref, act_ref, wa_ref, *, tn):
    e = pl.program_id(1)
    act_ref[...] = lax.dot_general(_unpack(u_ref[...]), _unpack(h2_ref[...]), _NT, preferred_element_type=F32)
    n_first = PEER_ETILE // PEER_KEYS
    first = pl.multiple_of(e * n_first, 8)
    zero = jnp.zeros((), BF16)
    for lc in range(tn // 128):
        lanes = slice(lc * 128, (lc + 1) * 128)
        cnt8 = [cnt_ref[h, pl.ds(first, n_first), lanes] for h in range(PEER_HEADS)]
        p18 = [p1_ref[h, pl.ds(first, n_first), lanes] for h in range(PEER_HEADS)]
        for j in range(n_first):
            w = jnp.zeros((PEER_KEYS, 128), BF16)
            for h in range(PEER_HEADS):
                cnt = pltpu.bitcast(jnp.broadcast_to(cnt8[h][j:j + 1], (PEER_KEYS // 2, 128)), BF16)
                p1 = pltpu.bitcast(jnp.broadcast_to(p18[h][j:j + 1], (PEER_KEYS // 2, 128)), BF16)
                w += jnp.where(_unpack(r2_ref[h, :, lanes]) < cnt, _unpack(p2_ref[h, :, lanes]), zero) * p1
            rows = slice(j * PEER_KEYS, (j + 1) * PEER_KEYS)
            wa_ref[rows, lanes] = w * _gelu(act_ref[rows, lanes]).astype(BF16)
    contrib = jnp.dot(_unpack(vt_ref[...]), wa_ref[...], preferred_element_type=F32)

    @pl.when(e == 0)
    def _():
        o_ref[...] = contrib

    @pl.when(e > 0)
    def _():
        o_ref[...] += contrib


def _peer_dense(h2, u_tab, vt_tab, r2, p2, cnt, p1):
    ntok = 2 * h2.shape[0]
    tn = _largest_divisor(ntok, 768, 256)
    nexp = 2 * u_tab.shape[0]
    sel = lambda rows: pl.BlockSpec((PEER_HEADS, rows, tn), lambda t, e: (0, 0, t))
    return pl.pallas_call(
        functools.partial(_dense_kernel, tn=tn),
        grid=(ntok // tn, nexp // PEER_ETILE),
        in_specs=[
            pl.BlockSpec((tn // 2, D_MODEL), lambda t, e: (t, 0)),
            pl.BlockSpec((PEER_ETILE // 2, D_MODEL), lambda t, e: (e, 0)),
            pl.BlockSpec((D_MODEL // 2, PEER_ETILE), lambda t, e: (0, e)),
            sel(PEER_KEYS // 2), sel(PEER_KEYS // 2), sel(PEER_KEYS), sel(PEER_KEYS),
        ],
        out_specs=pl.BlockSpec((D_MODEL, tn), lambda t, e: (0, t)),
        out_shape=jax.ShapeDtypeStruct((D_MODEL, ntok), F32),
        scratch_shapes=[pltpu.VMEM((PEER_ETILE, tn), F32), pltpu.VMEM((PEER_ETILE, tn), BF16)],
        compiler_params=_params("parallel", "arbitrary"),
        name="peer_dense",
    )(h2, u_tab, vt_tab, r2, p2, cnt, p1)


def _final_kernel(ot_ref, hs_ref, g_ref, y_ref):
    hs = hs_ref[...] + ot_ref[...].T
    ms = jnp.mean(hs * hs, axis=-1, keepdims=True)
    y_ref[0] = hs * lax.rsqrt(ms + EPS) * g_ref[...]


def _final(o_t, hs, final_g, first_seq, n_seq, nt):
    tile_of = lambda b, i: (b + first_seq) * nt + i + 1
    return pl.pallas_call(
        _final_kernel,
        grid=(n_seq, nt - 1),
        in_specs=[
            pl.BlockSpec((D_MODEL, TILE), lambda b, i: (0, tile_of(b, i))),
            pl.BlockSpec((TILE, D_MODEL), lambda b, i: (tile_of(b, i), 0)),
            pl.BlockSpec((1, D_MODEL), lambda b, i: (0, 0)),
        ],
        out_specs=pl.BlockSpec((1, TILE, D_MODEL), lambda b, i: (b, i, 0)),
        out_shape=jax.ShapeDtypeStruct((n_seq, (nt - 1) * TILE, D_MODEL), F32),
        compiler_params=_params("parallel", "parallel"),
        name="final",
    )(o_t, hs, final_g)


def kernel(x_prompt, x_sample, meta, norm1_g, w_in, s5_lam_re, s5_lam_im, s5_log_step, s5_b_re, s5_b_im,
           s5_c_re, s5_c_im, s5_d, w_glu, hg_lb, hg_norm_g, w_hg_out, w_out, norm2_g, peer_wq, peer_keys,
           peer_u, peer_v, final_g):
    assert x_prompt.shape[1] == x_sample.shape[1] and x_prompt.shape[1] % TILE == 0
    assert norm1_g.shape[0] == 1, "single-layer trunk"
    n_prompt = x_prompt.shape[0]
    x = jnp.concatenate([x_prompt, x_sample], axis=0).astype(F32)
    metap = jnp.concatenate([jnp.zeros((FRONT, D_MODEL), F32), meta.astype(F32)], axis=0)
    row = lambda a: a.astype(F32).reshape(1, -1)

    u, q_pre, f_fwd, f_bwd, i_in, gates = _inproj(x, metap, row(norm1_g[0]), w_in[0].astype(BF16))

    w1, q_s5, dec = _s5_weights(s5_lam_re[0], s5_lam_im[0], s5_log_step[0], s5_b_re[0], s5_b_im[0],
                                s5_c_re[0], s5_c_im[0], s5_d[0])
    y_s5 = _s5_mixer(u, w1, q_s5, dec)

    lbs = jnp.cumsum(jax.nn.softmax(hg_lb.astype(F32), axis=0), axis=0)
    o_f, o_b = _hgrn2(q_pre, f_fwd, f_bwd, i_in, row(lbs[0]))

    hs, h2 = _merge(x, metap, y_s5, o_f, o_b, gates, w_glu[0].astype(BF16), row(hg_norm_g[0]),
                    w_hg_out[0].astype(BF16), w_out[0].astype(BF16), row(norm2_g[0]))

    bsz, p, _ = hs.shape
    hs = hs.reshape(bsz * p, D_MODEL)
    h2 = h2.reshape(bsz * p // 2, D_MODEL)
    r2, p2, cnt, p1 = _peer_scores(h2, peer_wq[0].astype(BF16), peer_keys[0].astype(BF16))
    o_t = _peer_dense(h2, _pack_rows(peer_u[0]), _pack_rows(peer_v[0].T), r2, p2, cnt, p1)
    fin = functools.partial(_final, o_t, hs, row(final_g), nt=p // TILE)
    return (fin(first_seq=0, n_seq=n_prompt), fin(first_seq=n_prompt, n_seq=bsz - n_prompt))
```

```python
import functools
import math

import jax
import jax.numpy as jnp
from jax import lax
from jax.experimental import pallas as pl
from jax.experimental.pallas import tpu as pltpu

F32 = jnp.float32
BF16 = jnp.bfloat16

D_MODEL = 1024
N_META = 16
EPS = 1e-6
TILE = 256
FRONT = TILE - N_META

S5_WIDTH = 512
S5_GROUP = 16
S5_GROUPS = 32
S5_STATE = 64
S5_CHUNK = 16
S5_LB = 128 // S5_GROUP
S5_NLB = S5_GROUPS // S5_LB
S5_K = S5_CHUNK * 128

HG_WIDTH = 512
HG_HEADS = 4
HG_DIM = 128
HG_CHUNK = 64

PEER_HEADS = 8
PEER_KEYS = 128
PEER_TOPK = 16
PEER_QDIM = 256
PEER_HALF = 128
PEER_ETILE = 1024
PEER_NCAND = 56

VMEM_LIMIT = 56 * 1024 * 1024

_NT = (((1,), (1,)), ((), ()))
_TN = (((0,), (0,)), ((), ()))


def _largest_divisor(n, target, multiple=1):
    best = None
    for d in range(multiple, min(n, target) + 1, multiple):
        if n % d == 0:
            best = d
    assert best is not None, (n, target, multiple)
    return best


def _pack(x):
    return pltpu.bitcast(x.astype(BF16), jnp.uint32)


def _unpack(x):
    return pltpu.bitcast(x, BF16)


def _pack_rows(w):
    rows, cols = w.shape
    pairs = w.astype(BF16).reshape(rows // 2, 2, cols).swapaxes(-1, -2)
    return lax.bitcast_convert_type(pairs, jnp.uint32)


def _params(*sem):
    return pltpu.CompilerParams(dimension_semantics=sem, vmem_limit_bytes=VMEM_LIMIT)


def _hidden_tile(xp_ref, xs_ref, metap_ref, n_prompt):
    b, i = pl.program_id(0), pl.program_id(1)
    x = jnp.where(b < n_prompt, xp_ref[0], xs_ref[0])
    return jnp.where(i == 0, metap_ref[...], x)


def _x_specs(n_prompt, nt):
    tile = lambda i: jnp.maximum(i - 1, 0)
    prompt = pl.BlockSpec((1, TILE, D_MODEL), lambda b, i: (
        jnp.minimum(b, n_prompt - 1), jnp.where(b < n_prompt, tile(i), nt - 2), 0))
    sample = pl.BlockSpec((1, TILE, D_MODEL), lambda b, i: (
        jnp.maximum(b - n_prompt, 0), jnp.where(b < n_prompt, 0, tile(i)), 0))
    return [prompt, sample]


def _inproj_kernel(xp_ref, xs_ref, metap_ref, g_ref, w_ref, u_ref, q_ref, ff_ref, fb_ref, v_ref, gate_ref, *,
                   n_prompt):
    i = pl.program_id(1)
    hs = _hidden_tile(xp_ref, xs_ref, metap_ref, n_prompt)
    ms = jnp.mean(hs * hs, axis=-1, keepdims=True)
    h = (hs * lax.rsqrt(ms + EPS) * g_ref[...]).astype(BF16)
    row = lax.broadcasted_iota(jnp.int32, (TILE, 1), 0)
    valid = jnp.logical_or(i > 0, row >= FRONT)
    u = jnp.dot(h, w_ref[:, 0:S5_WIDTH], preferred_element_type=F32)
    u_ref[0] = jnp.where(valid, u, 0.0)
    off = S5_WIDTH
    for ref in (q_ref, ff_ref, fb_ref, v_ref):
        ref[0] = jnp.dot(h, w_ref[:, off:off + HG_WIDTH], preferred_element_type=F32)
        off += HG_WIDTH
    gate_ref[0] = jnp.dot(h, w_ref[:, off:], preferred_element_type=F32)


def _inproj(x_prompt, x_sample, metap, g, w_in):
    n_prompt, length, _ = x_prompt.shape
    bsz = n_prompt + x_sample.shape[0]
    nt = length // TILE + 1
    p = nt * TILE
    ncols = w_in.shape[1]
    ngate = ncols - S5_WIDTH - 4 * HG_WIDTH
    tok = lambda width: pl.BlockSpec((1, TILE, width), lambda b, i: (b, i, 0))
    full = lambda shape: pl.BlockSpec(shape, lambda b, i: (0,) * len(shape))
    hg_shape = jax.ShapeDtypeStruct((bsz, p, HG_WIDTH), F32)
    return pl.pallas_call(
        functools.partial(_inproj_kernel, n_prompt=n_prompt),
        grid=(bsz, nt),
        in_specs=_x_specs(n_prompt, nt) + [full((TILE, D_MODEL)), full((1, D_MODEL)), full((D_MODEL, ncols))],
        out_specs=[tok(S5_WIDTH), tok(HG_WIDTH), tok(HG_WIDTH), tok(HG_WIDTH), tok(HG_WIDTH), tok(ngate)],
        out_shape=[hg_shape, hg_shape, hg_shape, hg_shape, hg_shape, jax.ShapeDtypeStruct((bsz, p, ngate), F32)],
        compiler_params=_params("arbitrary", "arbitrary"),
        name="inproj",
    )(x_prompt, x_sample, metap, g, w_in)


def _s5_weights(lam_re, lam_im, log_step, b_re, b_im, c_re, c_im, d_skip):
    hi = lax.Precision.HIGHEST
    f = lambda a: a.astype(F32)
    lam_re, lam_im, log_step = f(lam_re), f(lam_im), f(log_step)
    b_re, b_im, c_re, c_im, d_skip = f(b_re), f(b_im), f(c_re), f(c_im), f(d_skip)
    step = jnp.exp(log_step)[:, :, None]
    pw = jnp.arange(S5_CHUNK + 1, dtype=F32)[:, None, None, None]
    mag = jnp.exp(pw * lam_re[None] * step[None])
    ang = pw * lam_im[None] * step[None]
    a_re, a_im = mag * jnp.cos(ang), mag * jnp.sin(ang)
    den = lam_re * lam_re + lam_im * lam_im
    n_re, n_im = a_re[1] - 1.0, a_im[1]
    coef_re = (n_re * lam_re + n_im * lam_im) / den
    coef_im = (n_im * lam_re - n_re * lam_im) / den
    w_re = coef_re[None] * a_re - coef_im[None] * a_im
    w_im = coef_re[None] * a_im + coef_im[None] * a_re

    def conv_taps(direction):
        wr, wi = w_re[:S5_CHUNK, direction], w_im[:S5_CHUNK, direction]
        m_re = wr[..., None] * b_re[None] - wi[..., None] * b_im[None]
        m_im = wr[..., None] * b_im[None] + wi[..., None] * b_re[None]
        return (jnp.einsum('gon,dgni->dgoi', c_re, m_re, precision=hi)
                - jnp.einsum('gon,dgni->dgoi', c_im, m_im, precision=hi))

    k_f, k_b = conv_taps(0), conv_taps(1)
    center = k_f[0] + k_b[0] + jnp.eye(S5_GROUP, dtype=F32)[None] * d_skip.reshape(S5_GROUPS, S5_GROUP, 1)
    taps = jnp.concatenate([k_b[:0:-1], center[None], k_f[1:]], axis=0)
    t_idx = jnp.arange(S5_CHUNK)
    toe = taps[t_idx[None, :] - t_idx[:, None] + S5_CHUNK - 1]
    eye = jnp.eye(S5_LB, dtype=F32)
    split = lambda a, axis: a.reshape(a.shape[:axis] + (S5_NLB, S5_LB) + a.shape[axis + 1:])

    toe = jnp.einsum('stbgoi,gh->bsgitho', split(toe, 2), eye).reshape(S5_NLB, S5_K, S5_K)

    def end_map(direction, powers):
        pr, pi = a_re[powers, direction], a_im[powers, direction]
        return [pr[..., None] * b_re[None] - pi[..., None] * b_im[None],
                pr[..., None] * b_im[None] + pi[..., None] * b_re[None]]

    ends = jnp.stack(end_map(0, S5_CHUNK - 1 - t_idx) + end_map(1, t_idx))
    ends = jnp.einsum('ksbgni,gh->bsgikhn', split(ends, 2), eye).reshape(S5_NLB, S5_K, S5_K)

    def out_map(direction, powers):
        wr, wi = w_re[powers, direction], w_im[powers, direction]
        return [c_re[None] * wr[:, :, None, :] - c_im[None] * wi[:, :, None, :],
                -c_re[None] * wi[:, :, None, :] - c_im[None] * wr[:, :, None, :]]

    outs = jnp.stack(out_map(0, t_idx + 1) + out_map(1, S5_CHUNK - t_idx))
    outs = jnp.einsum('ktbgon,gh->bkgntho', split(outs, 2), eye).reshape(S5_NLB, S5_K, S5_K)

    def decay(direction):
        ar = split(a_re[S5_CHUNK, direction], 0).reshape(S5_NLB, 1, S5_K // 4)
        ai = split(a_im[S5_CHUNK, direction], 0).reshape(S5_NLB, 1, S5_K // 4)
        same = jnp.concatenate([ar, ar], axis=2).reshape(1, S5_NLB * S5_K // 2)
        cross = jnp.concatenate([-ai, ai], axis=2).reshape(1, S5_NLB * S5_K // 2)
        return same, cross

    return toe.astype(BF16), ends.astype(BF16), outs.astype(BF16), decay(0) + decay(1)


def _chunk_rows(u_ref, nc):
    parts = [u_ref[0, pl.ds(s, nc, stride=S5_CHUNK), :] for s in range(S5_CHUNK)]
    return jnp.concatenate(parts, axis=1).astype(BF16)


def _s5_local_kernel(u_ref, w_ref, ef_ref, eb_ref, *, nc):
    e = jnp.dot(_chunk_rows(u_ref, nc), w_ref[0], preferred_element_type=F32)
    ef_ref[0] = e[:, :S5_K // 2]
    eb_ref[0] = e[:, S5_K // 2:]


def _s5_local(u, ends, nc):
    bsz, p, _ = u.shape
    nch = p // S5_CHUNK
    out = lambda: pl.BlockSpec((1, nc, S5_K // 2), lambda lb, b, i: (b, i, lb))
    shape = jax.ShapeDtypeStruct((bsz, nch, S5_NLB * S5_K // 2), F32)
    return pl.pallas_call(
        functools.partial(_s5_local_kernel, nc=nc),
        grid=(S5_NLB, bsz, nch // nc),
        in_specs=[pl.BlockSpec((1, nc * S5_CHUNK, 128), lambda lb, b, i: (b, i, lb)),
                  pl.BlockSpec((1, S5_K, S5_K), lambda lb, b, i: (lb, 0, 0))],
        out_specs=[out(), out()],
        out_shape=[shape, shape],
        compiler_params=_params("arbitrary", "arbitrary", "arbitrary"),
        name="s5_local",
    )(u, ends)


def _swap_re_im(x):
    half = S5_K // 4
    parts = []
    for lb in range(x.shape[1] // (2 * half)):
        parts += [x[:, (2 * lb + 1) * half:(2 * lb + 2) * half], x[:, 2 * lb * half:(2 * lb + 1) * half]]
    return jnp.concatenate(parts, axis=1)


def _s5_scan_kernel(fs_ref, fc_ref, bs_ref, bc_ref, ef_ref, eb_ref, xf_ref, xb_ref, st_ref, *, ct):
    j = pl.program_id(1)

    @pl.when(j == 0)
    def _():
        st_ref[...] = jnp.zeros_like(st_ref)

    width = S5_K
    for part in range(st_ref.shape[2] // width):
        lanes = slice(part * width, (part + 1) * width)
        f_same, f_cross = fs_ref[:, lanes], fc_ref[:, lanes]
        b_same, b_cross = bs_ref[:, lanes], bc_ref[:, lanes]

        def body(c, carry):
            sf, sb = carry
            xf_ref[0, pl.ds(c, 1), lanes] = sf
            sf = f_same * sf + f_cross * _swap_re_im(sf) + ef_ref[0, pl.ds(c, 1), lanes]
            cb = ct - 1 - c
            xb_ref[0, pl.ds(cb, 1), lanes] = sb
            sb = b_same * sb + b_cross * _swap_re_im(sb) + eb_ref[0, pl.ds(cb, 1), lanes]
            return sf, sb

        sf, sb = lax.fori_loop(0, ct, body, (st_ref[0, :, lanes], st_ref[1, :, lanes]))
        st_ref[0, :, lanes] = sf
        st_ref[1, :, lanes] = sb


def _s5_scan(ef, eb, dec):
    bsz, nch, width = ef.shape
    ct = _largest_divisor(nch, 208, 8)
    nj = nch // ct
    fwd = lambda: pl.BlockSpec((1, ct, width), lambda b, j: (b, j, 0))
    bwd = lambda: pl.BlockSpec((1, ct, width), lambda b, j: (b, nj - 1 - j, 0))
    row = lambda: pl.BlockSpec((1, width), lambda b, j: (0, 0))
    shape = jax.ShapeDtypeStruct(ef.shape, F32)
    return pl.pallas_call(
        functools.partial(_s5_scan_kernel, ct=ct),
        grid=(bsz, nj),
        in_specs=[row(), row(), row(), row(), fwd(), bwd()],
        out_specs=[fwd(), bwd()],
        out_shape=[shape, shape],
        scratch_shapes=[pltpu.VMEM((2, 1, width), F32)],
        compiler_params=_params("parallel", "arbitrary"),
        name="s5_scan",
    )(*dec, ef, eb)


def _s5_out_kernel(u_ref, xf_ref, xb_ref, toe_ref, outs_ref, y_ref, *, nc):
    states = jnp.concatenate([xf_ref[0], xb_ref[0]], axis=1).astype(BF16)
    y = jnp.dot(_chunk_rows(u_ref, nc), toe_ref[0], preferred_element_type=F32)
    y += jnp.dot(states, outs_ref[0], preferred_element_type=F32)
    for t in range(S5_CHUNK):
        y_ref[0, pl.ds(t, nc, stride=S5_CHUNK), :] = y[:, t * 128:(t + 1) * 128]


def _s5_out(u, xf, xb, toe, outs, nc):
    bsz, p, _ = u.shape
    nch = p // S5_CHUNK
    tok = lambda: pl.BlockSpec((1, nc * S5_CHUNK, 128), lambda lb, b, i: (b, i, lb))
    state = lambda: pl.BlockSpec((1, nc, S5_K // 2), lambda lb, b, i: (b, i, lb))
    weight = lambda: pl.BlockSpec((1, S5_K, S5_K), lambda lb, b, i: (lb, 0, 0))
    return pl.pallas_call(
        functools.partial(_s5_out_kernel, nc=nc),
        grid=(S5_NLB, bsz, nch // nc),
        in_specs=[tok(), state(), state(), weight(), weight()],
        out_specs=tok(),
        out_shape=jax.ShapeDtypeStruct(u.shape, F32),
        compiler_params=_params("arbitrary", "arbitrary", "arbitrary"),
        name="s5_out",
    )(u, xf, xb, toe, outs)


def _s5_mixer(u, toe, ends, outs, dec):
    nch = u.shape[1] // S5_CHUNK
    nc = _largest_divisor(nch, 260, 8)
    ef, eb = _s5_local(u, ends, nc)
    xf, xb = _s5_scan(ef, eb, dec)
    return _s5_out(u, xf, xb, toe, outs, nc)


def _hg_kernel(lb_ref, qf_ref, ff_ref, vf_ref, qb_ref, fb_ref, vb_ref, of_ref, ob_ref, st_ref):
    i = pl.program_id(1)
    n = pl.num_programs(1)

    @pl.when(i == 0)
    def _():
        st_ref[...] = jnp.zeros_like(st_ref)

    lb = lb_ref[...]
    r = lax.broadcasted_iota(jnp.int32, (HG_CHUNK, HG_CHUNK), 0)
    c = lax.broadcasted_iota(jnp.int32, (HG_CHUNK, HG_CHUNK), 1)
    row = lax.broadcasted_iota(jnp.int32, (HG_CHUNK, 1), 0)
    passes = (
        (0, qf_ref, ff_ref, vf_ref, of_ref, i, c <= r, range(TILE // HG_CHUNK), HG_CHUNK - 1),
        (1, qb_ref, fb_ref, vb_ref, ob_ref, n - 1 - i, c >= r, reversed(range(TILE // HG_CHUNK)), 0),
    )
    for d, q_ref, f_ref, v_ref, o_ref, tile, keep, chunks, tot_row in passes:
        cum = keep.astype(F32)
        for ch in chunks:
            rows = slice(ch * HG_CHUNK, (ch + 1) * HG_CHUNK)
            q_pre = q_ref[0, rows, :]
            q = q_pre * jax.nn.sigmoid(q_pre)
            g = lb + (1.0 - lb) * jax.nn.sigmoid(f_ref[0, rows, :])
            valid = tile * TILE + ch * HG_CHUNK + row >= FRONT
            log_f = jnp.where(valid, jnp.log(g), 0.0)
            k = jnp.where(valid, 1.0 - g, 0.0)
            b = jnp.dot(cum, log_f, preferred_element_type=F32, precision=lax.Precision.HIGHEST)
            b_tot = b[tot_row:tot_row + 1]
            q_dec = (q * jnp.exp(b)).astype(BF16)
            k_inv = (k * jnp.exp(-b)).astype(BF16)
            k_end = (k * jnp.exp(b_tot - b)).astype(BF16)
            decay = jnp.exp(b_tot)
            v = v_ref[0, rows, :].astype(BF16)
            for h in range(HG_HEADS):
                ls = slice(h * HG_DIM, (h + 1) * HG_DIM)
                scores = lax.dot_general(q_dec[:, ls], k_inv[:, ls], _NT, preferred_element_type=F32)
                scores = jnp.where(keep, scores, 0.0).astype(BF16)
                st = st_ref[d, h]
                o = jnp.dot(scores, v[:, ls], preferred_element_type=F32)
                o += lax.dot_general(q_dec[:, ls], st.astype(BF16), _NT, preferred_element_type=F32)
                kv = lax.dot_general(v[:, ls], k_end[:, ls], _TN, preferred_element_type=F32)
                st_ref[d, h] = decay[:, ls] * st + kv
                o_ref[0, rows, ls] = o


def _hgrn2(q_pre, f_fwd, f_bwd, i_in, lb):
    bsz, p, _ = q_pre.shape
    nt = p // TILE
    fwd = lambda: pl.BlockSpec((1, TILE, HG_WIDTH), lambda b, i: (b, i, 0))
    bwd = lambda: pl.BlockSpec((1, TILE, HG_WIDTH), lambda b, i: (b, nt - 1 - i, 0))
    shape = jax.ShapeDtypeStruct((bsz, p, HG_WIDTH), F32)
    return pl.pallas_call(
        _hg_kernel,
        grid=(bsz, nt),
        in_specs=[pl.BlockSpec((1, HG_WIDTH), lambda b, i: (0, 0)), fwd(), fwd(), fwd(), bwd(), bwd(), bwd()],
        out_specs=[fwd(), bwd()],
        out_shape=[shape, shape],
        scratch_shapes=[pltpu.VMEM((2, HG_HEADS, HG_DIM, HG_DIM), F32)],
        compiler_params=_params("parallel", "arbitrary"),
        name="hgrn2",
    )(lb, q_pre, f_fwd, i_in, q_pre, f_bwd, i_in)


def _gelu(x):
    return 0.5 * x * (1.0 + lax.erf(x * (1.0 / math.sqrt(2.0))))


def _merge_kernel(xp_ref, xs_ref, metap_ref, ys_ref, of_ref, ob_ref, gate_ref, wglu_ref, hgn_ref, whg_ref,
                  wout_ref, n2_ref, hs_ref, h2_ref, *, n_prompt):
    hs = _hidden_tile(xp_ref, xs_ref, metap_ref, n_prompt)
    glu = jnp.dot(_gelu(ys_ref[0]).astype(BF16), wglu_ref[...], preferred_element_type=F32)
    y_a = glu[:, :D_MODEL] * jax.nn.sigmoid(glu[:, D_MODEL:])
    o = of_ref[0] + ob_ref[0]
    normed = []
    for h in range(HG_HEADS):
        oh = o[:, h * HG_DIM:(h + 1) * HG_DIM]
        normed.append(oh * lax.rsqrt(jnp.mean(oh * oh, axis=-1, keepdims=True) + EPS))
    o_gate = gate_ref[0, :, 0:HG_WIDTH]
    y_hg = jnp.concatenate(normed, axis=-1) * hgn_ref[...] * (o_gate * jax.nn.sigmoid(o_gate))
    y_b = jnp.dot(y_hg.astype(BF16), whg_ref[...], preferred_element_type=F32)
    gate_a = gate_ref[0, :, HG_WIDTH:HG_WIDTH + D_MODEL]
    gate_b = gate_ref[0, :, HG_WIDTH + D_MODEL:]
    mixed = jax.nn.sigmoid(gate_a) * y_a + jax.nn.sigmoid(gate_b) * y_b
    hs = hs + jnp.dot(mixed.astype(BF16), wout_ref[...], preferred_element_type=F32)
    hs_ref[0] = hs
    ms = jnp.mean(hs * hs, axis=-1, keepdims=True)
    h2_ref[0] = _pack(hs * lax.rsqrt(ms + EPS) * n2_ref[...])


def _merge(x_prompt, x_sample, metap, y_s5, o_f, o_b, gates, w_glu, hg_norm_g, w_hg_out, w_out, norm2_g):
    bsz, p, _ = y_s5.shape
    nt = p // TILE
    n_prompt = x_prompt.shape[0]
    tok = lambda width: pl.BlockSpec((1, TILE, width), lambda b, i: (b, i, 0))
    full = lambda a: pl.BlockSpec(a.shape, lambda b, i: (0,) * a.ndim)
    return pl.pallas_call(
        functools.partial(_merge_kernel, n_prompt=n_prompt),
        grid=(bsz, nt),
        in_specs=_x_specs(n_prompt, nt) + [
            full(metap), tok(S5_WIDTH), tok(HG_WIDTH), tok(HG_WIDTH), tok(gates.shape[-1]),
            full(w_glu), full(hg_norm_g), full(w_hg_out), full(w_out), full(norm2_g),
        ],
        out_specs=[tok(D_MODEL), pl.BlockSpec((1, TILE // 2, D_MODEL), lambda b, i: (b, i, 0))],
        out_shape=[jax.ShapeDtypeStruct((bsz, p, D_MODEL), F32),
                   jax.ShapeDtypeStruct((bsz, p // 2, D_MODEL), jnp.uint32)],
        compiler_params=_params("arbitrary", "arbitrary"),
        name="merge",
    )(x_prompt, x_sample, metap, y_s5, o_f, o_b, gates, w_glu, hg_norm_g, w_hg_out, w_out, norm2_g)


def _staircase():
    return [(i, PEER_TOPK // (i + 1)) for i in range(PEER_TOPK)]


def _top16(s):
    work = s
    rank = jnp.full(s.shape, float(PEER_TOPK), F32)
    tops = []
    for it in range(PEER_TOPK):
        m = jnp.max(work, axis=0, keepdims=True)
        hit = work == m
        rank = jnp.where(hit, float(it), rank)
        work = jnp.where(hit, -jnp.inf, work)
        tops.append(m)
    return tops, rank


def _paired_bf16_words(x):
    bits = pltpu.bitcast(x.astype(BF16).astype(F32), jnp.uint32)
    high = bits & jnp.uint32(0xFFFF0000)
    return high | (high >> 16)


def _score_kernel(h2_ref, wq_ref, keys_ref, r2_ref, p2_ref, cnt_ref, p1_ref, cand_ref):
    h2 = _unpack(h2_ref[...])
    for h in range(PEER_HEADS):
        qh = jnp.dot(h2, wq_ref[:, h * PEER_QDIM:(h + 1) * PEER_QDIM], preferred_element_type=F32).astype(BF16)
        s1 = lax.dot_general(keys_ref[h, 0], qh[:, :PEER_HALF], _NT, preferred_element_type=F32)
        s2 = lax.dot_general(keys_ref[h, 1], qh[:, PEER_HALF:], _NT, preferred_element_type=F32)
        t1, rank1 = _top16(s1)
        t2, rank2 = _top16(s2)
        t2_all = jnp.concatenate(t2, axis=0)
        cand_ref[...] = jnp.full(cand_ref.shape, -jnp.inf, F32)
        off = 0
        for i, n_i in _staircase():
            cand_ref[off:off + n_i, :] = t1[i] + t2_all[0:n_i]
            off += n_i
        cand = cand_ref[...]
        c_max = t1[0] + t2[0]
        work = cand
        z = jnp.zeros_like(c_max)
        tau = c_max
        for it in range(PEER_TOPK):
            tau = jnp.max(work, axis=0, keepdims=True)
            z = z + jnp.exp(tau - c_max)
            work = jnp.where(work == tau, -jnp.inf, work)
        chosen = (cand >= tau).astype(F32)
        cnt = jnp.zeros_like(s1)
        off = 0
        for i, n_i in _staircase():
            cnt_i = jnp.sum(chosen[off:off + n_i], axis=0, keepdims=True)
            cnt = jnp.where(rank1 == float(i), cnt_i, cnt)
            off += n_i
        r2_ref[h] = _pack(rank2)
        p2_ref[h] = _pack(jnp.exp(s2 - t2[0]))
        cnt_ref[h] = _paired_bf16_words(cnt)
        p1_ref[h] = _paired_bf16_words(jnp.exp(s1 - t1[0]) / z)


def _peer_scores(h2, w_q, keys):
    ntok = 2 * h2.shape[0]
    blk = lambda rows: pl.BlockSpec((PEER_HEADS, rows, TILE), lambda i: (0, 0, i))
    shape = lambda rows: jax.ShapeDtypeStruct((PEER_HEADS, rows, ntok), jnp.uint32)
    return pl.pallas_call(
        _score_kernel,
        grid=(ntok // TILE,),
        in_specs=[
            pl.BlockSpec((TILE // 2, D_MODEL), lambda i: (i, 0)),
            pl.BlockSpec(w_q.shape, lambda i: (0, 0)),
            pl.BlockSpec(keys.shape, lambda i: (0, 0, 0, 0)),
        ],
        out_specs=[blk(PEER_KEYS // 2), blk(PEER_KEYS // 2), blk(PEER_KEYS), blk(PEER_KEYS)],
        out_shape=[shape(PEER_KEYS // 2), shape(PEER_KEYS // 2), shape(PEER_KEYS), shape(PEER_KEYS)],
        scratch_shapes=[pltpu.VMEM((PEER_NCAND, TILE), F32)],
        compiler_params=_params("parallel"),
        name="peer_scores",
    )(h2, w_q, keys)


def _dense_kernel(h2_ref, u_ref, vt_ref, r2_ref, p2_ref, cnt_ref, p1_ref, o_ref, act_ref, wa_ref, *, tn):
    e = pl.program_id(1)
    act_ref[...] = lax.dot_general(_unpack(u_ref[...]), _unpack(h2_ref[...]), _NT, preferred_element_type=F32)
    n_first = PEER_ETILE // PEER_KEYS
    first = pl.multiple_of(e * n_first, 8)
    zero = jnp.zeros((), BF16)
    for lc in range(tn // 128):
        lanes = slice(lc * 128, (lc + 1) * 128)
        cnt8 = [cnt_ref[h, pl.ds(first, n_first), lanes] for h in range(PEER_HEADS)]
        p18 = [p1_ref[h, pl.ds(first, n_first), lanes] for h in range(PEER_HEADS)]
        for j in range(n_first):
            w = jnp.zeros((PEER_KEYS, 128), BF16)
            for h in range(PEER_HEADS):
                cnt = pltpu.bitcast(jnp.broadcast_to(cnt8[h][j:j + 1], (PEER_KEYS // 2, 128)), BF16)
                p1 = pltpu.bitcast(jnp.broadcast_to(p18[h][j:j + 1], (PEER_KEYS // 2, 128)), BF16)
                w += jnp.where(_unpack(r2_ref[h, :, lanes]) < cnt, _unpack(p2_ref[h, :, lanes]), zero) * p1
            rows = slice(j * PEER_KEYS, (j + 1) * PEER_KEYS)
            wa_ref[rows, lanes] = w * _gelu(act_ref[rows, lanes]).astype(BF16)
    contrib = jnp.dot(_unpack(vt_ref[...]), wa_ref[...], preferred_element_type=F32)

    @pl.when(e == 0)
    def _():
        o_ref[...] = contrib

    @pl.when(e > 0)
    def _():
        o_ref[...] += contrib


def _peer_dense(h2, u_tab, vt_tab, r2, p2, cnt, p1):
    ntok = 2 * h2.shape[0]
    tn = _largest_divisor(ntok, 768, 256)
    nexp = 2 * u_tab.shape[0]
    sel = lambda rows: pl.BlockSpec((PEER_HEADS, rows, tn), lambda t, e: (0, 0, t))
    return pl.pallas_call(
        functools.partial(_dense_kernel, tn=tn),
        grid=(ntok // tn, nexp // PEER_ETILE),
        in_specs=[
            pl.BlockSpec((tn // 2, D_MODEL), lambda t, e: (t, 0)),
            pl.BlockSpec((PEER_ETILE // 2, D_MODEL), lambda t, e: (e, 0)),
            pl.BlockSpec((D_MODEL // 2, PEER_ETILE), lambda t, e: (0, e)),
            sel(PEER_KEYS // 2), sel(PEER_KEYS // 2), sel(PEER_KEYS), sel(PEER_KEYS),
        ],
        out_specs=pl.BlockSpec((D_MODEL, tn), lambda t, e: (0, t)),
        out_shape=jax.ShapeDtypeStruct((D_MODEL, ntok), F32),
        scratch_shapes=[pltpu.VMEM((PEER_ETILE, tn), F32), pltpu.VMEM((PEER_ETILE, tn), BF16)],
        compiler_params=_params("parallel", "arbitrary"),
        name="peer_dense",
    )(h2, u_tab, vt_tab, r2, p2, cnt, p1)


def _final_kernel(ot_ref, hs_ref, g_ref, y_ref):
    hs = hs_ref[...] + ot_ref[...].T
    ms = jnp.mean(hs * hs, axis=-1, keepdims=True)
    y_ref[0] = hs * lax.rsqrt(ms + EPS) * g_ref[...]


def _final(o_t, hs, final_g, first_seq, n_seq, nt):
    tile_of = lambda b, i: (b + first_seq) * nt + i + 1
    return pl.pallas_call(
        _final_kernel,
        grid=(n_seq, nt - 1),
        in_specs=[
            pl.BlockSpec((D_MODEL, TILE), lambda b, i: (0, tile_of(b, i))),
            pl.BlockSpec((TILE, D_MODEL), lambda b, i: (tile_of(b, i), 0)),
            pl.BlockSpec((1, D_MODEL), lambda b, i: (0, 0)),
        ],
        out_specs=pl.BlockSpec((1, TILE, D_MODEL), lambda b, i: (b, i, 0)),
        out_shape=jax.ShapeDtypeStruct((n_seq, (nt - 1) * TILE, D_MODEL), F32),
        compiler_params=_params("parallel", "parallel"),
        name="final",
    )(o_t, hs, final_g)


def kernel(x_prompt, x_sample, meta, norm1_g, w_in, s5_lam_re, s5_lam_im, s5_log_step, s5_b_re, s5_b_im,
           s5_c_re, s5_c_im, s5_d, w_glu, hg_lb, hg_norm_g, w_hg_out, w_out, norm2_g, peer_wq, peer_keys,
           peer_u, peer_v, final_g):
    assert x_prompt.shape[1] == x_sample.shape[1] and x_prompt.shape[1] % TILE == 0
    assert norm1_g.shape[0] == 1, "single-layer trunk"
    n_prompt = x_prompt.shape[0]
    x_prompt, x_sample = x_prompt.astype(F32), x_sample.astype(F32)
    metap = jnp.concatenate([jnp.zeros((FRONT, D_MODEL), F32), meta.astype(F32)], axis=0)
    row = lambda a: a.astype(F32).reshape(1, -1)

    u, q_pre, f_fwd, f_bwd, i_in, gates = _inproj(x_prompt, x_sample, metap, row(norm1_g[0]),
                                                  w_in[0].astype(BF16))

    y_s5 = _s5_mixer(u, *_s5_weights(s5_lam_re[0], s5_lam_im[0], s5_log_step[0], s5_b_re[0], s5_b_im[0],
                                     s5_c_re[0], s5_c_im[0], s5_d[0]))

    lbs = jnp.cumsum(jax.nn.softmax(hg_lb.astype(F32), axis=0), axis=0)
    o_f, o_b = _hgrn2(q_pre, f_fwd, f_bwd, i_in, row(lbs[0]))

    hs, h2 = _merge(x_prompt, x_sample, metap, y_s5, o_f, o_b, gates, w_glu[0].astype(BF16), row(hg_norm_g[0]),
                    w_hg_out[0].astype(BF16), w_out[0].astype(BF16), row(norm2_g[0]))

    bsz, p, _ = hs.shape
    hs = hs.reshape(bsz * p, D_MODEL)
    h2 = h2.reshape(bsz * p // 2, D_MODEL)
    r2, p2, cnt, p1 = _peer_scores(h2, peer_wq[0].astype(BF16), peer_keys[0].astype(BF16))
    o_t = _peer_dense(h2, _pack_rows(peer_u[0]), _pack_rows(peer_v[0].T), r2, p2, cnt, p1)
    fin = functools.partial(_final, o_t, hs, row(final_g), nt=p // TILE)
    return (fin(first_seq=0, n_seq=n_prompt), fin(first_seq=n_prompt, n_seq=bsz - n_prompt))
```

```python
import functools
import math

import jax
import jax.numpy as jnp
from jax import lax
from jax.experimental import pallas as pl
from jax.experimental.pallas import tpu as pltpu

F32 = jnp.float32
BF16 = jnp.bfloat16

D_MODEL = 1024
N_META = 16
EPS = 1e-6
TILE = 256
FRONT = TILE - N_META

S5_WIDTH = 512
S5_GROUP = 16
S5_GROUPS = 32
S5_STATE = 64
S5_CHUNK = 16
S5_LB = 128 // S5_GROUP
S5_NLB = S5_GROUPS // S5_LB
S5_K = S5_CHUNK * 128

HG_WIDTH = 512
HG_HEADS = 4
HG_DIM = 128
HG_CHUNK = 64

PEER_HEADS = 8
PEER_KEYS = 128
PEER_TOPK = 16
PEER_QDIM = 256
PEER_HALF = 128
PEER_ETILE = 1024
PEER_NCAND = 56

VMEM_LIMIT = 56 * 1024 * 1024

_NT = (((1,), (1,)), ((), ()))
_TN = (((0,), (0,)), ((), ()))


def _largest_divisor(n, target, multiple=1):
    best = None
    for d in range(multiple, min(n, target) + 1, multiple):
        if n % d == 0:
            best = d
    assert best is not None, (n, target, multiple)
    return best


def _pack(x):
    return pltpu.bitcast(x.astype(BF16), jnp.uint32)


def _unpack(x):
    return pltpu.bitcast(x, BF16)


def _params(*sem):
    return pltpu.CompilerParams(dimension_semantics=sem, vmem_limit_bytes=VMEM_LIMIT)


def _hidden_tile(xp_ref, xs_ref, metap_ref, n_prompt):
    b, i = pl.program_id(0), pl.program_id(1)
    x = jnp.where(b < n_prompt, xp_ref[0], xs_ref[0])
    return jnp.where(i == 0, metap_ref[...], x)


def _x_specs(n_prompt, nt):
    tile = lambda i: jnp.maximum(i - 1, 0)
    prompt = pl.BlockSpec((1, TILE, D_MODEL), lambda b, i: (
        jnp.minimum(b, n_prompt - 1), jnp.where(b < n_prompt, tile(i), nt - 2), 0))
    sample = pl.BlockSpec((1, TILE, D_MODEL), lambda b, i: (
        jnp.maximum(b - n_prompt, 0), jnp.where(b < n_prompt, 0, tile(i)), 0))
    return [prompt, sample]


def _inproj_kernel(xp_ref, xs_ref, metap_ref, g_ref, w_ref, u_ref, q_ref, ff_ref, fb_ref, v_ref, gate_ref, *,
                   n_prompt):
    i = pl.program_id(1)
    hs = _hidden_tile(xp_ref, xs_ref, metap_ref, n_prompt)
    ms = jnp.mean(hs * hs, axis=-1, keepdims=True)
    h = (hs * lax.rsqrt(ms + EPS) * g_ref[...]).astype(BF16)
    row = lax.broadcasted_iota(jnp.int32, (TILE, 1), 0)
    valid = jnp.logical_or(i > 0, row >= FRONT)
    u = jnp.dot(h, w_ref[:, 0:S5_WIDTH], preferred_element_type=F32)
    u_ref[0] = jnp.where(valid, u, 0.0)
    off = S5_WIDTH
    for ref in (q_ref, ff_ref, fb_ref, v_ref):
        ref[0] = jnp.dot(h, w_ref[:, off:off + HG_WIDTH], preferred_element_type=F32)
        off += HG_WIDTH
    gate_ref[0] = jnp.dot(h, w_ref[:, off:], preferred_element_type=F32)


def _inproj(x_prompt, x_sample, metap, g, w_in):
    n_prompt, length, _ = x_prompt.shape
    bsz = n_prompt + x_sample.shape[0]
    nt = length // TILE + 1
    p = nt * TILE
    ncols = w_in.shape[1]
    ngate = ncols - S5_WIDTH - 4 * HG_WIDTH
    tok = lambda width: pl.BlockSpec((1, TILE, width), lambda b, i: (b, i, 0))
    full = lambda shape: pl.BlockSpec(shape, lambda b, i: (0,) * len(shape))
    hg_shape = jax.ShapeDtypeStruct((bsz, p, HG_WIDTH), F32)
    return pl.pallas_call(
        functools.partial(_inproj_kernel, n_prompt=n_prompt),
        grid=(bsz, nt),
        in_specs=_x_specs(n_prompt, nt) + [full((TILE, D_MODEL)), full((1, D_MODEL)), full((D_MODEL, ncols))],
        out_specs=[tok(S5_WIDTH), tok(HG_WIDTH), tok(HG_WIDTH), tok(HG_WIDTH), tok(HG_WIDTH), tok(ngate)],
        out_shape=[hg_shape, hg_shape, hg_shape, hg_shape, hg_shape, jax.ShapeDtypeStruct((bsz, p, ngate), F32)],
        compiler_params=_params("arbitrary", "arbitrary"),
        name="inproj",
    )(x_prompt, x_sample, metap, g, w_in)


def _s5_weights(lam_re, lam_im, log_step, b_re, b_im, c_re, c_im, d_skip):
    hi = lax.Precision.HIGHEST
    f = lambda a: a.astype(F32)
    lam_re, lam_im, log_step = f(lam_re), f(lam_im), f(log_step)
    b_re, b_im, c_re, c_im, d_skip = f(b_re), f(b_im), f(c_re), f(c_im), f(d_skip)
    step = jnp.exp(log_step)[:, :, None]
    pw = jnp.arange(S5_CHUNK + 1, dtype=F32)[:, None, None, None]
    mag = jnp.exp(pw * lam_re[None] * step[None])
    ang = pw * lam_im[None] * step[None]
    a_re, a_im = mag * jnp.cos(ang), mag * jnp.sin(ang)
    den = lam_re * lam_re + lam_im * lam_im
    n_re, n_im = a_re[1] - 1.0, a_im[1]
    coef_re = (n_re * lam_re + n_im * lam_im) / den
    coef_im = (n_im * lam_re - n_re * lam_im) / den
    w_re = coef_re[None] * a_re - coef_im[None] * a_im
    w_im = coef_re[None] * a_im + coef_im[None] * a_re

    def conv_taps(direction):
        wr, wi = w_re[:S5_CHUNK, direction], w_im[:S5_CHUNK, direction]
        m_re = wr[..., None] * b_re[None] - wi[..., None] * b_im[None]
        m_im = wr[..., None] * b_im[None] + wi[..., None] * b_re[None]
        return (jnp.einsum('gon,dgni->dgoi', c_re, m_re, precision=hi)
                - jnp.einsum('gon,dgni->dgoi', c_im, m_im, precision=hi))

    k_f, k_b = conv_taps(0), conv_taps(1)
    center = k_f[0] + k_b[0] + jnp.eye(S5_GROUP, dtype=F32)[None] * d_skip.reshape(S5_GROUPS, S5_GROUP, 1)
    taps = jnp.concatenate([k_b[:0:-1], center[None], k_f[1:]], axis=0)
    t_idx = jnp.arange(S5_CHUNK)
    toe = taps[t_idx[None, :] - t_idx[:, None] + S5_CHUNK - 1]
    split = lambda a, axis: a.reshape(a.shape[:axis] + (S5_NLB, S5_LB) + a.shape[axis + 1:])

    def block_diag(src):
        src = src.astype(BF16)
        width = src.shape[-1]
        lane_pad = lambda g: [(0, 0, 0)] * 4 + [(g * width, (S5_LB - 1 - g) * width, 0)]
        parts = [lax.pad(src[:, g], jnp.zeros((), BF16), lane_pad(g)) for g in range(S5_LB)]
        return jnp.stack(parts, axis=2).reshape(S5_NLB, S5_K, S5_K)

    toe = block_diag(split(toe.transpose(2, 0, 4, 1, 3), 0))

    def end_map(direction, powers):
        pr, pi = a_re[powers, direction], a_im[powers, direction]
        return [pr[..., None] * b_re[None] - pi[..., None] * b_im[None],
                pr[..., None] * b_im[None] + pi[..., None] * b_re[None]]

    ends = jnp.stack(end_map(0, S5_CHUNK - 1 - t_idx) + end_map(1, t_idx))
    ends = block_diag(split(ends.transpose(2, 1, 4, 0, 3), 0))

    def out_map(direction, powers):
        wr, wi = w_re[powers, direction], w_im[powers, direction]
        return [c_re[None] * wr[:, :, None, :] - c_im[None] * wi[:, :, None, :],
                -c_re[None] * wi[:, :, None, :] - c_im[None] * wr[:, :, None, :]]

    outs = jnp.stack(out_map(0, t_idx + 1) + out_map(1, S5_CHUNK - t_idx))
    outs = block_diag(split(outs.transpose(2, 0, 4, 1, 3), 0))

    def decay(direction):
        ar = split(a_re[S5_CHUNK, direction], 0).reshape(S5_NLB, 1, S5_K // 4)
        ai = split(a_im[S5_CHUNK, direction], 0).reshape(S5_NLB, 1, S5_K // 4)
        same = jnp.concatenate([ar, ar], axis=2).reshape(1, S5_NLB * S5_K // 2)
        cross = jnp.concatenate([-ai, ai], axis=2).reshape(1, S5_NLB * S5_K // 2)
        return same, cross

    return toe.astype(BF16), ends.astype(BF16), outs.astype(BF16), decay(0) + decay(1)


def _chunk_rows(u_ref, nc):
    parts = [u_ref[0, pl.ds(s, nc, stride=S5_CHUNK), :] for s in range(S5_CHUNK)]
    return jnp.concatenate(parts, axis=1).astype(BF16)


def _s5_local_kernel(u_ref, w_ref, ef_ref, eb_ref, *, nc):
    e = jnp.dot(_chunk_rows(u_ref, nc), w_ref[0], preferred_element_type=F32)
    ef_ref[0] = e[:, :S5_K // 2]
    eb_ref[0] = e[:, S5_K // 2:]


def _s5_local(u, ends, nc):
    bsz, p, _ = u.shape
    nch = p // S5_CHUNK
    out = lambda: pl.BlockSpec((1, nc, S5_K // 2), lambda lb, b, i: (b, i, lb))
    shape = jax.ShapeDtypeStruct((bsz, nch, S5_NLB * S5_K // 2), F32)
    return pl.pallas_call(
        functools.partial(_s5_local_kernel, nc=nc),
        grid=(S5_NLB, bsz, nch // nc),
        in_specs=[pl.BlockSpec((1, nc * S5_CHUNK, 128), lambda lb, b, i: (b, i, lb)),
                  pl.BlockSpec((1, S5_K, S5_K), lambda lb, b, i: (lb, 0, 0))],
        out_specs=[out(), out()],
        out_shape=[shape, shape],
        compiler_params=_params("arbitrary", "arbitrary", "arbitrary"),
        name="s5_local",
    )(u, ends)


def _swap_re_im(x):
    half = S5_K // 4
    parts = []
    for lb in range(x.shape[1] // (2 * half)):
        parts += [x[:, (2 * lb + 1) * half:(2 * lb + 2) * half], x[:, 2 * lb * half:(2 * lb + 1) * half]]
    return jnp.concatenate(parts, axis=1)


def _s5_scan_kernel(fs_ref, fc_ref, bs_ref, bc_ref, ef_ref, eb_ref, xf_ref, xb_ref, st_ref, *, ct):
    j = pl.program_id(1)

    @pl.when(j == 0)
    def _():
        st_ref[...] = jnp.zeros_like(st_ref)

    width = S5_K
    for part in range(st_ref.shape[2] // width):
        lanes = slice(part * width, (part + 1) * width)
        f_same, f_cross = fs_ref[:, lanes], fc_ref[:, lanes]
        b_same, b_cross = bs_ref[:, lanes], bc_ref[:, lanes]

        def body(c, carry):
            sf, sb = carry
            xf_ref[0, pl.ds(c, 1), lanes] = sf
            sf = f_same * sf + f_cross * _swap_re_im(sf) + ef_ref[0, pl.ds(c, 1), lanes]
            cb = ct - 1 - c
            xb_ref[0, pl.ds(cb, 1), lanes] = sb
            sb = b_same * sb + b_cross * _swap_re_im(sb) + eb_ref[0, pl.ds(cb, 1), lanes]
            return sf, sb

        sf, sb = lax.fori_loop(0, ct, body, (st_ref[0, :, lanes], st_ref[1, :, lanes]))
        st_ref[0, :, lanes] = sf
        st_ref[1, :, lanes] = sb


def _s5_scan(ef, eb, dec):
    bsz, nch, width = ef.shape
    ct = _largest_divisor(nch, 208, 8)
    nj = nch // ct
    fwd = lambda: pl.BlockSpec((1, ct, width), lambda b, j: (b, j, 0))
    bwd = lambda: pl.BlockSpec((1, ct, width), lambda b, j: (b, nj - 1 - j, 0))
    row = lambda: pl.BlockSpec((1, width), lambda b, j: (0, 0))
    shape = jax.ShapeDtypeStruct(ef.shape, F32)
    return pl.pallas_call(
        functools.partial(_s5_scan_kernel, ct=ct),
        grid=(bsz, nj),
        in_specs=[row(), row(), row(), row(), fwd(), bwd()],
        out_specs=[fwd(), bwd()],
        out_shape=[shape, shape],
        scratch_shapes=[pltpu.VMEM((2, 1, width), F32)],
        compiler_params=_params("parallel", "arbitrary"),
        name="s5_scan",
    )(*dec, ef, eb)


def _s5_out_kernel(u_ref, xf_ref, xb_ref, toe_ref, outs_ref, y_ref, *, nc):
    states = jnp.concatenate([xf_ref[0], xb_ref[0]], axis=1).astype(BF16)
    y = jnp.dot(_chunk_rows(u_ref, nc), toe_ref[0], preferred_element_type=F32)
    y += jnp.dot(states, outs_ref[0], preferred_element_type=F32)
    for t in range(S5_CHUNK):
        y_ref[0, pl.ds(t, nc, stride=S5_CHUNK), :] = y[:, t * 128:(t + 1) * 128]


def _s5_out(u, xf, xb, toe, outs, nc):
    bsz, p, _ = u.shape
    nch = p // S5_CHUNK
    tok = lambda: pl.BlockSpec((1, nc * S5_CHUNK, 128), lambda lb, b, i: (b, i, lb))
    state = lambda: pl.BlockSpec((1, nc, S5_K // 2), lambda lb, b, i: (b, i, lb))
    weight = lambda: pl.BlockSpec((1, S5_K, S5_K), lambda lb, b, i: (lb, 0, 0))
    return pl.pallas_call(
        functools.partial(_s5_out_kernel, nc=nc),
        grid=(S5_NLB, bsz, nch // nc),
        in_specs=[tok(), state(), state(), weight(), weight()],
        out_specs=tok(),
        out_shape=jax.ShapeDtypeStruct(u.shape, F32),
        compiler_params=_params("arbitrary", "arbitrary", "arbitrary"),
        name="s5_out",
    )(u, xf, xb, toe, outs)


def _s5_mixer(u, toe, ends, outs, dec):
    nch = u.shape[1] // S5_CHUNK
    nc = _largest_divisor(nch, 260, 8)
    ef, eb = _s5_local(u, ends, nc)
    xf, xb = _s5_scan(ef, eb, dec)
    return _s5_out(u, xf, xb, toe, outs, nc)


def _hg_kernel(lb_ref, qf_ref, ff_ref, vf_ref, qb_ref, fb_ref, vb_ref, of_ref, ob_ref, st_ref):
    i = pl.program_id(1)
    n = pl.num_programs(1)

    @pl.when(i == 0)
    def _():
        st_ref[...] = jnp.zeros_like(st_ref)

    lb = lb_ref[...]
    r = lax.broadcasted_iota(jnp.int32, (HG_CHUNK, HG_CHUNK), 0)
    c = lax.broadcasted_iota(jnp.int32, (HG_CHUNK, HG_CHUNK), 1)
    row = lax.broadcasted_iota(jnp.int32, (HG_CHUNK, 1), 0)
    passes = (
        (0, qf_ref, ff_ref, vf_ref, of_ref, i, c <= r, range(TILE // HG_CHUNK), HG_CHUNK - 1),
        (1, qb_ref, fb_ref, vb_ref, ob_ref, n - 1 - i, c >= r, reversed(range(TILE // HG_CHUNK)), 0),
    )
    for d, q_ref, f_ref, v_ref, o_ref, tile, keep, chunks, tot_row in passes:
        cum = keep.astype(F32)
        for ch in chunks:
            rows = slice(ch * HG_CHUNK, (ch + 1) * HG_CHUNK)
            q_pre = q_ref[0, rows, :]
            q = q_pre * jax.nn.sigmoid(q_pre)
            g = lb + (1.0 - lb) * jax.nn.sigmoid(f_ref[0, rows, :])
            valid = tile * TILE + ch * HG_CHUNK + row >= FRONT
            log_f = jnp.where(valid, jnp.log(g), 0.0)
            k = jnp.where(valid, 1.0 - g, 0.0)
            b = jnp.dot(cum, log_f, preferred_element_type=F32, precision=lax.Precision.HIGHEST)
            b_tot = b[tot_row:tot_row + 1]
            q_dec = (q * jnp.exp(b)).astype(BF16)
            k_inv = (k * jnp.exp(-b)).astype(BF16)
            k_end = (k * jnp.exp(b_tot - b)).astype(BF16)
            decay = jnp.exp(b_tot)
            v = v_ref[0, rows, :].astype(BF16)
            for h in range(HG_HEADS):
                ls = slice(h * HG_DIM, (h + 1) * HG_DIM)
                scores = lax.dot_general(q_dec[:, ls], k_inv[:, ls], _NT, preferred_element_type=F32)
                scores = jnp.where(keep, scores, 0.0).astype(BF16)
                st = st_ref[d, h]
                o = jnp.dot(scores, v[:, ls], preferred_element_type=F32)
                o += lax.dot_general(q_dec[:, ls], st.astype(BF16), _NT, preferred_element_type=F32)
                kv = lax.dot_general(v[:, ls], k_end[:, ls], _TN, preferred_element_type=F32)
                st_ref[d, h] = decay[:, ls] * st + kv
                o_ref[0, rows, ls] = o


def _hgrn2(q_pre, f_fwd, f_bwd, i_in, lb):
    bsz, p, _ = q_pre.shape
    nt = p // TILE
    fwd = lambda: pl.BlockSpec((1, TILE, HG_WIDTH), lambda b, i: (b, i, 0))
    bwd = lambda: pl.BlockSpec((1, TILE, HG_WIDTH), lambda b, i: (b, nt - 1 - i, 0))
    shape = jax.ShapeDtypeStruct((bsz, p, HG_WIDTH), F32)
    return pl.pallas_call(
        _hg_kernel,
        grid=(bsz, nt),
        in_specs=[pl.BlockSpec((1, HG_WIDTH), lambda b, i: (0, 0)), fwd(), fwd(), fwd(), bwd(), bwd(), bwd()],
        out_specs=[fwd(), bwd()],
        out_shape=[shape, shape],
        scratch_shapes=[pltpu.VMEM((2, HG_HEADS, HG_DIM, HG_DIM), F32)],
        compiler_params=_params("parallel", "arbitrary"),
        name="hgrn2",
    )(lb, q_pre, f_fwd, i_in, q_pre, f_bwd, i_in)


def _gelu(x):
    return 0.5 * x * (1.0 + lax.erf(x * (1.0 / math.sqrt(2.0))))


def _merge_kernel(xp_ref, xs_ref, metap_ref, ys_ref, of_ref, ob_ref, gate_ref, wglu_ref, hgn_ref, whg_ref,
                  wout_ref, n2_ref, hs_ref, h2_ref, *, n_prompt):
    hs = _hidden_tile(xp_ref, xs_ref, metap_ref, n_prompt)
    glu = jnp.dot(_gelu(ys_ref[0]).astype(BF16), wglu_ref[...], preferred_element_type=F32)
    y_a = glu[:, :D_MODEL] * jax.nn.sigmoid(glu[:, D_MODEL:])
    o = of_ref[0] + ob_ref[0]
    normed = []
    for h in range(HG_HEADS):
        oh = o[:, h * HG_DIM:(h + 1) * HG_DIM]
        normed.append(oh * lax.rsqrt(jnp.mean(oh * oh, axis=-1, keepdims=True) + EPS))
    o_gate = gate_ref[0, :, 0:HG_WIDTH]
    y_hg = jnp.concatenate(normed, axis=-1) * hgn_ref[...] * (o_gate * jax.nn.sigmoid(o_gate))
    y_b = jnp.dot(y_hg.astype(BF16), whg_ref[...], preferred_element_type=F32)
    gate_a = gate_ref[0, :, HG_WIDTH:HG_WIDTH + D_MODEL]
    gate_b = gate_ref[0, :, HG_WIDTH + D_MODEL:]
    mixed = jax.nn.sigmoid(gate_a) * y_a + jax.nn.sigmoid(gate_b) * y_b
    hs = hs + jnp.dot(mixed.astype(BF16), wout_ref[...], preferred_element_type=F32)
    hs_ref[0] = hs
    ms = jnp.mean(hs * hs, axis=-1, keepdims=True)
    h2_ref[0] = _pack(hs * lax.rsqrt(ms + EPS) * n2_ref[...])


def _merge(x_prompt, x_sample, metap, y_s5, o_f, o_b, gates, w_glu, hg_norm_g, w_hg_out, w_out, norm2_g):
    bsz, p, _ = y_s5.shape
    nt = p // TILE
    n_prompt = x_prompt.shape[0]
    tok = lambda width: pl.BlockSpec((1, TILE, width), lambda b, i: (b, i, 0))
    full = lambda a: pl.BlockSpec(a.shape, lambda b, i: (0,) * a.ndim)
    return pl.pallas_call(
        functools.partial(_merge_kernel, n_prompt=n_prompt),
        grid=(bsz, nt),
        in_specs=_x_specs(n_prompt, nt) + [
            full(metap), tok(S5_WIDTH), tok(HG_WIDTH), tok(HG_WIDTH), tok(gates.shape[-1]),
            full(w_glu), full(hg_norm_g), full(w_hg_out), full(w_out), full(norm2_g),
        ],
        out_specs=[tok(D_MODEL), pl.BlockSpec((1, TILE // 2, D_MODEL), lambda b, i: (b, i, 0))],
        out_shape=[jax.ShapeDtypeStruct((bsz, p, D_MODEL), F32),
                   jax.ShapeDtypeStruct((bsz, p // 2, D_MODEL), jnp.uint32)],
        compiler_params=_params("arbitrary", "arbitrary"),
        name="merge",
    )(x_prompt, x_sample, metap, y_s5, o_f, o_b, gates, w_glu, hg_norm_g, w_hg_out, w_out, norm2_g)


def _staircase():
    return [(i, PEER_TOPK // (i + 1)) for i in range(PEER_TOPK)]


def _top16(s):
    work = s
    rank = jnp.full(s.shape, float(PEER_TOPK), F32)
    tops = []
    for it in range(PEER_TOPK):
        m = jnp.max(work, axis=0, keepdims=True)
        hit = work == m
        rank = jnp.where(hit, float(it), rank)
        work = jnp.where(hit, -jnp.inf, work)
        tops.append(m)
    return tops, rank


def _score_kernel(h2_ref, wq_ref, keys_ref, r2_ref, p2_ref, cnt_ref, p1_ref, cand_ref):
    h2 = _unpack(h2_ref[...])
    for h in range(PEER_HEADS):
        qh = jnp.dot(h2, wq_ref[:, h * PEER_QDIM:(h + 1) * PEER_QDIM], preferred_element_type=F32).astype(BF16)
        s1 = lax.dot_general(keys_ref[h, 0], qh[:, :PEER_HALF], _NT, preferred_element_type=F32)
        s2 = lax.dot_general(keys_ref[h, 1], qh[:, PEER_HALF:], _NT, preferred_element_type=F32)
        t1, rank1 = _top16(s1)
        t2, rank2 = _top16(s2)
        t2_all = jnp.concatenate(t2, axis=0)
        cand_ref[...] = jnp.full(cand_ref.shape, -jnp.inf, F32)
        off = 0
        for i, n_i in _staircase():
            cand_ref[off:off + n_i, :] = t1[i] + t2_all[0:n_i]
            off += n_i
        cand = cand_ref[...]
        c_max = t1[0] + t2[0]
        work = cand
        z = jnp.zeros_like(c_max)
        tau = c_max
        for it in range(PEER_TOPK):
            tau = jnp.max(work, axis=0, keepdims=True)
            z = z + jnp.exp(tau - c_max)
            work = jnp.where(work == tau, -jnp.inf, work)
        chosen = (cand >= tau).astype(F32)
        cnt = jnp.zeros_like(s1)
        off = 0
        for i, n_i in _staircase():
            cnt_i = jnp.sum(chosen[off:off + n_i], axis=0, keepdims=True)
            cnt = jnp.where(rank1 == float(i), cnt_i, cnt)
            off += n_i
        r2_ref[h] = _pack(rank2)
        p2_ref[h] = _pack(jnp.exp(s2 - t2[0]))
        cnt_ref[h] = cnt
        p1_ref[h] = jnp.exp(s1 - t1[0]) / z


def _peer_scores(h2, w_q, keys):
    ntok = 2 * h2.shape[0]
    blk = lambda rows: pl.BlockSpec((PEER_HEADS, rows, TILE), lambda i: (0, 0, i))
    packed = jax.ShapeDtypeStruct((PEER_HEADS, PEER_KEYS // 2, ntok), jnp.uint32)
    plain = jax.ShapeDtypeStruct((PEER_HEADS, PEER_KEYS, ntok), F32)
    return pl.pallas_call(
        _score_kernel,
        grid=(ntok // TILE,),
        in_specs=[
            pl.BlockSpec((TILE // 2, D_MODEL), lambda i: (i, 0)),
            pl.BlockSpec(w_q.shape, lambda i: (0, 0)),
            pl.BlockSpec(keys.shape, lambda i: (0, 0, 0, 0)),
        ],
        out_specs=[blk(PEER_KEYS // 2), blk(PEER_KEYS // 2), blk(PEER_KEYS), blk(PEER_KEYS)],
        out_shape=[packed, packed, plain, plain],
        scratch_shapes=[pltpu.VMEM((PEER_NCAND, TILE), F32)],
        compiler_params=_params("parallel"),
        name="peer_scores",
    )(h2, w_q, keys)


def _pack_tables_kernel(u_ref, v_ref, up_ref, vtp_ref):
    up_ref[...] = _pack(u_ref[...])
    vtp_ref[...] = _pack(v_ref[...].T)


def _pack_tables(u_tab, v_tab):
    nexp, d = u_tab.shape
    te = _largest_divisor(nexp, 512, 128)
    tile = lambda: pl.BlockSpec((te, d), lambda e: (e, 0))
    return pl.pallas_call(
        _pack_tables_kernel,
        grid=(nexp // te,),
        in_specs=[tile(), tile()],
        out_specs=[pl.BlockSpec((te // 2, d), lambda e: (e, 0)), pl.BlockSpec((d // 2, te), lambda e: (0, e))],
        out_shape=[jax.ShapeDtypeStruct((nexp // 2, d), jnp.uint32), jax.ShapeDtypeStruct((d // 2, nexp), jnp.uint32)],
        compiler_params=_params("parallel"),
        name="pack_tables",
    )(u_tab.astype(F32), v_tab.astype(F32))


def _dense_kernel(h2_ref, u_ref, vt_ref, r2_ref, p2_ref, cnt_ref, p1_ref, o_ref, act_ref, wa_ref, *, tn):
    e = pl.program_id(1)
    act_ref[...] = lax.dot_general(_unpack(u_ref[...]), _unpack(h2_ref[...]), _NT, preferred_element_type=F32)
    n_first = PEER_ETILE // PEER_KEYS
    first = pl.multiple_of(e * n_first, 8)
    zero = jnp.zeros((), BF16)
    for lc in range(tn // 128):
        lanes = slice(lc * 128, (lc + 1) * 128)
        cnts = [cnt_ref[h, pl.ds(first, n_first), lanes] for h in range(PEER_HEADS)]
        p1s = [p1_ref[h, pl.ds(first, n_first), lanes] for h in range(PEER_HEADS)]
        for j in range(n_first):
            w = jnp.zeros((PEER_KEYS, 128), BF16)
            for h in range(PEER_HEADS):
                cnt = jnp.broadcast_to(cnts[h][j:j + 1], (PEER_KEYS, 128)).astype(BF16)
                p1 = jnp.broadcast_to(p1s[h][j:j + 1], (PEER_KEYS, 128)).astype(BF16)
                w += jnp.where(_unpack(r2_ref[h, :, lanes]) < cnt, _unpack(p2_ref[h, :, lanes]), zero) * p1
            rows = slice(j * PEER_KEYS, (j + 1) * PEER_KEYS)
            wa_ref[rows, lanes] = w * _gelu(act_ref[rows, lanes]).astype(BF16)
    contrib = jnp.dot(_unpack(vt_ref[...]), wa_ref[...], preferred_element_type=F32)

    @pl.when(e == 0)
    def _():
        o_ref[...] = contrib

    @pl.when(e > 0)
    def _():
        o_ref[...] += contrib


def _peer_dense(h2, u_tab, vt_tab, r2, p2, cnt, p1):
    ntok = 2 * h2.shape[0]
    tn = _largest_divisor(ntok, 768, 256)
    nexp = 2 * u_tab.shape[0]
    sel = lambda rows: pl.BlockSpec((PEER_HEADS, rows, tn), lambda t, e: (0, 0, t))
    return pl.pallas_call(
        functools.partial(_dense_kernel, tn=tn),
        grid=(ntok // tn, nexp // PEER_ETILE),
        in_specs=[pl.BlockSpec((tn // 2, D_MODEL), lambda t, e: (t, 0)),
                  pl.BlockSpec((PEER_ETILE // 2, D_MODEL), lambda t, e: (e, 0)),
                  pl.BlockSpec((D_MODEL // 2, PEER_ETILE), lambda t, e: (0, e)),
                  sel(PEER_KEYS // 2), sel(PEER_KEYS // 2), sel(PEER_KEYS), sel(PEER_KEYS)],
        out_specs=pl.BlockSpec((D_MODEL, tn), lambda t, e: (0, t)),
        out_shape=jax.ShapeDtypeStruct((D_MODEL, ntok), F32),
        scratch_shapes=[pltpu.VMEM((PEER_ETILE, tn), F32), pltpu.VMEM((PEER_ETILE, tn), BF16)],
        compiler_params=_params("parallel", "arbitrary"),
        name="peer_dense",
    )(h2, u_tab, vt_tab, r2, p2, cnt, p1)


def _final_kernel(ot_ref, hs_ref, g_ref, y_ref):
    hs = hs_ref[...] + ot_ref[...].T
    ms = jnp.mean(hs * hs, axis=-1, keepdims=True)
    y_ref[0] = hs * lax.rsqrt(ms + EPS) * g_ref[...]


def _final(o_t, hs, final_g, first_seq, n_seq, nt):
    tile_of = lambda b, i: (b + first_seq) * nt + i + 1
    return pl.pallas_call(
        _final_kernel,
        grid=(n_seq, nt - 1),
        in_specs=[
            pl.BlockSpec((D_MODEL, TILE), lambda b, i: (0, tile_of(b, i))),
            pl.BlockSpec((TILE, D_MODEL), lambda b, i: (tile_of(b, i), 0)),
            pl.BlockSpec((1, D_MODEL), lambda b, i: (0, 0)),
        ],
        out_specs=pl.BlockSpec((1, TILE, D_MODEL), lambda b, i: (b, i, 0)),
        out_shape=jax.ShapeDtypeStruct((n_seq, (nt - 1) * TILE, D_MODEL), F32),
        compiler_params=_params("parallel", "parallel"),
        name="final",
    )(o_t, hs, final_g)


def kernel(x_prompt, x_sample, meta, norm1_g, w_in, s5_lam_re, s5_lam_im, s5_log_step, s5_b_re, s5_b_im,
           s5_c_re, s5_c_im, s5_d, w_glu, hg_lb, hg_norm_g, w_hg_out, w_out, norm2_g, peer_wq, peer_keys,
           peer_u, peer_v, final_g):
    assert x_prompt.shape[1] == x_sample.shape[1] and x_prompt.shape[1] % TILE == 0
    assert norm1_g.shape[0] == 1, "single-layer trunk"
    n_prompt = x_prompt.shape[0]
    x_prompt, x_sample = x_prompt.astype(F32), x_sample.astype(F32)
    metap = jnp.concatenate([jnp.zeros((FRONT, D_MODEL), F32), meta.astype(F32)], axis=0)
    row = lambda a: a.astype(F32).reshape(1, -1)

    u, q_pre, f_fwd, f_bwd, i_in, gates = _inproj(x_prompt, x_sample, metap, row(norm1_g[0]),
                                                  w_in[0].astype(BF16))

    y_s5 = _s5_mixer(u, *_s5_weights(s5_lam_re[0], s5_lam_im[0], s5_log_step[0], s5_b_re[0], s5_b_im[0],
                                     s5_c_re[0], s5_c_im[0], s5_d[0]))

    lb = jax.nn.softmax(hg_lb.astype(F32), axis=0)[0]
    o_f, o_b = _hgrn2(q_pre, f_fwd, f_bwd, i_in, row(lb))

    hs, h2 = _merge(x_prompt, x_sample, metap, y_s5, o_f, o_b, gates, w_glu[0].astype(BF16), row(hg_norm_g[0]),
                    w_hg_out[0].astype(BF16), w_out[0].astype(BF16), row(norm2_g[0]))

    bsz, p, _ = hs.shape
    hs = hs.reshape(bsz * p, D_MODEL)
    h2 = h2.reshape(bsz * p // 2, D_MODEL)
    r2, p2, cnt, p1 = _peer_scores(h2, peer_wq[0].astype(BF16), peer_keys[0].astype(BF16))
    u_tab, vt_tab = _pack_tables(peer_u[0], peer_v[0])
    o_t = _peer_dense(h2, u_tab, vt_tab, r2, p2, cnt, p1)
    fin = functools.partial(_final, o_t, hs, row(final_g), nt=p // TILE)
    return (fin(first_seq=0, n_seq=n_prompt), fin(first_seq=n_prompt, n_seq=bsz - n_prompt))
```

```python
import functools
import math

import jax
import jax.numpy as jnp
from jax import lax
from jax.experimental import pallas as pl
from jax.experimental.pallas import tpu as pltpu

F32 = jnp.float32
BF16 = jnp.bfloat16

D_MODEL = 1024
N_META = 16
EPS = 1e-6
TILE = 256
FRONT = TILE - N_META

S5_WIDTH = 512
S5_GROUP = 16
S5_GROUPS = 32
S5_STATE = 64
S5_CHUNK = 16
S5_LB = 128 // S5_GROUP
S5_NLB = S5_GROUPS // S5_LB
S5_K = S5_CHUNK * 128

HG_WIDTH = 512
HG_HEADS = 4
HG_DIM = 128
HG_CHUNK = 64

PEER_HEADS = 8
PEER_KEYS = 128
PEER_TOPK = 16
PEER_QDIM = 256
PEER_HALF = 128
PEER_ETILE = 1024
PEER_TTILE = 1280
PEER_NCAND = 56

VMEM_LIMIT = 56 * 1024 * 1024

_NT = (((1,), (1,)), ((), ()))
_TN = (((0,), (0,)), ((), ()))


def _largest_divisor(n, target, multiple=1):
    best = None
    for d in range(multiple, min(n, target) + 1, multiple):
        if n % d == 0:
            best = d
    assert best is not None, (n, target, multiple)
    return best


def _pack(x):
    return pltpu.bitcast(x.astype(BF16), jnp.uint32)


def _unpack(x):
    return pltpu.bitcast(x, BF16)


def _params(*sem):
    return pltpu.CompilerParams(dimension_semantics=sem, vmem_limit_bytes=VMEM_LIMIT)


def _hidden_tile(xp_ref, xs_ref, metap_ref, n_prompt):
    b, i = pl.program_id(0), pl.program_id(1)
    x = jnp.where(b < n_prompt, xp_ref[0], xs_ref[0])
    return jnp.where(i == 0, metap_ref[...], x)


def _x_specs(n_prompt, nt):
    tile = lambda i: jnp.maximum(i - 1, 0)
    prompt = pl.BlockSpec((1, TILE, D_MODEL), lambda b, i: (
        jnp.minimum(b, n_prompt - 1), jnp.where(b < n_prompt, tile(i), nt - 2), 0))
    sample = pl.BlockSpec((1, TILE, D_MODEL), lambda b, i: (
        jnp.maximum(b - n_prompt, 0), jnp.where(b < n_prompt, 0, tile(i)), 0))
    return [prompt, sample]


def _inproj_kernel(xp_ref, xs_ref, metap_ref, g_ref, w_ref, u_ref, q_ref, ff_ref, fb_ref, v_ref, gate_ref, *,
                   n_prompt):
    i = pl.program_id(1)
    hs = _hidden_tile(xp_ref, xs_ref, metap_ref, n_prompt)
    ms = jnp.mean(hs * hs, axis=-1, keepdims=True)
    h = (hs * lax.rsqrt(ms + EPS) * g_ref[...]).astype(BF16)
    row = lax.broadcasted_iota(jnp.int32, (TILE, 1), 0)
    valid = jnp.logical_or(i > 0, row >= FRONT)
    u = jnp.dot(h, w_ref[:, 0:S5_WIDTH], preferred_element_type=F32)
    u_ref[0] = jnp.where(valid, u, 0.0)
    off = S5_WIDTH
    for ref in (q_ref, ff_ref, fb_ref, v_ref):
        ref[0] = jnp.dot(h, w_ref[:, off:off + HG_WIDTH], preferred_element_type=F32)
        off += HG_WIDTH
    gate_ref[0] = jnp.dot(h, w_ref[:, off:], preferred_element_type=F32)


def _inproj(x_prompt, x_sample, metap, g, w_in):
    n_prompt, length, _ = x_prompt.shape
    bsz = n_prompt + x_sample.shape[0]
    nt = length // TILE + 1
    p = nt * TILE
    ncols = w_in.shape[1]
    ngate = ncols - S5_WIDTH - 4 * HG_WIDTH
    tok = lambda width: pl.BlockSpec((1, TILE, width), lambda b, i: (b, i, 0))
    full = lambda shape: pl.BlockSpec(shape, lambda b, i: (0,) * len(shape))
    hg_shape = jax.ShapeDtypeStruct((bsz, p, HG_WIDTH), F32)
    return pl.pallas_call(
        functools.partial(_inproj_kernel, n_prompt=n_prompt),
        grid=(bsz, nt),
        in_specs=_x_specs(n_prompt, nt) + [full((TILE, D_MODEL)), full((1, D_MODEL)), full((D_MODEL, ncols))],
        out_specs=[tok(S5_WIDTH), tok(HG_WIDTH), tok(HG_WIDTH), tok(HG_WIDTH), tok(HG_WIDTH), tok(ngate)],
        out_shape=[hg_shape, hg_shape, hg_shape, hg_shape, hg_shape, jax.ShapeDtypeStruct((bsz, p, ngate), F32)],
        compiler_params=_params("arbitrary", "arbitrary"),
        name="inproj",
    )(x_prompt, x_sample, metap, g, w_in)


def _s5_weights(lam_re, lam_im, log_step, b_re, b_im, c_re, c_im, d_skip):
    hi = lax.Precision.HIGHEST
    f = lambda a: a.astype(F32)
    lam_re, lam_im, log_step = f(lam_re), f(lam_im), f(log_step)
    b_re, b_im, c_re, c_im, d_skip = f(b_re), f(b_im), f(c_re), f(c_im), f(d_skip)
    step = jnp.exp(log_step)[:, :, None]
    pw = jnp.arange(S5_CHUNK + 1, dtype=F32)[:, None, None, None]
    mag = jnp.exp(pw * lam_re[None] * step[None])
    ang = pw * lam_im[None] * step[None]
    a_re, a_im = mag * jnp.cos(ang), mag * jnp.sin(ang)
    den = lam_re * lam_re + lam_im * lam_im
    n_re, n_im = a_re[1] - 1.0, a_im[1]
    coef_re = (n_re * lam_re + n_im * lam_im) / den
    coef_im = (n_im * lam_re - n_re * lam_im) / den
    w_re = coef_re[None] * a_re - coef_im[None] * a_im
    w_im = coef_re[None] * a_im + coef_im[None] * a_re

    def conv_taps(direction):
        wr, wi = w_re[:S5_CHUNK, direction], w_im[:S5_CHUNK, direction]
        m_re = wr[..., None] * b_re[None] - wi[..., None] * b_im[None]
        m_im = wr[..., None] * b_im[None] + wi[..., None] * b_re[None]
        return (jnp.einsum('gon,dgni->dgoi', c_re, m_re, precision=hi)
                - jnp.einsum('gon,dgni->dgoi', c_im, m_im, precision=hi))

    k_f, k_b = conv_taps(0), conv_taps(1)
    center = k_f[0] + k_b[0] + jnp.eye(S5_GROUP, dtype=F32)[None] * d_skip.reshape(S5_GROUPS, S5_GROUP, 1)
    taps = jnp.concatenate([k_b[:0:-1], center[None], k_f[1:]], axis=0)
    t_idx = jnp.arange(S5_CHUNK)
    toe = taps[t_idx[None, :] - t_idx[:, None] + S5_CHUNK - 1]
    split = lambda a, axis: a.reshape(a.shape[:axis] + (S5_NLB, S5_LB) + a.shape[axis + 1:])

    def block_diag(src):
        src = src.astype(BF16)
        width = src.shape[-1]
        lane_pad = lambda g: [(0, 0, 0)] * 4 + [(g * width, (S5_LB - 1 - g) * width, 0)]
        parts = [lax.pad(src[:, g], jnp.zeros((), BF16), lane_pad(g)) for g in range(S5_LB)]
        return jnp.stack(parts, axis=2).reshape(S5_NLB, S5_K, S5_K)

    toe = block_diag(split(toe.transpose(2, 0, 4, 1, 3), 0))

    def end_map(direction, powers):
        pr, pi = a_re[powers, direction], a_im[powers, direction]
        return [pr[..., None] * b_re[None] - pi[..., None] * b_im[None],
                pr[..., None] * b_im[None] + pi[..., None] * b_re[None]]

    ends = jnp.stack(end_map(0, S5_CHUNK - 1 - t_idx) + end_map(1, t_idx))
    ends = block_diag(split(ends.transpose(2, 1, 4, 0, 3), 0))

    def out_map(direction, powers):
        wr, wi = w_re[powers, direction], w_im[powers, direction]
        return [c_re[None] * wr[:, :, None, :] - c_im[None] * wi[:, :, None, :],
                -c_re[None] * wi[:, :, None, :] - c_im[None] * wr[:, :, None, :]]

    outs = jnp.stack(out_map(0, t_idx + 1) + out_map(1, S5_CHUNK - t_idx))
    outs = block_diag(split(outs.transpose(2, 0, 4, 1, 3), 0))

    def decay(direction):
        ar = split(a_re[S5_CHUNK, direction], 0).reshape(S5_NLB, 1, S5_K // 4)
        ai = split(a_im[S5_CHUNK, direction], 0).reshape(S5_NLB, 1, S5_K // 4)
        same = jnp.concatenate([ar, ar], axis=2).reshape(1, S5_NLB * S5_K // 2)
        cross = jnp.concatenate([-ai, ai], axis=2).reshape(1, S5_NLB * S5_K // 2)
        return same, cross

    return toe.astype(BF16), ends.astype(BF16), outs.astype(BF16), decay(0) + decay(1)


def _chunk_rows(u_ref, nc):
    parts = [u_ref[0, pl.ds(s, nc, stride=S5_CHUNK), :] for s in range(S5_CHUNK)]
    return jnp.concatenate(parts, axis=1).astype(BF16)


def _s5_local_kernel(u_ref, w_ref, ef_ref, eb_ref, *, nc):
    e = jnp.dot(_chunk_rows(u_ref, nc), w_ref[0], preferred_element_type=F32)
    ef_ref[0] = e[:, :S5_K // 2]
    eb_ref[0] = e[:, S5_K // 2:]


def _s5_local(u, ends, nc):
    bsz, p, _ = u.shape
    nch = p // S5_CHUNK
    out = lambda: pl.BlockSpec((1, nc, S5_K // 2), lambda lb, b, i: (b, i, lb))
    shape = jax.ShapeDtypeStruct((bsz, nch, S5_NLB * S5_K // 2), F32)
    return pl.pallas_call(
        functools.partial(_s5_local_kernel, nc=nc),
        grid=(S5_NLB, bsz, nch // nc),
        in_specs=[pl.BlockSpec((1, nc * S5_CHUNK, 128), lambda lb, b, i: (b, i, lb)),
                  pl.BlockSpec((1, S5_K, S5_K), lambda lb, b, i: (lb, 0, 0))],
        out_specs=[out(), out()],
        out_shape=[shape, shape],
        compiler_params=_params("arbitrary", "arbitrary", "arbitrary"),
        name="s5_local",
    )(u, ends)


def _swap_re_im(x):
    half = S5_K // 4
    parts = []
    for lb in range(x.shape[1] // (2 * half)):
        parts += [x[:, (2 * lb + 1) * half:(2 * lb + 2) * half], x[:, 2 * lb * half:(2 * lb + 1) * half]]
    return jnp.concatenate(parts, axis=1)


def _s5_scan_kernel(fs_ref, fc_ref, bs_ref, bc_ref, ef_ref, eb_ref, xf_ref, xb_ref, st_ref, *, ct):
    j = pl.program_id(1)

    @pl.when(j == 0)
    def _():
        st_ref[...] = jnp.zeros_like(st_ref)

    width = S5_K
    for part in range(st_ref.shape[2] // width):
        lanes = slice(part * width, (part + 1) * width)
        f_same, f_cross = fs_ref[:, lanes], fc_ref[:, lanes]
        b_same, b_cross = bs_ref[:, lanes], bc_ref[:, lanes]

        def body(c, carry):
            sf, sb = carry
            xf_ref[0, pl.ds(c, 1), lanes] = sf
            sf = f_same * sf + f_cross * _swap_re_im(sf) + ef_ref[0, pl.ds(c, 1), lanes]
            cb = ct - 1 - c
            xb_ref[0, pl.ds(cb, 1), lanes] = sb
            sb = b_same * sb + b_cross * _swap_re_im(sb) + eb_ref[0, pl.ds(cb, 1), lanes]
            return sf, sb

        sf, sb = lax.fori_loop(0, ct, body, (st_ref[0, :, lanes], st_ref[1, :, lanes]))
        st_ref[0, :, lanes] = sf
        st_ref[1, :, lanes] = sb


def _s5_scan(ef, eb, dec):
    bsz, nch, width = ef.shape
    ct = _largest_divisor(nch, 208, 8)
    nj = nch // ct
    fwd = lambda: pl.BlockSpec((1, ct, width), lambda b, j: (b, j, 0))
    bwd = lambda: pl.BlockSpec((1, ct, width), lambda b, j: (b, nj - 1 - j, 0))
    row = lambda: pl.BlockSpec((1, width), lambda b, j: (0, 0))
    shape = jax.ShapeDtypeStruct(ef.shape, F32)
    return pl.pallas_call(
        functools.partial(_s5_scan_kernel, ct=ct),
        grid=(bsz, nj),
        in_specs=[row(), row(), row(), row(), fwd(), bwd()],
        out_specs=[fwd(), bwd()],
        out_shape=[shape, shape],
        scratch_shapes=[pltpu.VMEM((2, 1, width), F32)],
        compiler_params=_params("parallel", "arbitrary"),
        name="s5_scan",
    )(*dec, ef, eb)


def _s5_out_kernel(u_ref, xf_ref, xb_ref, toe_ref, outs_ref, y_ref, *, nc):
    states = jnp.concatenate([xf_ref[0], xb_ref[0]], axis=1).astype(BF16)
    y = jnp.dot(_chunk_rows(u_ref, nc), toe_ref[0], preferred_element_type=F32)
    y += jnp.dot(states, outs_ref[0], preferred_element_type=F32)
    for t in range(S5_CHUNK):
        y_ref[0, pl.ds(t, nc, stride=S5_CHUNK), :] = y[:, t * 128:(t + 1) * 128]


def _s5_out(u, xf, xb, toe, outs, nc):
    bsz, p, _ = u.shape
    nch = p // S5_CHUNK
    tok = lambda: pl.BlockSpec((1, nc * S5_CHUNK, 128), lambda lb, b, i: (b, i, lb))
    state = lambda: pl.BlockSpec((1, nc, S5_K // 2), lambda lb, b, i: (b, i, lb))
    weight = lambda: pl.BlockSpec((1, S5_K, S5_K), lambda lb, b, i: (lb, 0, 0))
    return pl.pallas_call(
        functools.partial(_s5_out_kernel, nc=nc),
        grid=(S5_NLB, bsz, nch // nc),
        in_specs=[tok(), state(), state(), weight(), weight()],
        out_specs=tok(),
        out_shape=jax.ShapeDtypeStruct(u.shape, F32),
        compiler_params=_params("arbitrary", "arbitrary", "arbitrary"),
        name="s5_out",
    )(u, xf, xb, toe, outs)


def _s5_mixer(u, toe, ends, outs, dec):
    nch = u.shape[1] // S5_CHUNK
    nc = _largest_divisor(nch, 260, 8)
    ef, eb = _s5_local(u, ends, nc)
    xf, xb = _s5_scan(ef, eb, dec)
    return _s5_out(u, xf, xb, toe, outs, nc)


def _hg_kernel(lb_ref, qf_ref, ff_ref, vf_ref, qb_ref, fb_ref, vb_ref, of_ref, ob_ref, st_ref):
    i = pl.program_id(1)
    n = pl.num_programs(1)

    @pl.when(i == 0)
    def _():
        st_ref[...] = jnp.zeros_like(st_ref)

    lb = lb_ref[...]
    r = lax.broadcasted_iota(jnp.int32, (HG_CHUNK, HG_CHUNK), 0)
    c = lax.broadcasted_iota(jnp.int32, (HG_CHUNK, HG_CHUNK), 1)
    row = lax.broadcasted_iota(jnp.int32, (HG_CHUNK, 1), 0)
    passes = (
        (0, qf_ref, ff_ref, vf_ref, of_ref, i, c <= r, range(TILE // HG_CHUNK), HG_CHUNK - 1),
        (1, qb_ref, fb_ref, vb_ref, ob_ref, n - 1 - i, c >= r, reversed(range(TILE // HG_CHUNK)), 0),
    )
    for d, q_ref, f_ref, v_ref, o_ref, tile, keep, chunks, tot_row in passes:
        cum = keep.astype(F32)
        for ch in chunks:
            rows = slice(ch * HG_CHUNK, (ch + 1) * HG_CHUNK)
            q_pre = q_ref[0, rows, :]
            q = q_pre * jax.nn.sigmoid(q_pre)
            g = lb + (1.0 - lb) * jax.nn.sigmoid(f_ref[0, rows, :])
            valid = tile * TILE + ch * HG_CHUNK + row >= FRONT
            log_f = jnp.where(valid, jnp.log(g), 0.0)
            k = jnp.where(valid, 1.0 - g, 0.0)
            b = jnp.dot(cum, log_f, preferred_element_type=F32, precision=lax.Precision.HIGHEST)
            b_tot = b[tot_row:tot_row + 1]
            q_dec = (q * jnp.exp(b)).astype(BF16)
            k_inv = (k * jnp.exp(-b)).astype(BF16)
            k_end = (k * jnp.exp(b_tot - b)).astype(BF16)
            decay = jnp.exp(b_tot)
            v = v_ref[0, rows, :].astype(BF16)
            for h in range(HG_HEADS):
                ls = slice(h * HG_DIM, (h + 1) * HG_DIM)
                scores = lax.dot_general(q_dec[:, ls], k_inv[:, ls], _NT, preferred_element_type=F32)
                scores = jnp.where(keep, scores, 0.0).astype(BF16)
                st = st_ref[d, h]
                o = jnp.dot(scores, v[:, ls], preferred_element_type=F32)
                o += lax.dot_general(q_dec[:, ls], st.astype(BF16), _NT, preferred_element_type=F32)
                kv = lax.dot_general(v[:, ls], k_end[:, ls], _TN, preferred_element_type=F32)
                st_ref[d, h] = decay[:, ls] * st + kv
                o_ref[0, rows, ls] = o


def _hgrn2(q_pre, f_fwd, f_bwd, i_in, lb):
    bsz, p, _ = q_pre.shape
    nt = p // TILE
    fwd = lambda: pl.BlockSpec((1, TILE, HG_WIDTH), lambda b, i: (b, i, 0))
    bwd = lambda: pl.BlockSpec((1, TILE, HG_WIDTH), lambda b, i: (b, nt - 1 - i, 0))
    shape = jax.ShapeDtypeStruct((bsz, p, HG_WIDTH), F32)
    return pl.pallas_call(
        _hg_kernel,
        grid=(bsz, nt),
        in_specs=[pl.BlockSpec((1, HG_WIDTH), lambda b, i: (0, 0)), fwd(), fwd(), fwd(), bwd(), bwd(), bwd()],
        out_specs=[fwd(), bwd()],
        out_shape=[shape, shape],
        scratch_shapes=[pltpu.VMEM((2, HG_HEADS, HG_DIM, HG_DIM), F32)],
        compiler_params=_params("parallel", "arbitrary"),
        name="hgrn2",
    )(lb, q_pre, f_fwd, i_in, q_pre, f_bwd, i_in)


def _gelu(x):
    return 0.5 * x * (1.0 + lax.erf(x * (1.0 / math.sqrt(2.0))))


def _merge_kernel(xp_ref, xs_ref, metap_ref, ys_ref, of_ref, ob_ref, gate_ref, wglu_ref, hgn_ref, whg_ref,
                  wout_ref, n2_ref, hs_ref, h2_ref, *, n_prompt):
    hs = _hidden_tile(xp_ref, xs_ref, metap_ref, n_prompt)
    glu = jnp.dot(_gelu(ys_ref[0]).astype(BF16), wglu_ref[...], preferred_element_type=F32)
    y_a = glu[:, :D_MODEL] * jax.nn.sigmoid(glu[:, D_MODEL:])
    o = of_ref[0] + ob_ref[0]
    normed = []
    for h in range(HG_HEADS):
        oh = o[:, h * HG_DIM:(h + 1) * HG_DIM]
        normed.append(oh * lax.rsqrt(jnp.mean(oh * oh, axis=-1, keepdims=True) + EPS))
    o_gate = gate_ref[0, :, 0:HG_WIDTH]
    y_hg = jnp.concatenate(normed, axis=-1) * hgn_ref[...] * (o_gate * jax.nn.sigmoid(o_gate))
    y_b = jnp.dot(y_hg.astype(BF16), whg_ref[...], preferred_element_type=F32)
    gate_a = gate_ref[0, :, HG_WIDTH:HG_WIDTH + D_MODEL]
    gate_b = gate_ref[0, :, HG_WIDTH + D_MODEL:]
    mixed = jax.nn.sigmoid(gate_a) * y_a + jax.nn.sigmoid(gate_b) * y_b
    hs = hs + jnp.dot(mixed.astype(BF16), wout_ref[...], preferred_element_type=F32)
    hs_ref[0] = hs
    ms = jnp.mean(hs * hs, axis=-1, keepdims=True)
    h2_ref[0] = _pack(hs * lax.rsqrt(ms + EPS) * n2_ref[...])


def _merge(x_prompt, x_sample, metap, y_s5, o_f, o_b, gates, w_glu, hg_norm_g, w_hg_out, w_out, norm2_g):
    bsz, p, _ = y_s5.shape
    nt = p // TILE
    n_prompt = x_prompt.shape[0]
    tok = lambda width: pl.BlockSpec((1, TILE, width), lambda b, i: (b, i, 0))
    full = lambda a: pl.BlockSpec(a.shape, lambda b, i: (0,) * a.ndim)
    return pl.pallas_call(
        functools.partial(_merge_kernel, n_prompt=n_prompt),
        grid=(bsz, nt),
        in_specs=_x_specs(n_prompt, nt) + [
            full(metap), tok(S5_WIDTH), tok(HG_WIDTH), tok(HG_WIDTH), tok(gates.shape[-1]),
            full(w_glu), full(hg_norm_g), full(w_hg_out), full(w_out), full(norm2_g),
        ],
        out_specs=[tok(D_MODEL), pl.BlockSpec((1, TILE // 2, D_MODEL), lambda b, i: (b, i, 0))],
        out_shape=[jax.ShapeDtypeStruct((bsz, p, D_MODEL), F32),
                   jax.ShapeDtypeStruct((bsz, p // 2, D_MODEL), jnp.uint32)],
        compiler_params=_params("arbitrary", "arbitrary"),
        name="merge",
    )(x_prompt, x_sample, metap, y_s5, o_f, o_b, gates, w_glu, hg_norm_g, w_hg_out, w_out, norm2_g)


def _staircase():
    return [(i, PEER_TOPK // (i + 1)) for i in range(PEER_TOPK)]


def _top16(s):
    work = s
    rank = jnp.full(s.shape, float(PEER_TOPK), F32)
    tops = []
    for it in range(PEER_TOPK):
        m = jnp.max(work, axis=0, keepdims=True)
        hit = work == m
        rank = jnp.where(hit, float(it), rank)
        work = jnp.where(hit, -jnp.inf, work)
        tops.append(m)
    return tops, rank


def _score_kernel(h2_ref, wq_ref, keys_ref, r2_ref, p2_ref, cnt_ref, p1_ref, cand_ref):
    h2 = _unpack(h2_ref[...])
    for h in range(PEER_HEADS):
        qh = jnp.dot(h2, wq_ref[:, h * PEER_QDIM:(h + 1) * PEER_QDIM], preferred_element_type=F32).astype(BF16)
        s1 = lax.dot_general(keys_ref[h, 0], qh[:, :PEER_HALF], _NT, preferred_element_type=F32)
        s2 = lax.dot_general(keys_ref[h, 1], qh[:, PEER_HALF:], _NT, preferred_element_type=F32)
        t1, rank1 = _top16(s1)
        t2, rank2 = _top16(s2)
        t2_all = jnp.concatenate(t2, axis=0)
        cand_ref[...] = jnp.full(cand_ref.shape, -jnp.inf, F32)
        off = 0
        for i, n_i in _staircase():
            cand_ref[off:off + n_i, :] = t1[i] + t2_all[0:n_i]
            off += n_i
        cand = cand_ref[...]
        c_max = t1[0] + t2[0]
        work = cand
        z = jnp.zeros_like(c_max)
        tau = c_max
        for it in range(PEER_TOPK):
            tau = jnp.max(work, axis=0, keepdims=True)
            z = z + jnp.exp(tau - c_max)
            work = jnp.where(work == tau, -jnp.inf, work)
        chosen = (cand >= tau).astype(F32)
        cnt = jnp.zeros_like(s1)
        off = 0
        for i, n_i in _staircase():
            cnt_i = jnp.sum(chosen[off:off + n_i], axis=0, keepdims=True)
            cnt = jnp.where(rank1 == float(i), cnt_i, cnt)
            off += n_i
        r2_ref[h] = _pack(rank2)
        p2_ref[h] = _pack(jnp.exp(s2 - t2[0]))
        cnt_ref[h] = cnt
        p1_ref[h] = jnp.exp(s1 - t1[0]) / z


def _peer_scores(h2, w_q, keys):
    ntok = 2 * h2.shape[0]
    blk = lambda rows: pl.BlockSpec((PEER_HEADS, rows, TILE), lambda i: (0, 0, i))
    packed = jax.ShapeDtypeStruct((PEER_HEADS, PEER_KEYS // 2, ntok), jnp.uint32)
    plain = jax.ShapeDtypeStruct((PEER_HEADS, PEER_KEYS, ntok), F32)
    return pl.pallas_call(
        _score_kernel,
        grid=(ntok // TILE,),
        in_specs=[
            pl.BlockSpec((TILE // 2, D_MODEL), lambda i: (i, 0)),
            pl.BlockSpec(w_q.shape, lambda i: (0, 0)),
            pl.BlockSpec(keys.shape, lambda i: (0, 0, 0, 0)),
        ],
        out_specs=[blk(PEER_KEYS // 2), blk(PEER_KEYS // 2), blk(PEER_KEYS), blk(PEER_KEYS)],
        out_shape=[packed, packed, plain, plain],
        scratch_shapes=[pltpu.VMEM((PEER_NCAND, TILE), F32)],
        compiler_params=_params("parallel"),
        name="peer_scores",
    )(h2, w_q, keys)


def _pack_tables_kernel(u_ref, v_ref, up_ref, vtp_ref):
    up_ref[...] = _pack(u_ref[...])
    vtp_ref[...] = _pack(v_ref[...].T)


def _pack_tables(u_tab, v_tab):
    nexp, d = u_tab.shape
    te = _largest_divisor(nexp, 512, 128)
    tile = lambda: pl.BlockSpec((te, d), lambda e: (e, 0))
    return pl.pallas_call(
        _pack_tables_kernel,
        grid=(nexp // te,),
        in_specs=[tile(), tile()],
        out_specs=[pl.BlockSpec((te // 2, d), lambda e: (e, 0)), pl.BlockSpec((d // 2, te), lambda e: (0, e))],
        out_shape=[jax.ShapeDtypeStruct((nexp // 2, d), jnp.uint32), jax.ShapeDtypeStruct((d // 2, nexp), jnp.uint32)],
        compiler_params=_params("parallel"),
        name="pack_tables",
    )(u_tab.astype(F32), v_tab.astype(F32))


def _dense_kernel(h2_ref, u_ref, vt_ref, r2_ref, p2_ref, cnt_ref, p1_ref, o_ref, act_ref, wa_ref, *, tn):
    e = pl.program_id(1)
    act_ref[...] = lax.dot_general(_unpack(u_ref[...]), _unpack(h2_ref[...]), _NT, preferred_element_type=F32)
    n_first = PEER_ETILE // PEER_KEYS
    zero = jnp.zeros((), BF16)
    spread = lambda row: jnp.broadcast_to(row, (PEER_KEYS, 128)).astype(BF16)
    for lc in range(tn // 128):
        lanes = slice(lc * 128, (lc + 1) * 128)
        cnts = [cnt_ref[h, :, lanes] for h in range(PEER_HEADS)]
        p1s = [p1_ref[h, :, lanes] for h in range(PEER_HEADS)]
        for j in range(0, n_first, 2):
            w0 = jnp.zeros((PEER_KEYS, 128), BF16)
            w1 = jnp.zeros((PEER_KEYS, 128), BF16)
            for h in range(PEER_HEADS):
                r2 = _unpack(r2_ref[h, :, lanes])
                p2 = _unpack(p2_ref[h, :, lanes])
                w0 += jnp.where(r2 < spread(cnts[h][j:j + 1]), p2, zero) * spread(p1s[h][j:j + 1])
                w1 += jnp.where(r2 < spread(cnts[h][j + 1:j + 2]), p2, zero) * spread(p1s[h][j + 1:j + 2])
            for k, w in ((j, w0), (j + 1, w1)):
                rows = slice(k * PEER_KEYS, (k + 1) * PEER_KEYS)
                wa_ref[rows, lanes] = w * _gelu(act_ref[rows, lanes]).astype(BF16)
    contrib = jnp.dot(_unpack(vt_ref[...]), wa_ref[...], preferred_element_type=F32)

    @pl.when(e == 0)
    def _():
        o_ref[...] = contrib

    @pl.when(e > 0)
    def _():
        o_ref[...] += contrib


def _peer_dense(h2, u_tab, vt_tab, r2, p2, cnt, p1):
    ntok = 2 * h2.shape[0]
    tn = _largest_divisor(ntok, PEER_TTILE, 256)
    nexp = 2 * u_tab.shape[0]
    second = lambda: pl.BlockSpec((PEER_HEADS, PEER_KEYS // 2, tn), lambda t, e: (0, 0, t))
    first = lambda: pl.BlockSpec((PEER_HEADS, PEER_ETILE // PEER_KEYS, tn), lambda t, e: (0, e, t))
    return pl.pallas_call(
        functools.partial(_dense_kernel, tn=tn),
        grid=(ntok // tn, nexp // PEER_ETILE),
        in_specs=[pl.BlockSpec((tn // 2, D_MODEL), lambda t, e: (t, 0)),
                  pl.BlockSpec((PEER_ETILE // 2, D_MODEL), lambda t, e: (e, 0)),
                  pl.BlockSpec((D_MODEL // 2, PEER_ETILE), lambda t, e: (0, e)),
                  second(), second(), first(), first()],
        out_specs=pl.BlockSpec((D_MODEL, tn), lambda t, e: (0, t)),
        out_shape=jax.ShapeDtypeStruct((D_MODEL, ntok), F32),
        scratch_shapes=[pltpu.VMEM((PEER_ETILE, tn), F32), pltpu.VMEM((PEER_ETILE, tn), BF16)],
        compiler_params=_params("parallel", "arbitrary"),
        name="peer_dense",
    )(h2, u_tab, vt_tab, r2, p2, cnt, p1)


def _final_kernel(ot_ref, hs_ref, g_ref, y_ref):
    hs = hs_ref[...] + ot_ref[...].T
    ms = jnp.mean(hs * hs, axis=-1, keepdims=True)
    y_ref[0] = hs * lax.rsqrt(ms + EPS) * g_ref[...]


def _final(o_t, hs, final_g, first_seq, n_seq, nt):
    tile_of = lambda b, i: (b + first_seq) * nt + i + 1
    return pl.pallas_call(
        _final_kernel,
        grid=(n_seq, nt - 1),
        in_specs=[
            pl.BlockSpec((D_MODEL, TILE), lambda b, i: (0, tile_of(b, i))),
            pl.BlockSpec((TILE, D_MODEL), lambda b, i: (tile_of(b, i), 0)),
            pl.BlockSpec((1, D_MODEL), lambda b, i: (0, 0)),
        ],
        out_specs=pl.BlockSpec((1, TILE, D_MODEL), lambda b, i: (b, i, 0)),
        out_shape=jax.ShapeDtypeStruct((n_seq, (nt - 1) * TILE, D_MODEL), F32),
        compiler_params=_params("parallel", "parallel"),
        name="final",
    )(o_t, hs, final_g)


def kernel(x_prompt, x_sample, meta, norm1_g, w_in, s5_lam_re, s5_lam_im, s5_log_step, s5_b_re, s5_b_im,
           s5_c_re, s5_c_im, s5_d, w_glu, hg_lb, hg_norm_g, w_hg_out, w_out, norm2_g, peer_wq, peer_keys,
           peer_u, peer_v, final_g):
    assert x_prompt.shape[1] == x_sample.shape[1] and x_prompt.shape[1] % TILE == 0
    assert norm1_g.shape[0] == 1, "single-layer trunk"
    n_prompt = x_prompt.shape[0]
    x_prompt, x_sample = x_prompt.astype(F32), x_sample.astype(F32)
    metap = jnp.concatenate([jnp.zeros((FRONT, D_MODEL), F32), meta.astype(F32)], axis=0)
    row = lambda a: a.astype(F32).reshape(1, -1)

    u, q_pre, f_fwd, f_bwd, i_in, gates = _inproj(x_prompt, x_sample, metap, row(norm1_g[0]),
                                                  w_in[0].astype(BF16))

    y_s5 = _s5_mixer(u, *_s5_weights(s5_lam_re[0], s5_lam_im[0], s5_log_step[0], s5_b_re[0], s5_b_im[0],
                                     s5_c_re[0], s5_c_im[0], s5_d[0]))

    lb = jax.nn.softmax(hg_lb.astype(F32), axis=0)[0]
    o_f, o_b = _hgrn2(q_pre, f_fwd, f_bwd, i_in, row(lb))

    hs, h2 = _merge(x_prompt, x_sample, metap, y_s5, o_f, o_b, gates, w_glu[0].astype(BF16), row(hg_norm_g[0]),
                    w_hg_out[0].astype(BF16), w_out[0].astype(BF16), row(norm2_g[0]))

    bsz, p, _ = hs.shape
    hs = hs.reshape(bsz * p, D_MODEL)
    h2 = h2.reshape(bsz * p // 2, D_MODEL)
    r2, p2, cnt, p1 = _peer_scores(h2, peer_wq[0].astype(BF16), peer_keys[0].astype(BF16))
    u_tab, vt_tab = _pack_tables(peer_u[0], peer_v[0])
    o_t = _peer_dense(h2, u_tab, vt_tab, r2, p2, cnt, p1)
    fin = functools.partial(_final, o_t, hs, row(final_g), nt=p // TILE)
    return (fin(first_seq=0, n_seq=n_prompt), fin(first_seq=n_prompt, n_seq=bsz - n_prompt))
```

```python
import functools
import math

import jax
import jax.numpy as jnp
from jax import lax
from jax.experimental import pallas as pl
from jax.experimental.pallas import tpu as pltpu

F32 = jnp.float32
BF16 = jnp.bfloat16

D_MODEL = 1024
N_META = 16
EPS = 1e-6
TILE = 256
FRONT = TILE - N_META

S5_WIDTH = 512
S5_GROUP = 16
S5_GROUPS = 32
S5_STATE = 64
S5_CHUNK = 16
S5_LB = 128 // S5_GROUP
S5_NLB = S5_GROUPS // S5_LB
S5_K = S5_CHUNK * 128

HG_WIDTH = 512
HG_HEADS = 4
HG_DIM = 128
HG_CHUNK = 64

PEER_HEADS = 8
PEER_KEYS = 128
PEER_TOPK = 16
PEER_QDIM = 256
PEER_HALF = 128
PEER_ETILE = 1024
PEER_TTILE = 1280
PEER_NCAND = 56

VMEM_LIMIT = 56 * 1024 * 1024

_NT = (((1,), (1,)), ((), ()))
_TN = (((0,), (0,)), ((), ()))


def _largest_divisor(n, target, multiple=1):
    best = None
    for d in range(multiple, min(n, target) + 1, multiple):
        if n % d == 0:
            best = d
    assert best is not None, (n, target, multiple)
    return best


def _pack(x):
    return pltpu.bitcast(x.astype(BF16), jnp.uint32)


def _unpack(x):
    return pltpu.bitcast(x, BF16)


def _params(*sem):
    return pltpu.CompilerParams(dimension_semantics=sem, vmem_limit_bytes=VMEM_LIMIT)


def _hidden_tile(xp_ref, xs_ref, metap_ref, n_prompt):
    b, i = pl.program_id(0), pl.program_id(1)
    x = jnp.where(b < n_prompt, xp_ref[0], xs_ref[0])
    return jnp.where(i == 0, metap_ref[...], x)


def _x_specs(n_prompt, nt):
    tile = lambda i: jnp.maximum(i - 1, 0)
    prompt = pl.BlockSpec((1, TILE, D_MODEL), lambda b, i: (
        jnp.minimum(b, n_prompt - 1), jnp.where(b < n_prompt, tile(i), nt - 2), 0))
    sample = pl.BlockSpec((1, TILE, D_MODEL), lambda b, i: (
        jnp.maximum(b - n_prompt, 0), jnp.where(b < n_prompt, 0, tile(i)), 0))
    return [prompt, sample]


def _inproj_kernel(xp_ref, xs_ref, metap_ref, g_ref, w_ref, u_ref, q_ref, ff_ref, fb_ref, v_ref, gate_ref, *,
                   n_prompt):
    i = pl.program_id(1)
    hs = _hidden_tile(xp_ref, xs_ref, metap_ref, n_prompt)
    ms = jnp.mean(hs * hs, axis=-1, keepdims=True)
    h = (hs * lax.rsqrt(ms + EPS) * g_ref[...]).astype(BF16)
    row = lax.broadcasted_iota(jnp.int32, (TILE, 1), 0)
    valid = jnp.logical_or(i > 0, row >= FRONT)
    u = jnp.dot(h, w_ref[:, 0:S5_WIDTH], preferred_element_type=F32)
    u_ref[0] = jnp.where(valid, u, 0.0)
    off = S5_WIDTH
    for ref in (q_ref, ff_ref, fb_ref, v_ref):
        ref[0] = jnp.dot(h, w_ref[:, off:off + HG_WIDTH], preferred_element_type=F32)
        off += HG_WIDTH
    gate_ref[0] = jnp.dot(h, w_ref[:, off:], preferred_element_type=F32)


def _inproj(x_prompt, x_sample, metap, g, w_in):
    n_prompt, length, _ = x_prompt.shape
    bsz = n_prompt + x_sample.shape[0]
    nt = length // TILE + 1
    p = nt * TILE
    ncols = w_in.shape[1]
    ngate = ncols - S5_WIDTH - 4 * HG_WIDTH
    tok = lambda width: pl.BlockSpec((1, TILE, width), lambda b, i: (b, i, 0))
    full = lambda shape: pl.BlockSpec(shape, lambda b, i: (0,) * len(shape))
    hg_shape = jax.ShapeDtypeStruct((bsz, p, HG_WIDTH), F32)
    return pl.pallas_call(
        functools.partial(_inproj_kernel, n_prompt=n_prompt),
        grid=(bsz, nt),
        in_specs=_x_specs(n_prompt, nt) + [full((TILE, D_MODEL)), full((1, D_MODEL)), full((D_MODEL, ncols))],
        out_specs=[tok(S5_WIDTH), tok(HG_WIDTH), tok(HG_WIDTH), tok(HG_WIDTH), tok(HG_WIDTH), tok(ngate)],
        out_shape=[hg_shape, hg_shape, hg_shape, hg_shape, hg_shape, jax.ShapeDtypeStruct((bsz, p, ngate), F32)],
        compiler_params=_params("arbitrary", "arbitrary"),
        name="inproj",
    )(x_prompt, x_sample, metap, g, w_in)


def _s5_weights(lam_re, lam_im, log_step, b_re, b_im, c_re, c_im, d_skip):
    hi = lax.Precision.HIGHEST
    f = lambda a: a.astype(F32)
    lam_re, lam_im, log_step = f(lam_re), f(lam_im), f(log_step)
    b_re, b_im, c_re, c_im, d_skip = f(b_re), f(b_im), f(c_re), f(c_im), f(d_skip)
    step = jnp.exp(log_step)[:, :, None]
    pw = jnp.arange(S5_CHUNK + 1, dtype=F32)[:, None, None, None]
    mag = jnp.exp(pw * lam_re[None] * step[None])
    ang = pw * lam_im[None] * step[None]
    a_re, a_im = mag * jnp.cos(ang), mag * jnp.sin(ang)
    den = lam_re * lam_re + lam_im * lam_im
    n_re, n_im = a_re[1] - 1.0, a_im[1]
    coef_re = (n_re * lam_re + n_im * lam_im) / den
    coef_im = (n_im * lam_re - n_re * lam_im) / den
    w_re = coef_re[None] * a_re - coef_im[None] * a_im
    w_im = coef_re[None] * a_im + coef_im[None] * a_re

    def conv_taps(direction):
        wr, wi = w_re[:S5_CHUNK, direction], w_im[:S5_CHUNK, direction]
        m_re = wr[..., None] * b_re[None] - wi[..., None] * b_im[None]
        m_im = wr[..., None] * b_im[None] + wi[..., None] * b_re[None]
        return (jnp.einsum('gon,dgni->dgoi', c_re, m_re, precision=hi)
                - jnp.einsum('gon,dgni->dgoi', c_im, m_im, precision=hi))

    k_f, k_b = conv_taps(0), conv_taps(1)
    center = k_f[0] + k_b[0] + jnp.eye(S5_GROUP, dtype=F32)[None] * d_skip.reshape(S5_GROUPS, S5_GROUP, 1)
    taps = jnp.concatenate([k_b[:0:-1], center[None], k_f[1:]], axis=0)
    t_idx = jnp.arange(S5_CHUNK)
    toe = taps[t_idx[None, :] - t_idx[:, None] + S5_CHUNK - 1]
    split = lambda a, axis: a.reshape(a.shape[:axis] + (S5_NLB, S5_LB) + a.shape[axis + 1:])

    def block_diag(src):
        src = src.astype(BF16)
        width = src.shape[-1]
        lane_pad = lambda g: [(0, 0, 0)] * 4 + [(g * width, (S5_LB - 1 - g) * width, 0)]
        parts = [lax.pad(src[:, g], jnp.zeros((), BF16), lane_pad(g)) for g in range(S5_LB)]
        return jnp.stack(parts, axis=2).reshape(S5_NLB, S5_K, S5_K)

    toe = block_diag(split(toe.transpose(2, 0, 4, 1, 3), 0))

    def end_map(direction, powers):
        pr, pi = a_re[powers, direction], a_im[powers, direction]
        return [pr[..., None] * b_re[None] - pi[..., None] * b_im[None],
                pr[..., None] * b_im[None] + pi[..., None] * b_re[None]]

    ends = jnp.stack(end_map(0, S5_CHUNK - 1 - t_idx) + end_map(1, t_idx))
    ends = block_diag(split(ends.transpose(2, 1, 4, 0, 3), 0))

    def out_map(direction, powers):
        wr, wi = w_re[powers, direction], w_im[powers, direction]
        return [c_re[None] * wr[:, :, None, :] - c_im[None] * wi[:, :, None, :],
                -c_re[None] * wi[:, :, None, :] - c_im[None] * wr[:, :, None, :]]

    outs = jnp.stack(out_map(0, t_idx + 1) + out_map(1, S5_CHUNK - t_idx))
    outs = block_diag(split(outs.transpose(2, 0, 4, 1, 3), 0))

    def decay(direction):
        ar = split(a_re[S5_CHUNK, direction], 0).reshape(S5_NLB, 1, S5_K // 4)
        ai = split(a_im[S5_CHUNK, direction], 0).reshape(S5_NLB, 1, S5_K // 4)
        same = jnp.concatenate([ar, ar], axis=2).reshape(1, S5_NLB * S5_K // 2)
        cross = jnp.concatenate([-ai, ai], axis=2).reshape(1, S5_NLB * S5_K // 2)
        return same, cross

    return toe.astype(BF16), ends.astype(BF16), outs.astype(BF16), decay(0) + decay(1)


def _chunk_rows(u_ref, nc):
    parts = [u_ref[0, pl.ds(s, nc, stride=S5_CHUNK), :] for s in range(S5_CHUNK)]
    return jnp.concatenate(parts, axis=1).astype(BF16)


def _s5_local_kernel(u_ref, w_ref, ef_ref, eb_ref, *, nc):
    e = jnp.dot(_chunk_rows(u_ref, nc), w_ref[0], preferred_element_type=F32)
    ef_ref[0] = e[:, :S5_K // 2]
    eb_ref[0] = e[:, S5_K // 2:]


def _s5_local(u, ends, nc):
    bsz, p, _ = u.shape
    nch = p // S5_CHUNK
    out = lambda: pl.BlockSpec((1, nc, S5_K // 2), lambda lb, b, i: (b, i, lb))
    shape = jax.ShapeDtypeStruct((bsz, nch, S5_NLB * S5_K // 2), F32)
    return pl.pallas_call(
        functools.partial(_s5_local_kernel, nc=nc),
        grid=(S5_NLB, bsz, nch // nc),
        in_specs=[pl.BlockSpec((1, nc * S5_CHUNK, 128), lambda lb, b, i: (b, i, lb)),
                  pl.BlockSpec((1, S5_K, S5_K), lambda lb, b, i: (lb, 0, 0))],
        out_specs=[out(), out()],
        out_shape=[shape, shape],
        compiler_params=_params("arbitrary", "arbitrary", "arbitrary"),
        name="s5_local",
    )(u, ends)


def _swap_re_im(x):
    half = S5_K // 4
    parts = []
    for lb in range(x.shape[1] // (2 * half)):
        parts += [x[:, (2 * lb + 1) * half:(2 * lb + 2) * half], x[:, 2 * lb * half:(2 * lb + 1) * half]]
    return jnp.concatenate(parts, axis=1)


def _s5_scan_kernel(fs_ref, fc_ref, bs_ref, bc_ref, ef_ref, eb_ref, xf_ref, xb_ref, st_ref, *, ct):
    j = pl.program_id(1)

    @pl.when(j == 0)
    def _():
        st_ref[...] = jnp.zeros_like(st_ref)

    width = S5_K
    for part in range(st_ref.shape[2] // width):
        lanes = slice(part * width, (part + 1) * width)
        f_same, f_cross = fs_ref[:, lanes], fc_ref[:, lanes]
        b_same, b_cross = bs_ref[:, lanes], bc_ref[:, lanes]

        def body(c, carry):
            sf, sb = carry
            xf_ref[0, pl.ds(c, 1), lanes] = sf
            sf = f_same * sf + f_cross * _swap_re_im(sf) + ef_ref[0, pl.ds(c, 1), lanes]
            cb = ct - 1 - c
            xb_ref[0, pl.ds(cb, 1), lanes] = sb
            sb = b_same * sb + b_cross * _swap_re_im(sb) + eb_ref[0, pl.ds(cb, 1), lanes]
            return sf, sb

        sf, sb = lax.fori_loop(0, ct, body, (st_ref[0, :, lanes], st_ref[1, :, lanes]))
        st_ref[0, :, lanes] = sf
        st_ref[1, :, lanes] = sb


def _s5_scan(ef, eb, dec):
    bsz, nch, width = ef.shape
    ct = _largest_divisor(nch, 208, 8)
    nj = nch // ct
    fwd = lambda: pl.BlockSpec((1, ct, width), lambda b, j: (b, j, 0))
    bwd = lambda: pl.BlockSpec((1, ct, width), lambda b, j: (b, nj - 1 - j, 0))
    row = lambda: pl.BlockSpec((1, width), lambda b, j: (0, 0))
    shape = jax.ShapeDtypeStruct(ef.shape, F32)
    return pl.pallas_call(
        functools.partial(_s5_scan_kernel, ct=ct),
        grid=(bsz, nj),
        in_specs=[row(), row(), row(), row(), fwd(), bwd()],
        out_specs=[fwd(), bwd()],
        out_shape=[shape, shape],
        scratch_shapes=[pltpu.VMEM((2, 1, width), F32)],
        compiler_params=_params("parallel", "arbitrary"),
        name="s5_scan",
    )(*dec, ef, eb)


def _s5_out_kernel(u_ref, xf_ref, xb_ref, toe_ref, outs_ref, y_ref, *, nc):
    states = jnp.concatenate([xf_ref[0], xb_ref[0]], axis=1).astype(BF16)
    y = jnp.dot(_chunk_rows(u_ref, nc), toe_ref[0], preferred_element_type=F32)
    y += jnp.dot(states, outs_ref[0], preferred_element_type=F32)
    for t in range(S5_CHUNK):
        y_ref[0, pl.ds(t, nc, stride=S5_CHUNK), :] = y[:, t * 128:(t + 1) * 128]


def _s5_out(u, xf, xb, toe, outs, nc):
    bsz, p, _ = u.shape
    nch = p // S5_CHUNK
    tok = lambda: pl.BlockSpec((1, nc * S5_CHUNK, 128), lambda lb, b, i: (b, i, lb))
    state = lambda: pl.BlockSpec((1, nc, S5_K // 2), lambda lb, b, i: (b, i, lb))
    weight = lambda: pl.BlockSpec((1, S5_K, S5_K), lambda lb, b, i: (lb, 0, 0))
    return pl.pallas_call(
        functools.partial(_s5_out_kernel, nc=nc),
        grid=(S5_NLB, bsz, nch // nc),
        in_specs=[tok(), state(), state(), weight(), weight()],
        out_specs=tok(),
        out_shape=jax.ShapeDtypeStruct(u.shape, F32),
        compiler_params=_params("arbitrary", "arbitrary", "arbitrary"),
        name="s5_out",
    )(u, xf, xb, toe, outs)


def _s5_mixer(u, toe, ends, outs, dec):
    nch = u.shape[1] // S5_CHUNK
    nc = _largest_divisor(nch, 260, 8)
    ef, eb = _s5_local(u, ends, nc)
    xf, xb = _s5_scan(ef, eb, dec)
    return _s5_out(u, xf, xb, toe, outs, nc)


def _hg_kernel(lb_ref, qf_ref, ff_ref, vf_ref, qb_ref, fb_ref, vb_ref, of_ref, ob_ref, st_ref):
    i = pl.program_id(1)
    n = pl.num_programs(1)

    @pl.when(i == 0)
    def _():
        st_ref[...] = jnp.zeros_like(st_ref)

    lb = lb_ref[...]
    n_chunks = TILE // HG_CHUNK
    shift = HG_CHUNK.bit_length() - 1
    r = lax.broadcasted_iota(jnp.int32, (TILE, TILE), 0)
    c = lax.broadcasted_iota(jnp.int32, (TILE, TILE), 1)
    same_chunk = jnp.right_shift(r, shift) == jnp.right_shift(c, shift)
    r64 = lax.broadcasted_iota(jnp.int32, (HG_CHUNK, HG_CHUNK), 0)
    c64 = lax.broadcasted_iota(jnp.int32, (HG_CHUNK, HG_CHUNK), 1)
    row = lax.broadcasted_iota(jnp.int32, (TILE, 1), 0)
    passes = (
        (0, qf_ref, ff_ref, vf_ref, of_ref, i, c <= r, c64 <= r64, range(n_chunks), HG_CHUNK - 1),
        (1, qb_ref, fb_ref, vb_ref, ob_ref, n - 1 - i, c >= r, c64 >= r64, reversed(range(n_chunks)), 0),
    )
    for d, q_ref, f_ref, v_ref, o_ref, tile, before, keep, chunks, tot_row in passes:
        q_pre = q_ref[0]
        q = q_pre * jax.nn.sigmoid(q_pre)
        g = lb + (1.0 - lb) * jax.nn.sigmoid(f_ref[0])
        valid = tile * TILE + row >= FRONT
        log_f = jnp.where(valid, jnp.log(g), 0.0)
        k = jnp.where(valid, 1.0 - g, 0.0)
        cum = jnp.logical_and(same_chunk, before).astype(F32)
        b = jnp.dot(cum, log_f, preferred_element_type=F32, precision=lax.Precision.HIGHEST)
        totals = [b[ch * HG_CHUNK + tot_row:ch * HG_CHUNK + tot_row + 1] for ch in range(n_chunks)]
        b_tot = jnp.concatenate([jnp.broadcast_to(t, (HG_CHUNK, HG_WIDTH)) for t in totals], axis=0)
        q_dec = (q * jnp.exp(b)).astype(BF16)
        k_inv = (k * jnp.exp(-b)).astype(BF16)
        k_end = (k * jnp.exp(b_tot - b)).astype(BF16)
        v = v_ref[0].astype(BF16)
        states = [st_ref[d, h] for h in range(HG_HEADS)]
        for ch in chunks:
            rows = slice(ch * HG_CHUNK, (ch + 1) * HG_CHUNK)
            decay = jnp.exp(totals[ch])
            for h in range(HG_HEADS):
                ls = slice(h * HG_DIM, (h + 1) * HG_DIM)
                scores = lax.dot_general(q_dec[rows, ls], k_inv[rows, ls], _NT, preferred_element_type=F32)
                scores = jnp.where(keep, scores, 0.0).astype(BF16)
                o = jnp.dot(scores, v[rows, ls], preferred_element_type=F32)
                o += lax.dot_general(q_dec[rows, ls], states[h].astype(BF16), _NT, preferred_element_type=F32)
                kv = lax.dot_general(v[rows, ls], k_end[rows, ls], _TN, preferred_element_type=F32)
                states[h] = decay[:, ls] * states[h] + kv
                o_ref[0, rows, ls] = o
        for h in range(HG_HEADS):
            st_ref[d, h] = states[h]


def _hgrn2(q_pre, f_fwd, f_bwd, i_in, lb):
    bsz, p, _ = q_pre.shape
    nt = p // TILE
    fwd = lambda: pl.BlockSpec((1, TILE, HG_WIDTH), lambda b, i: (b, i, 0))
    bwd = lambda: pl.BlockSpec((1, TILE, HG_WIDTH), lambda b, i: (b, nt - 1 - i, 0))
    shape = jax.ShapeDtypeStruct((bsz, p, HG_WIDTH), F32)
    return pl.pallas_call(
        _hg_kernel,
        grid=(bsz, nt),
        in_specs=[pl.BlockSpec((1, HG_WIDTH), lambda b, i: (0, 0)), fwd(), fwd(), fwd(), bwd(), bwd(), bwd()],
        out_specs=[fwd(), bwd()],
        out_shape=[shape, shape],
        scratch_shapes=[pltpu.VMEM((2, HG_HEADS, HG_DIM, HG_DIM), F32)],
        compiler_params=_params("parallel", "arbitrary"),
        name="hgrn2",
    )(lb, q_pre, f_fwd, i_in, q_pre, f_bwd, i_in)


def _gelu(x):
    return 0.5 * x * (1.0 + lax.erf(x * (1.0 / math.sqrt(2.0))))


def _merge_kernel(xp_ref, xs_ref, metap_ref, ys_ref, of_ref, ob_ref, gate_ref, wglu_ref, hgn_ref, whg_ref,
                  wout_ref, n2_ref, hs_ref, h2_ref, *, n_prompt):
    hs = _hidden_tile(xp_ref, xs_ref, metap_ref, n_prompt)
    glu = jnp.dot(_gelu(ys_ref[0]).astype(BF16), wglu_ref[...], preferred_element_type=F32)
    y_a = glu[:, :D_MODEL] * jax.nn.sigmoid(glu[:, D_MODEL:])
    o = of_ref[0] + ob_ref[0]
    normed = []
    for h in range(HG_HEADS):
        oh = o[:, h * HG_DIM:(h + 1) * HG_DIM]
        normed.append(oh * lax.rsqrt(jnp.mean(oh * oh, axis=-1, keepdims=True) + EPS))
    o_gate = gate_ref[0, :, 0:HG_WIDTH]
    y_hg = jnp.concatenate(normed, axis=-1) * hgn_ref[...] * (o_gate * jax.nn.sigmoid(o_gate))
    y_b = jnp.dot(y_hg.astype(BF16), whg_ref[...], preferred_element_type=F32)
    gate_a = gate_ref[0, :, HG_WIDTH:HG_WIDTH + D_MODEL]
    gate_b = gate_ref[0, :, HG_WIDTH + D_MODEL:]
    mixed = jax.nn.sigmoid(gate_a) * y_a + jax.nn.sigmoid(gate_b) * y_b
    hs = hs + jnp.dot(mixed.astype(BF16), wout_ref[...], preferred_element_type=F32)
    hs_ref[0] = hs
    ms = jnp.mean(hs * hs, axis=-1, keepdims=True)
    h2_ref[0] = _pack(hs * lax.rsqrt(ms + EPS) * n2_ref[...])


def _merge(x_prompt, x_sample, metap, y_s5, o_f, o_b, gates, w_glu, hg_norm_g, w_hg_out, w_out, norm2_g):
    bsz, p, _ = y_s5.shape
    nt = p // TILE
    n_prompt = x_prompt.shape[0]
    tok = lambda width: pl.BlockSpec((1, TILE, width), lambda b, i: (b, i, 0))
    full = lambda a: pl.BlockSpec(a.shape, lambda b, i: (0,) * a.ndim)
    return pl.pallas_call(
        functools.partial(_merge_kernel, n_prompt=n_prompt),
        grid=(bsz, nt),
        in_specs=_x_specs(n_prompt, nt) + [
            full(metap), tok(S5_WIDTH), tok(HG_WIDTH), tok(HG_WIDTH), tok(gates.shape[-1]),
            full(w_glu), full(hg_norm_g), full(w_hg_out), full(w_out), full(norm2_g),
        ],
        out_specs=[tok(D_MODEL), pl.BlockSpec((1, TILE // 2, D_MODEL), lambda b, i: (b, i, 0))],
        out_shape=[jax.ShapeDtypeStruct((bsz, p, D_MODEL), F32),
                   jax.ShapeDtypeStruct((bsz, p // 2, D_MODEL), jnp.uint32)],
        compiler_params=_params("arbitrary", "arbitrary"),
        name="merge",
    )(x_prompt, x_sample, metap, y_s5, o_f, o_b, gates, w_glu, hg_norm_g, w_hg_out, w_out, norm2_g)


def _staircase():
    return [(i, PEER_TOPK // (i + 1)) for i in range(PEER_TOPK)]


def _top16(s):
    work = s
    rank = jnp.full(s.shape, float(PEER_TOPK), F32)
    tops = []
    for it in range(PEER_TOPK):
        m = jnp.max(work, axis=0, keepdims=True)
        hit = work == m
        rank = jnp.where(hit, float(it), rank)
        work = jnp.where(hit, -jnp.inf, work)
        tops.append(m)
    return tops, rank


def _score_kernel(h2_ref, wq_ref, keys_ref, r2_ref, p2_ref, cnt_ref, p1_ref, cand_ref):
    h2 = _unpack(h2_ref[...])
    for h in range(PEER_HEADS):
        qh = jnp.dot(h2, wq_ref[:, h * PEER_QDIM:(h + 1) * PEER_QDIM], preferred_element_type=F32).astype(BF16)
        s1_all = lax.dot_general(keys_ref[h, 0], qh[:, :PEER_HALF], _NT, preferred_element_type=F32)
        s2_all = lax.dot_general(keys_ref[h, 1], qh[:, PEER_HALF:], _NT, preferred_element_type=F32)
        for part in range(TILE // 128):
            lanes = slice(part * 128, (part + 1) * 128)
            s1, s2 = s1_all[:, lanes], s2_all[:, lanes]
            t1, rank1 = _top16(s1)
            t2, rank2 = _top16(s2)
            t2_all = jnp.concatenate(t2, axis=0)
            cand_ref[:, lanes] = jnp.full((PEER_NCAND, 128), -jnp.inf, F32)
            off = 0
            for i, n_i in _staircase():
                cand_ref[off:off + n_i, lanes] = t1[i] + t2_all[0:n_i]
                off += n_i
            cand = cand_ref[:, lanes]
            c_max = t1[0] + t2[0]
            work = cand
            z = jnp.zeros_like(c_max)
            tau = c_max
            for it in range(PEER_TOPK):
                tau = jnp.max(work, axis=0, keepdims=True)
                z = z + jnp.exp(tau - c_max)
                work = jnp.where(work == tau, -jnp.inf, work)
            chosen = (cand >= tau).astype(F32)
            cnt = jnp.zeros_like(s1)
            off = 0
            for i, n_i in _staircase():
                cnt_i = jnp.sum(chosen[off:off + n_i], axis=0, keepdims=True)
                cnt = jnp.where(rank1 == float(i), cnt_i, cnt)
                off += n_i
            r2_ref[h, :, lanes] = _pack(rank2)
            p2_ref[h, :, lanes] = _pack(jnp.exp(s2 - t2[0]))
            cnt_ref[h, :, lanes] = cnt
            p1_ref[h, :, lanes] = 0.5 * jnp.exp(s1 - t1[0]) / z


def _peer_scores(h2, w_q, keys):
    ntok = 2 * h2.shape[0]
    blk = lambda rows: pl.BlockSpec((PEER_HEADS, rows, TILE), lambda i: (0, 0, i))
    packed = jax.ShapeDtypeStruct((PEER_HEADS, PEER_KEYS // 2, ntok), jnp.uint32)
    plain = jax.ShapeDtypeStruct((PEER_HEADS, PEER_KEYS, ntok), F32)
    return pl.pallas_call(
        _score_kernel,
        grid=(ntok // TILE,),
        in_specs=[
            pl.BlockSpec((TILE // 2, D_MODEL), lambda i: (i, 0)),
            pl.BlockSpec(w_q.shape, lambda i: (0, 0)),
            pl.BlockSpec(keys.shape, lambda i: (0, 0, 0, 0)),
        ],
        out_specs=[blk(PEER_KEYS // 2), blk(PEER_KEYS // 2), blk(PEER_KEYS), blk(PEER_KEYS)],
        out_shape=[packed, packed, plain, plain],
        scratch_shapes=[pltpu.VMEM((PEER_NCAND, TILE), F32)],
        compiler_params=_params("parallel"),
        name="peer_scores",
    )(h2, w_q, keys)


def _pack_tables_kernel(u_ref, v_ref, up_ref, vtp_ref):
    up_ref[...] = _pack(u_ref[...])
    vtp_ref[...] = _pack(v_ref[...].T)


def _pack_tables(u_tab, v_tab):
    nexp, d = u_tab.shape
    te = _largest_divisor(nexp, 512, 128)
    tile = lambda: pl.BlockSpec((te, d), lambda e: (e, 0))
    return pl.pallas_call(
        _pack_tables_kernel,
        grid=(nexp // te,),
        in_specs=[tile(), tile()],
        out_specs=[pl.BlockSpec((te // 2, d), lambda e: (e, 0)), pl.BlockSpec((d // 2, te), lambda e: (0, e))],
        out_shape=[jax.ShapeDtypeStruct((nexp // 2, d), jnp.uint32), jax.ShapeDtypeStruct((d // 2, nexp), jnp.uint32)],
        compiler_params=_params("parallel"),
        name="pack_tables",
    )(u_tab.astype(F32), v_tab.astype(F32))


def _dense_kernel(h2_ref, u_ref, vt_ref, r2_ref, p2_ref, cnt_ref, p1_ref, o_ref, act_ref, wa_ref, *, tn):
    e = pl.program_id(1)
    act_ref[...] = lax.dot_general(_unpack(u_ref[...]), _unpack(h2_ref[...]), _NT, preferred_element_type=F32)
    n_first = PEER_ETILE // PEER_KEYS
    zero = jnp.zeros((), BF16)
    spread = lambda row: jnp.broadcast_to(row, (PEER_KEYS, 128)).astype(BF16)
    for lc in range(tn // 128):
        lanes = slice(lc * 128, (lc + 1) * 128)
        cnts = [cnt_ref[h, :, lanes] for h in range(PEER_HEADS)]
        p1s = [p1_ref[h, :, lanes] for h in range(PEER_HEADS)]
        for j in range(0, n_first, 2):
            w0 = jnp.zeros((PEER_KEYS, 128), BF16)
            w1 = jnp.zeros((PEER_KEYS, 128), BF16)
            for h in range(PEER_HEADS):
                r2 = _unpack(r2_ref[h, :, lanes])
                p2 = _unpack(p2_ref[h, :, lanes])
                w0 += jnp.where(r2 < spread(cnts[h][j:j + 1]), p2, zero) * spread(p1s[h][j:j + 1])
                w1 += jnp.where(r2 < spread(cnts[h][j + 1:j + 2]), p2, zero) * spread(p1s[h][j + 1:j + 2])
            for k, w in ((j, w0), (j + 1, w1)):
                rows = slice(k * PEER_KEYS, (k + 1) * PEER_KEYS)
                act = act_ref[rows, lanes]
                twice_gelu = act * (1.0 + lax.erf(act * (1.0 / math.sqrt(2.0))))
                wa_ref[rows, lanes] = w * twice_gelu.astype(BF16)
    contrib = jnp.dot(_unpack(vt_ref[...]), wa_ref[...], preferred_element_type=F32)

    @pl.when(e == 0)
    def _():
        o_ref[...] = contrib

    @pl.when(e > 0)
    def _():
        o_ref[...] += contrib


def _peer_dense(h2, u_tab, vt_tab, r2, p2, cnt, p1):
    ntok = 2 * h2.shape[0]
    tn = _largest_divisor(ntok, PEER_TTILE, 256)
    nexp = 2 * u_tab.shape[0]
    second = lambda: pl.BlockSpec((PEER_HEADS, PEER_KEYS // 2, tn), lambda t, e: (0, 0, t))
    first = lambda: pl.BlockSpec((PEER_HEADS, PEER_ETILE // PEER_KEYS, tn), lambda t, e: (0, e, t))
    return pl.pallas_call(
        functools.partial(_dense_kernel, tn=tn),
        grid=(ntok // tn, nexp // PEER_ETILE),
        in_specs=[pl.BlockSpec((tn // 2, D_MODEL), lambda t, e: (t, 0)),
                  pl.BlockSpec((PEER_ETILE // 2, D_MODEL), lambda t, e: (e, 0)),
                  pl.BlockSpec((D_MODEL // 2, PEER_ETILE), lambda t, e: (0, e)),
                  second(), second(), first(), first()],
        out_specs=pl.BlockSpec((D_MODEL, tn), lambda t, e: (0, t)),
        out_shape=jax.ShapeDtypeStruct((D_MODEL, ntok), F32),
        scratch_shapes=[pltpu.VMEM((PEER_ETILE, tn), F32), pltpu.VMEM((PEER_ETILE, tn), BF16)],
        compiler_params=_params("parallel", "arbitrary"),
        name="peer_dense",
    )(h2, u_tab, vt_tab, r2, p2, cnt, p1)


def _final_kernel(ot_ref, hs_ref, g_ref, y_ref):
    hs = hs_ref[...] + ot_ref[...].T
    ms = jnp.mean(hs * hs, axis=-1, keepdims=True)
    y_ref[0] = hs * lax.rsqrt(ms + EPS) * g_ref[...]


def _final(o_t, hs, final_g, first_seq, n_seq, nt):
    tile_of = lambda b, i: (b + first_seq) * nt + i + 1
    return pl.pallas_call(
        _final_kernel,
        grid=(n_seq, nt - 1),
        in_specs=[
            pl.BlockSpec((D_MODEL, TILE), lambda b, i: (0, tile_of(b, i))),
            pl.BlockSpec((TILE, D_MODEL), lambda b, i: (tile_of(b, i), 0)),
            pl.BlockSpec((1, D_MODEL), lambda b, i: (0, 0)),
        ],
        out_specs=pl.BlockSpec((1, TILE, D_MODEL), lambda b, i: (b, i, 0)),
        out_shape=jax.ShapeDtypeStruct((n_seq, (nt - 1) * TILE, D_MODEL), F32),
        compiler_params=_params("parallel", "parallel"),
        name="final",
    )(o_t, hs, final_g)


def kernel(x_prompt, x_sample, meta, norm1_g, w_in, s5_lam_re, s5_lam_im, s5_log_step, s5_b_re, s5_b_im,
           s5_c_re, s5_c_im, s5_d, w_glu, hg_lb, hg_norm_g, w_hg_out, w_out, norm2_g, peer_wq, peer_keys,
           peer_u, peer_v, final_g):
    assert x_prompt.shape[1] == x_sample.shape[1] and x_prompt.shape[1] % TILE == 0
    assert norm1_g.shape[0] == 1, "single-layer trunk"
    n_prompt = x_prompt.shape[0]
    x_prompt, x_sample = x_prompt.astype(F32), x_sample.astype(F32)
    metap = jnp.concatenate([jnp.zeros((FRONT, D_MODEL), F32), meta.astype(F32)], axis=0)
    row = lambda a: a.astype(F32).reshape(1, -1)

    u, q_pre, f_fwd, f_bwd, i_in, gates = _inproj(x_prompt, x_sample, metap, row(norm1_g[0]),
                                                  w_in[0].astype(BF16))

    y_s5 = _s5_mixer(u, *_s5_weights(s5_lam_re[0], s5_lam_im[0], s5_log_step[0], s5_b_re[0], s5_b_im[0],
                                     s5_c_re[0], s5_c_im[0], s5_d[0]))

    lb = jax.nn.softmax(hg_lb.astype(F32), axis=0)[0]
    o_f, o_b = _hgrn2(q_pre, f_fwd, f_bwd, i_in, row(lb))

    hs, h2 = _merge(x_prompt, x_sample, metap, y_s5, o_f, o_b, gates, w_glu[0].astype(BF16), row(hg_norm_g[0]),
                    w_hg_out[0].astype(BF16), w_out[0].astype(BF16), row(norm2_g[0]))

    bsz, p, _ = hs.shape
    hs = hs.reshape(bsz * p, D_MODEL)
    h2 = h2.reshape(bsz * p // 2, D_MODEL)
    r2, p2, cnt, p1 = _peer_scores(h2, peer_wq[0].astype(BF16), peer_keys[0].astype(BF16))
    u_tab, vt_tab = _pack_tables(peer_u[0], peer_v[0])
    o_t = _peer_dense(h2, u_tab, vt_tab, r2, p2, cnt, p1)
    fin = functools.partial(_final, o_t, hs, row(final_g), nt=p // TILE)
    return (fin(first_seq=0, n_seq=n_prompt), fin(first_seq=n_prompt, n_seq=bsz - n_prompt))
```

```python
import functools
import math

import jax
import jax.numpy as jnp
from jax import lax
from jax.experimental import pallas as pl
from jax.experimental.pallas import tpu as pltpu

F32 = jnp.float32
BF16 = jnp.bfloat16

D_MODEL = 1024
N_META = 16
EPS = 1e-6
TILE = 256
FRONT = TILE - N_META

S5_WIDTH = 512
S5_GROUP = 16
S5_GROUPS = 32
S5_STATE = 64
S5_CHUNK = 16
S5_LB = 128 // S5_GROUP
S5_NLB = S5_GROUPS // S5_LB
S5_K = S5_CHUNK * 128

HG_WIDTH = 512
HG_HEADS = 4
HG_DIM = 128
HG_CHUNK = 64

PEER_HEADS = 8
PEER_KEYS = 128
PEER_TOPK = 16
PEER_QDIM = 256
PEER_HALF = 128
PEER_ETILE = 1024
PEER_TTILE = 1280
PEER_NCAND = 56

VMEM_LIMIT = 56 * 1024 * 1024

_NT = (((1,), (1,)), ((), ()))
_TN = (((0,), (0,)), ((), ()))


def _largest_divisor(n, target, multiple=1):
    best = None
    for d in range(multiple, min(n, target) + 1, multiple):
        if n % d == 0:
            best = d
    assert best is not None, (n, target, multiple)
    return best


def _pack(x):
    return pltpu.bitcast(x.astype(BF16), jnp.uint32)


def _unpack(x):
    return pltpu.bitcast(x, BF16)


def _params(*sem):
    return pltpu.CompilerParams(dimension_semantics=sem, vmem_limit_bytes=VMEM_LIMIT)


def _hidden_tile(xp_ref, xs_ref, metap_ref, n_prompt):
    b, i = pl.program_id(0), pl.program_id(1)
    x = jnp.where(b < n_prompt, xp_ref[0], xs_ref[0])
    return jnp.where(i == 0, metap_ref[...], x)


def _x_specs(n_prompt, nt):
    tile = lambda i: jnp.maximum(i - 1, 0)
    prompt = pl.BlockSpec((1, TILE, D_MODEL), lambda b, i: (
        jnp.minimum(b, n_prompt - 1), jnp.where(b < n_prompt, tile(i), nt - 2), 0))
    sample = pl.BlockSpec((1, TILE, D_MODEL), lambda b, i: (
        jnp.maximum(b - n_prompt, 0), jnp.where(b < n_prompt, 0, tile(i)), 0))
    return [prompt, sample]


def _inproj_kernel(xp_ref, xs_ref, metap_ref, g_ref, w_ref, u_ref, q_ref, ff_ref, fb_ref, v_ref, gate_ref, *,
                   n_prompt):
    i = pl.program_id(1)
    hs = _hidden_tile(xp_ref, xs_ref, metap_ref, n_prompt)
    ms = jnp.mean(hs * hs, axis=-1, keepdims=True)
    h = (hs * lax.rsqrt(ms + EPS) * g_ref[...]).astype(BF16)
    row = lax.broadcasted_iota(jnp.int32, (TILE, 1), 0)
    valid = jnp.logical_or(i > 0, row >= FRONT)
    u = jnp.dot(h, w_ref[:, 0:S5_WIDTH], preferred_element_type=F32)
    u_ref[0] = jnp.where(valid, u, 0.0)
    off = S5_WIDTH
    for ref in (q_ref, ff_ref, fb_ref, v_ref):
        ref[0] = jnp.dot(h, w_ref[:, off:off + HG_WIDTH], preferred_element_type=F32)
        off += HG_WIDTH
    gate_ref[0] = jnp.dot(h, w_ref[:, off:], preferred_element_type=F32)


def _inproj(x_prompt, x_sample, metap, g, w_in):
    n_prompt, length, _ = x_prompt.shape
    bsz = n_prompt + x_sample.shape[0]
    nt = length // TILE + 1
    p = nt * TILE
    ncols = w_in.shape[1]
    ngate = ncols - S5_WIDTH - 4 * HG_WIDTH
    tok = lambda width: pl.BlockSpec((1, TILE, width), lambda b, i: (b, i, 0))
    full = lambda shape: pl.BlockSpec(shape, lambda b, i: (0,) * len(shape))
    hg_shape = jax.ShapeDtypeStruct((bsz, p, HG_WIDTH), F32)
    return pl.pallas_call(
        functools.partial(_inproj_kernel, n_prompt=n_prompt),
        grid=(bsz, nt),
        in_specs=_x_specs(n_prompt, nt) + [full((TILE, D_MODEL)), full((1, D_MODEL)), full((D_MODEL, ncols))],
        out_specs=[tok(S5_WIDTH), tok(HG_WIDTH), tok(HG_WIDTH), tok(HG_WIDTH), tok(HG_WIDTH), tok(ngate)],
        out_shape=[hg_shape, hg_shape, hg_shape, hg_shape, hg_shape, jax.ShapeDtypeStruct((bsz, p, ngate), F32)],
        compiler_params=_params("arbitrary", "arbitrary"),
        name="inproj",
    )(x_prompt, x_sample, metap, g, w_in)


def _s5_weights(lam_re, lam_im, log_step, b_re, b_im, c_re, c_im, d_skip):
    hi = lax.Precision.HIGHEST
    f = lambda a: a.astype(F32)
    lam_re, lam_im, log_step = f(lam_re), f(lam_im), f(log_step)
    b_re, b_im, c_re, c_im, d_skip = f(b_re), f(b_im), f(c_re), f(c_im), f(d_skip)
    step = jnp.exp(log_step)[:, :, None]
    pw = jnp.arange(S5_CHUNK + 1, dtype=F32)[:, None, None, None]
    mag = jnp.exp(pw * lam_re[None] * step[None])
    ang = pw * lam_im[None] * step[None]
    a_re, a_im = mag * jnp.cos(ang), mag * jnp.sin(ang)
    den = lam_re * lam_re + lam_im * lam_im
    n_re, n_im = a_re[1] - 1.0, a_im[1]
    coef_re = (n_re * lam_re + n_im * lam_im) / den
    coef_im = (n_im * lam_re - n_re * lam_im) / den
    w_re = coef_re[None] * a_re - coef_im[None] * a_im
    w_im = coef_re[None] * a_im + coef_im[None] * a_re

    def conv_taps(direction):
        wr, wi = w_re[:S5_CHUNK, direction], w_im[:S5_CHUNK, direction]
        m_re = wr[..., None] * b_re[None] - wi[..., None] * b_im[None]
        m_im = wr[..., None] * b_im[None] + wi[..., None] * b_re[None]
        return (jnp.einsum('gon,dgni->dgoi', c_re, m_re, precision=hi)
                - jnp.einsum('gon,dgni->dgoi', c_im, m_im, precision=hi))

    k_f, k_b = conv_taps(0), conv_taps(1)
    center = k_f[0] + k_b[0] + jnp.eye(S5_GROUP, dtype=F32)[None] * d_skip.reshape(S5_GROUPS, S5_GROUP, 1)
    taps = jnp.concatenate([k_b[:0:-1], center[None], k_f[1:]], axis=0)
    t_idx = jnp.arange(S5_CHUNK)
    toe = taps[t_idx[None, :] - t_idx[:, None] + S5_CHUNK - 1]
    split = lambda a, axis: a.reshape(a.shape[:axis] + (S5_NLB, S5_LB) + a.shape[axis + 1:])

    def block_diag(src):
        src = src.astype(BF16)
        width = src.shape[-1]
        lane_pad = lambda g: [(0, 0, 0)] * 4 + [(g * width, (S5_LB - 1 - g) * width, 0)]
        parts = [lax.pad(src[:, g], jnp.zeros((), BF16), lane_pad(g)) for g in range(S5_LB)]
        return jnp.stack(parts, axis=2).reshape(S5_NLB, S5_K, S5_K)

    toe = block_diag(split(toe.transpose(2, 0, 4, 1, 3), 0))

    def end_map(direction, powers):
        pr, pi = a_re[powers, direction], a_im[powers, direction]
        return [pr[..., None] * b_re[None] - pi[..., None] * b_im[None],
                pr[..., None] * b_im[None] + pi[..., None] * b_re[None]]

    ends = jnp.stack(end_map(0, S5_CHUNK - 1 - t_idx) + end_map(1, t_idx))
    ends = block_diag(split(ends.transpose(2, 1, 4, 0, 3), 0))

    def out_map(direction, powers):
        wr, wi = w_re[powers, direction], w_im[powers, direction]
        return [c_re[None] * wr[:, :, None, :] - c_im[None] * wi[:, :, None, :],
                -c_re[None] * wi[:, :, None, :] - c_im[None] * wr[:, :, None, :]]

    outs = jnp.stack(out_map(0, t_idx + 1) + out_map(1, S5_CHUNK - t_idx))
    outs = block_diag(split(outs.transpose(2, 0, 4, 1, 3), 0))

    def decay(direction):
        ar = split(a_re[S5_CHUNK, direction], 0).reshape(S5_NLB, 1, S5_K // 4)
        ai = split(a_im[S5_CHUNK, direction], 0).reshape(S5_NLB, 1, S5_K // 4)
        same = jnp.concatenate([ar, ar], axis=2).reshape(1, S5_NLB * S5_K // 2)
        cross = jnp.concatenate([-ai, ai], axis=2).reshape(1, S5_NLB * S5_K // 2)
        return same, cross

    return toe.astype(BF16), ends.astype(BF16), outs.astype(BF16), decay(0) + decay(1)


def _chunk_rows(u_ref, nc):
    parts = [u_ref[0, pl.ds(s, nc, stride=S5_CHUNK), :] for s in range(S5_CHUNK)]
    return jnp.concatenate(parts, axis=1).astype(BF16)


def _s5_local_kernel(u_ref, w_ref, ef_ref, eb_ref, *, nc):
    e = jnp.dot(_chunk_rows(u_ref, nc), w_ref[0], preferred_element_type=F32)
    ef_ref[0] = e[:, :S5_K // 2]
    eb_ref[0] = e[:, S5_K // 2:]


def _s5_local(u, ends, nc):
    bsz, p, _ = u.shape
    nch = p // S5_CHUNK
    out = lambda: pl.BlockSpec((1, nc, S5_K // 2), lambda lb, b, i: (b, i, lb))
    shape = jax.ShapeDtypeStruct((bsz, nch, S5_NLB * S5_K // 2), F32)
    return pl.pallas_call(
        functools.partial(_s5_local_kernel, nc=nc),
        grid=(S5_NLB, bsz, nch // nc),
        in_specs=[pl.BlockSpec((1, nc * S5_CHUNK, 128), lambda lb, b, i: (b, i, lb)),
                  pl.BlockSpec((1, S5_K, S5_K), lambda lb, b, i: (lb, 0, 0))],
        out_specs=[out(), out()],
        out_shape=[shape, shape],
        compiler_params=_params("arbitrary", "arbitrary", "arbitrary"),
        name="s5_local",
    )(u, ends)


def _swap_re_im(x):
    half = S5_K // 4
    parts = []
    for lb in range(x.shape[1] // (2 * half)):
        parts += [x[:, (2 * lb + 1) * half:(2 * lb + 2) * half], x[:, 2 * lb * half:(2 * lb + 1) * half]]
    return jnp.concatenate(parts, axis=1)


def _s5_scan_kernel(fs_ref, fc_ref, bs_ref, bc_ref, ef_ref, eb_ref, xf_ref, xb_ref, st_ref, *, ct):
    j = pl.program_id(1)

    @pl.when(j == 0)
    def _():
        st_ref[...] = jnp.zeros_like(st_ref)

    width = S5_K
    for part in range(st_ref.shape[2] // width):
        lanes = slice(part * width, (part + 1) * width)
        f_same, f_cross = fs_ref[:, lanes], fc_ref[:, lanes]
        b_same, b_cross = bs_ref[:, lanes], bc_ref[:, lanes]

        def body(c, carry):
            sf, sb = carry
            xf_ref[0, pl.ds(c, 1), lanes] = sf
            sf = f_same * sf + f_cross * _swap_re_im(sf) + ef_ref[0, pl.ds(c, 1), lanes]
            cb = ct - 1 - c
            xb_ref[0, pl.ds(cb, 1), lanes] = sb
            sb = b_same * sb + b_cross * _swap_re_im(sb) + eb_ref[0, pl.ds(cb, 1), lanes]
            return sf, sb

        sf, sb = lax.fori_loop(0, ct, body, (st_ref[0, :, lanes], st_ref[1, :, lanes]))
        st_ref[0, :, lanes] = sf
        st_ref[1, :, lanes] = sb


def _s5_scan(ef, eb, dec):
    bsz, nch, width = ef.shape
    ct = _largest_divisor(nch, 208, 8)
    nj = nch // ct
    fwd = lambda: pl.BlockSpec((1, ct, width), lambda b, j: (b, j, 0))
    bwd = lambda: pl.BlockSpec((1, ct, width), lambda b, j: (b, nj - 1 - j, 0))
    row = lambda: pl.BlockSpec((1, width), lambda b, j: (0, 0))
    shape = jax.ShapeDtypeStruct(ef.shape, F32)
    return pl.pallas_call(
        functools.partial(_s5_scan_kernel, ct=ct),
        grid=(bsz, nj),
        in_specs=[row(), row(), row(), row(), fwd(), bwd()],
        out_specs=[fwd(), bwd()],
        out_shape=[shape, shape],
        scratch_shapes=[pltpu.VMEM((2, 1, width), F32)],
        compiler_params=_params("parallel", "arbitrary"),
        name="s5_scan",
    )(*dec, ef, eb)


def _s5_out_kernel(u_ref, xf_ref, xb_ref, toe_ref, outs_ref, y_ref, *, nc):
    states = jnp.concatenate([xf_ref[0], xb_ref[0]], axis=1).astype(BF16)
    y = jnp.dot(_chunk_rows(u_ref, nc), toe_ref[0], preferred_element_type=F32)
    y += jnp.dot(states, outs_ref[0], preferred_element_type=F32)
    for t in range(S5_CHUNK):
        y_ref[0, pl.ds(t, nc, stride=S5_CHUNK), :] = y[:, t * 128:(t + 1) * 128]


def _s5_out(u, xf, xb, toe, outs, nc):
    bsz, p, _ = u.shape
    nch = p // S5_CHUNK
    tok = lambda: pl.BlockSpec((1, nc * S5_CHUNK, 128), lambda lb, b, i: (b, i, lb))
    state = lambda: pl.BlockSpec((1, nc, S5_K // 2), lambda lb, b, i: (b, i, lb))
    weight = lambda: pl.BlockSpec((1, S5_K, S5_K), lambda lb, b, i: (lb, 0, 0))
    return pl.pallas_call(
        functools.partial(_s5_out_kernel, nc=nc),
        grid=(S5_NLB, bsz, nch // nc),
        in_specs=[tok(), state(), state(), weight(), weight()],
        out_specs=tok(),
        out_shape=jax.ShapeDtypeStruct(u.shape, F32),
        compiler_params=_params("arbitrary", "arbitrary", "arbitrary"),
        name="s5_out",
    )(u, xf, xb, toe, outs)


def _s5_mixer(u, toe, ends, outs, dec):
    nch = u.shape[1] // S5_CHUNK
    nc = _largest_divisor(nch, 260, 8)
    ef, eb = _s5_local(u, ends, nc)
    xf, xb = _s5_scan(ef, eb, dec)
    return _s5_out(u, xf, xb, toe, outs, nc)


def _hg_kernel(lb_ref, qf_ref, ff_ref, vf_ref, qb_ref, fb_ref, vb_ref, of_ref, ob_ref, st_ref):
    i = pl.program_id(1)
    n = pl.num_programs(1)

    @pl.when(i == 0)
    def _():
        st_ref[...] = jnp.zeros_like(st_ref)

    lb = lb_ref[...]
    n_chunks = TILE // HG_CHUNK
    shift = HG_CHUNK.bit_length() - 1
    r = lax.broadcasted_iota(jnp.int32, (TILE, TILE), 0)
    c = lax.broadcasted_iota(jnp.int32, (TILE, TILE), 1)
    same_chunk = jnp.right_shift(r, shift) == jnp.right_shift(c, shift)
    r64 = lax.broadcasted_iota(jnp.int32, (HG_CHUNK, HG_CHUNK), 0)
    c64 = lax.broadcasted_iota(jnp.int32, (HG_CHUNK, HG_CHUNK), 1)
    row = lax.broadcasted_iota(jnp.int32, (TILE, 1), 0)
    passes = (
        (0, qf_ref, ff_ref, vf_ref, of_ref, i, c <= r, c64 <= r64, range(n_chunks), HG_CHUNK - 1),
        (1, qb_ref, fb_ref, vb_ref, ob_ref, n - 1 - i, c >= r, c64 >= r64, reversed(range(n_chunks)), 0),
    )
    for d, q_ref, f_ref, v_ref, o_ref, tile, before, keep, chunks, tot_row in passes:
        q_pre = q_ref[0]
        q = q_pre * jax.nn.sigmoid(q_pre)
        g = lb + (1.0 - lb) * jax.nn.sigmoid(f_ref[0])
        valid = tile * TILE + row >= FRONT
        log_f = jnp.where(valid, jnp.log(g), 0.0)
        k = jnp.where(valid, 1.0 - g, 0.0)
        cum = jnp.logical_and(same_chunk, before).astype(F32)
        b = jnp.dot(cum, log_f, preferred_element_type=F32, precision=lax.Precision.HIGHEST)
        totals = [b[ch * HG_CHUNK + tot_row:ch * HG_CHUNK + tot_row + 1] for ch in range(n_chunks)]
        b_tot = jnp.concatenate([jnp.broadcast_to(t, (HG_CHUNK, HG_WIDTH)) for t in totals], axis=0)
        q_dec = (q * jnp.exp(b)).astype(BF16)
        k_inv = (k * jnp.exp(-b)).astype(BF16)
        k_end = (k * jnp.exp(b_tot - b)).astype(BF16)
        v = v_ref[0].astype(BF16)
        states = [st_ref[d, h] for h in range(HG_HEADS)]
        for ch in chunks:
            rows = slice(ch * HG_CHUNK, (ch + 1) * HG_CHUNK)
            decay = jnp.exp(totals[ch])
            for h in range(HG_HEADS):
                ls = slice(h * HG_DIM, (h + 1) * HG_DIM)
                scores = lax.dot_general(q_dec[rows, ls], k_inv[rows, ls], _NT, preferred_element_type=F32)
                scores = jnp.where(keep, scores, 0.0).astype(BF16)
                o = jnp.dot(scores, v[rows, ls], preferred_element_type=F32)
                o += lax.dot_general(q_dec[rows, ls], states[h].astype(BF16), _NT, preferred_element_type=F32)
                kv = lax.dot_general(v[rows, ls], k_end[rows, ls], _TN, preferred_element_type=F32)
                states[h] = decay[:, ls] * states[h] + kv
                o_ref[0, rows, ls] = o
        for h in range(HG_HEADS):
            st_ref[d, h] = states[h]


def _hgrn2(q_pre, f_fwd, f_bwd, i_in, lb):
    bsz, p, _ = q_pre.shape
    nt = p // TILE
    fwd = lambda: pl.BlockSpec((1, TILE, HG_WIDTH), lambda b, i: (b, i, 0))
    bwd = lambda: pl.BlockSpec((1, TILE, HG_WIDTH), lambda b, i: (b, nt - 1 - i, 0))
    shape = jax.ShapeDtypeStruct((bsz, p, HG_WIDTH), F32)
    return pl.pallas_call(
        _hg_kernel,
        grid=(bsz, nt),
        in_specs=[pl.BlockSpec((1, HG_WIDTH), lambda b, i: (0, 0)), fwd(), fwd(), fwd(), bwd(), bwd(), bwd()],
        out_specs=[fwd(), bwd()],
        out_shape=[shape, shape],
        scratch_shapes=[pltpu.VMEM((2, HG_HEADS, HG_DIM, HG_DIM), F32)],
        compiler_params=_params("parallel", "arbitrary"),
        name="hgrn2",
    )(lb, q_pre, f_fwd, i_in, q_pre, f_bwd, i_in)


def _gelu(x):
    return 0.5 * x * (1.0 + lax.erf(x * (1.0 / math.sqrt(2.0))))


def _merge_kernel(xp_ref, xs_ref, metap_ref, ys_ref, of_ref, ob_ref, gate_ref, wglu_ref, hgn_ref, whg_ref,
                  wout_ref, n2_ref, hs_ref, h2_ref, *, n_prompt):
    hs = _hidden_tile(xp_ref, xs_ref, metap_ref, n_prompt)
    glu = jnp.dot(_gelu(ys_ref[0]).astype(BF16), wglu_ref[...], preferred_element_type=F32)
    y_a = glu[:, :D_MODEL] * jax.nn.sigmoid(glu[:, D_MODEL:])
    o = of_ref[0] + ob_ref[0]
    normed = []
    for h in range(HG_HEADS):
        oh = o[:, h * HG_DIM:(h + 1) * HG_DIM]
        normed.append(oh * lax.rsqrt(jnp.mean(oh * oh, axis=-1, keepdims=True) + EPS))
    o_gate = gate_ref[0, :, 0:HG_WIDTH]
    y_hg = jnp.concatenate(normed, axis=-1) * hgn_ref[...] * (o_gate * jax.nn.sigmoid(o_gate))
    y_b = jnp.dot(y_hg.astype(BF16), whg_ref[...], preferred_element_type=F32)
    gate_a = gate_ref[0, :, HG_WIDTH:HG_WIDTH + D_MODEL]
    gate_b = gate_ref[0, :, HG_WIDTH + D_MODEL:]
    mixed = jax.nn.sigmoid(gate_a) * y_a + jax.nn.sigmoid(gate_b) * y_b
    hs = hs + jnp.dot(mixed.astype(BF16), wout_ref[...], preferred_element_type=F32)
    hs_ref[0] = hs
    ms = jnp.mean(hs * hs, axis=-1, keepdims=True)
    h2_ref[0] = _pack(hs * lax.rsqrt(ms + EPS) * n2_ref[...])


def _merge(x_prompt, x_sample, metap, y_s5, o_f, o_b, gates, w_glu, hg_norm_g, w_hg_out, w_out, norm2_g):
    bsz, p, _ = y_s5.shape
    nt = p // TILE
    n_prompt = x_prompt.shape[0]
    tok = lambda width: pl.BlockSpec((1, TILE, width), lambda b, i: (b, i, 0))
    full = lambda a: pl.BlockSpec(a.shape, lambda b, i: (0,) * a.ndim)
    return pl.pallas_call(
        functools.partial(_merge_kernel, n_prompt=n_prompt),
        grid=(bsz, nt),
        in_specs=_x_specs(n_prompt, nt) + [
            full(metap), tok(S5_WIDTH), tok(HG_WIDTH), tok(HG_WIDTH), tok(gates.shape[-1]),
            full(w_glu), full(hg_norm_g), full(w_hg_out), full(w_out), full(norm2_g),
        ],
        out_specs=[tok(D_MODEL), pl.BlockSpec((1, TILE // 2, D_MODEL), lambda b, i: (b, i, 0))],
        out_shape=[jax.ShapeDtypeStruct((bsz, p, D_MODEL), F32),
                   jax.ShapeDtypeStruct((bsz, p // 2, D_MODEL), jnp.uint32)],
        compiler_params=_params("arbitrary", "arbitrary"),
        name="merge",
    )(x_prompt, x_sample, metap, y_s5, o_f, o_b, gates, w_glu, hg_norm_g, w_hg_out, w_out, norm2_g)


def _staircase():
    return [(i, PEER_TOPK // (i + 1)) for i in range(PEER_TOPK)]


def _top16(s):
    work = s
    rank = jnp.full(s.shape, float(PEER_TOPK), F32)
    tops = []
    for it in range(PEER_TOPK):
        m = jnp.max(work, axis=0, keepdims=True)
        hit = work == m
        rank = jnp.where(hit, float(it), rank)
        work = jnp.where(hit, -jnp.inf, work)
        tops.append(m)
    return tops, rank


def _score_kernel(h2_ref, wq_ref, keys_ref, r2_ref, p2_ref, cnt_ref, p1_ref, cand_ref):
    h2 = _unpack(h2_ref[...])
    for h in range(PEER_HEADS):
        qh = jnp.dot(h2, wq_ref[:, h * PEER_QDIM:(h + 1) * PEER_QDIM], preferred_element_type=F32).astype(BF16)
        s1_all = lax.dot_general(keys_ref[h, 0], qh[:, :PEER_HALF], _NT, preferred_element_type=F32)
        s2_all = lax.dot_general(keys_ref[h, 1], qh[:, PEER_HALF:], _NT, preferred_element_type=F32)
        for part in range(TILE // 128):
            lanes = slice(part * 128, (part + 1) * 128)
            s1, s2 = s1_all[:, lanes], s2_all[:, lanes]
            t1, rank1 = _top16(s1)
            t2, rank2 = _top16(s2)
            t2_all = jnp.concatenate(t2, axis=0)
            cand_ref[:, lanes] = jnp.full((PEER_NCAND, 128), -jnp.inf, F32)
            off = 0
            for i, n_i in _staircase():
                cand_ref[off:off + n_i, lanes] = t1[i] + t2_all[0:n_i]
                off += n_i
            cand = cand_ref[:, lanes]
            c_max = t1[0] + t2[0]
            work = cand
            z = jnp.zeros_like(c_max)
            tau = c_max
            for it in range(PEER_TOPK):
                tau = jnp.max(work, axis=0, keepdims=True)
                z = z + jnp.exp(tau - c_max)
                work = jnp.where(work == tau, -jnp.inf, work)
            chosen = (cand >= tau).astype(F32)
            cnt = jnp.zeros_like(s1)
            off = 0
            for i, n_i in _staircase():
                cnt_i = jnp.sum(chosen[off:off + n_i], axis=0, keepdims=True)
                cnt = jnp.where(rank1 == float(i), cnt_i, cnt)
                off += n_i
            r2_ref[h, 0, :, lanes] = _pack(rank2)
            p2_ref[h, 0, :, lanes] = _pack(jnp.exp(s2 - t2[0]))
            cnt_ref[h, 0, :, lanes] = cnt
            p1_ref[h, 0, :, lanes] = 0.5 * jnp.exp(s1 - t1[0]) / z


def _peer_scores(h2, w_q, keys):
    ntok = 2 * h2.shape[0]
    blk = lambda rows: pl.BlockSpec((PEER_HEADS, 1, rows, TILE), lambda i: (0, i, 0, 0))
    packed = jax.ShapeDtypeStruct((PEER_HEADS, ntok // TILE, PEER_KEYS // 2, TILE), jnp.uint32)
    plain = jax.ShapeDtypeStruct((PEER_HEADS, ntok // TILE, PEER_KEYS, TILE), F32)
    return pl.pallas_call(
        _score_kernel,
        grid=(ntok // TILE,),
        in_specs=[
            pl.BlockSpec((TILE // 2, D_MODEL), lambda i: (i, 0)),
            pl.BlockSpec(w_q.shape, lambda i: (0, 0)),
            pl.BlockSpec(keys.shape, lambda i: (0, 0, 0, 0)),
        ],
        out_specs=[blk(PEER_KEYS // 2), blk(PEER_KEYS // 2), blk(PEER_KEYS), blk(PEER_KEYS)],
        out_shape=[packed, packed, plain, plain],
        scratch_shapes=[pltpu.VMEM((PEER_NCAND, TILE), F32)],
        compiler_params=_params("parallel"),
        name="peer_scores",
    )(h2, w_q, keys)


def _pack_tables_kernel(u_ref, v_ref, up_ref, vtp_ref):
    up_ref[...] = _pack(u_ref[...])
    vtp_ref[...] = _pack(v_ref[...].T)


def _pack_tables(u_tab, v_tab):
    nexp, d = u_tab.shape
    te = _largest_divisor(nexp, 512, 128)
    tile = lambda: pl.BlockSpec((te, d), lambda e: (e, 0))
    return pl.pallas_call(
        _pack_tables_kernel,
        grid=(nexp // te,),
        in_specs=[tile(), tile()],
        out_specs=[pl.BlockSpec((te // 2, d), lambda e: (e, 0)), pl.BlockSpec((d // 2, te), lambda e: (0, e))],
        out_shape=[jax.ShapeDtypeStruct((nexp // 2, d), jnp.uint32), jax.ShapeDtypeStruct((d // 2, nexp), jnp.uint32)],
        compiler_params=_params("parallel"),
        name="pack_tables",
    )(u_tab.astype(F32), v_tab.astype(F32))


def _dense_kernel(h2_ref, u_ref, vt_ref, r2_ref, p2_ref, cnt_ref, p1_ref, o_ref, act0_ref, act1_ref, wa0_ref,
                  wa1_ref, *, n_sub):
    s = pl.program_id(1)

    @pl.when(s == 0)
    def _():
        for ref in (act0_ref, act1_ref, wa0_ref, wa1_ref, o_ref):
            ref[...] = jnp.zeros_like(ref)

    n_first = PEER_ETILE // PEER_KEYS
    zero = jnp.zeros((), BF16)
    spread = lambda row: jnp.broadcast_to(row, (PEER_KEYS, 128)).astype(BF16)

    def sub_tile(c, carry, *, act_a, act_b, wa_b, wa_c):
        h2 = _unpack(h2_ref[pl.ds(pl.multiple_of(c * (TILE // 2), TILE // 2), TILE // 2), :])
        act_a[c] = lax.dot_general(_unpack(u_ref[...]), h2, _NT, preferred_element_type=F32)
        for lc in range(TILE // 128):
            lanes = slice(lc * 128, (lc + 1) * 128)
            for j in range(0, n_first, 2):
                w0 = jnp.zeros((PEER_KEYS, 128), BF16)
                w1 = jnp.zeros((PEER_KEYS, 128), BF16)
                for h in range(PEER_HEADS):
                    r2 = _unpack(r2_ref[h, c, :, lanes])
                    p2 = _unpack(p2_ref[h, c, :, lanes])
                    cnt = cnt_ref[h, c, :, lanes]
                    p1 = p1_ref[h, c, :, lanes]
                    w0 += jnp.where(r2 < spread(cnt[j:j + 1]), p2, zero) * spread(p1[j:j + 1])
                    w1 += jnp.where(r2 < spread(cnt[j + 1:j + 2]), p2, zero) * spread(p1[j + 1:j + 2])
                for k, w in ((j, w0), (j + 1, w1)):
                    rows = slice(k * PEER_KEYS, (k + 1) * PEER_KEYS)
                    act = act_b[c, rows, lanes]
                    twice_gelu = act * (1.0 + lax.erf(act * (1.0 / math.sqrt(2.0))))
                    wa_b[c, rows, lanes] = w * twice_gelu.astype(BF16)
        contrib = jnp.dot(_unpack(vt_ref[...]), wa_c[c], preferred_element_type=F32)
        o_ref[c] += jnp.where(s >= 2, contrib, 0.0)
        return carry

    @pl.when(lax.rem(s, 2) == 0)
    def _():
        lax.fori_loop(0, n_sub, functools.partial(sub_tile, act_a=act0_ref, act_b=act1_ref, wa_b=wa1_ref,
                                                  wa_c=wa0_ref), 0)

    @pl.when(lax.rem(s, 2) == 1)
    def _():
        lax.fori_loop(0, n_sub, functools.partial(sub_tile, act_a=act1_ref, act_b=act0_ref, wa_b=wa0_ref,
                                                  wa_c=wa1_ref), 0)


def _peer_dense(h2, u_tab, vt_tab, r2, p2, cnt, p1):
    ntok = 2 * h2.shape[0]
    n_sub = _largest_divisor(ntok // TILE, PEER_TTILE // TILE)
    tn = n_sub * TILE
    n_etiles = 2 * u_tab.shape[0] // PEER_ETILE
    last = n_etiles - 1
    stage = lambda s, lag: jnp.clip(s - lag, 0, last)
    second = lambda: pl.BlockSpec((PEER_HEADS, n_sub, PEER_KEYS // 2, TILE), lambda t, s: (0, t, 0, 0))
    first = lambda: pl.BlockSpec((PEER_HEADS, n_sub, PEER_ETILE // PEER_KEYS, TILE),
                                 lambda t, s: (0, t, stage(s, 1), 0))
    return pl.pallas_call(
        functools.partial(_dense_kernel, n_sub=n_sub),
        grid=(ntok // tn, n_etiles + 2),
        in_specs=[pl.BlockSpec((tn // 2, D_MODEL), lambda t, s: (t, 0)),
                  pl.BlockSpec((PEER_ETILE // 2, D_MODEL), lambda t, s: (stage(s, 0), 0)),
                  pl.BlockSpec((D_MODEL // 2, PEER_ETILE), lambda t, s: (0, stage(s, 2))),
                  second(), second(), first(), first()],
        out_specs=pl.BlockSpec((n_sub, D_MODEL, TILE), lambda t, s: (t, 0, 0)),
        out_shape=jax.ShapeDtypeStruct((ntok // TILE, D_MODEL, TILE), F32),
        scratch_shapes=[pltpu.VMEM((n_sub, PEER_ETILE, TILE), F32), pltpu.VMEM((n_sub, PEER_ETILE, TILE), F32),
                        pltpu.VMEM((n_sub, PEER_ETILE, TILE), BF16), pltpu.VMEM((n_sub, PEER_ETILE, TILE), BF16)],
        compiler_params=_params("parallel", "arbitrary"),
        name="peer_dense",
    )(h2, u_tab, vt_tab, r2, p2, cnt, p1)


def _final_kernel(ot_ref, hs_ref, g_ref, y_ref):
    hs = hs_ref[...] + ot_ref[0].T
    ms = jnp.mean(hs * hs, axis=-1, keepdims=True)
    y_ref[0] = hs * lax.rsqrt(ms + EPS) * g_ref[...]


def _final(o_t, hs, final_g, first_seq, n_seq, nt):
    tile_of = lambda b, i: (b + first_seq) * nt + i + 1
    return pl.pallas_call(
        _final_kernel,
        grid=(n_seq, nt - 1),
        in_specs=[
            pl.BlockSpec((1, D_MODEL, TILE), lambda b, i: (tile_of(b, i), 0, 0)),
            pl.BlockSpec((TILE, D_MODEL), lambda b, i: (tile_of(b, i), 0)),
            pl.BlockSpec((1, D_MODEL), lambda b, i: (0, 0)),
        ],
        out_specs=pl.BlockSpec((1, TILE, D_MODEL), lambda b, i: (b, i, 0)),
        out_shape=jax.ShapeDtypeStruct((n_seq, (nt - 1) * TILE, D_MODEL), F32),
        compiler_params=_params("parallel", "parallel"),
        name="final",
    )(o_t, hs, final_g)


def kernel(x_prompt, x_sample, meta, norm1_g, w_in, s5_lam_re, s5_lam_im, s5_log_step, s5_b_re, s5_b_im,
           s5_c_re, s5_c_im, s5_d, w_glu, hg_lb, hg_norm_g, w_hg_out, w_out, norm2_g, peer_wq, peer_keys,
           peer_u, peer_v, final_g):
    assert x_prompt.shape[1] == x_sample.shape[1] and x_prompt.shape[1] % TILE == 0
    assert norm1_g.shape[0] == 1, "single-layer trunk"
    n_prompt = x_prompt.shape[0]
    x_prompt, x_sample = x_prompt.astype(F32), x_sample.astype(F32)
    metap = jnp.concatenate([jnp.zeros((FRONT, D_MODEL), F32), meta.astype(F32)], axis=0)
    row = lambda a: a.astype(F32).reshape(1, -1)

    u, q_pre, f_fwd, f_bwd, i_in, gates = _inproj(x_prompt, x_sample, metap, row(norm1_g[0]),
                                                  w_in[0].astype(BF16))

    y_s5 = _s5_mixer(u, *_s5_weights(s5_lam_re[0], s5_lam_im[0], s5_log_step[0], s5_b_re[0], s5_b_im[0],
                                     s5_c_re[0], s5_c_im[0], s5_d[0]))

    lb = jax.nn.softmax(hg_lb.astype(F32), axis=0)[0]
    o_f, o_b = _hgrn2(q_pre, f_fwd, f_bwd, i_in, row(lb))

    hs, h2 = _merge(x_prompt, x_sample, metap, y_s5, o_f, o_b, gates, w_glu[0].astype(BF16), row(hg_norm_g[0]),
                    w_hg_out[0].astype(BF16), w_out[0].astype(BF16), row(norm2_g[0]))

    bsz, p, _ = hs.shape
    hs = hs.reshape(bsz * p, D_MODEL)
    h2 = h2.reshape(bsz * p // 2, D_MODEL)
    r2, p2, cnt, p1 = _peer_scores(h2, peer_wq[0].astype(BF16), peer_keys[0].astype(BF16))
    u_tab, vt_tab = _pack_tables(peer_u[0], peer_v[0])
    o_t = _peer_dense(h2, u_tab, vt_tab, r2, p2, cnt, p1)
    fin = functools.partial(_final, o_t, hs, row(final_g), nt=p // TILE)
    return (fin(first_seq=0, n_seq=n_prompt), fin(first_seq=n_prompt, n_seq=bsz - n_prompt))
```

```python
import functools
import math

import jax
import jax.numpy as jnp
from jax import lax
from jax.experimental import pallas as pl
from jax.experimental.pallas import tpu as pltpu

F32 = jnp.float32
BF16 = jnp.bfloat16

D_MODEL = 1024
N_META = 16
EPS = 1e-6
TILE = 256
FRONT = TILE - N_META

S5_WIDTH = 512
S5_GROUP = 16
S5_GROUPS = 32
S5_STATE = 64
S5_CHUNK = 16
S5_LB = 128 // S5_GROUP
S5_NLB = S5_GROUPS // S5_LB
S5_K = S5_CHUNK * 128

HG_WIDTH = 512
HG_HEADS = 4
HG_DIM = 128
HG_CHUNK = 64

PEER_HEADS = 8
PEER_KEYS = 128
PEER_TOPK = 16
PEER_QDIM = 256
PEER_HALF = 128
PEER_ETILE = 1024
PEER_TTILE = 1280
PEER_NCAND = 56

VMEM_LIMIT = 56 * 1024 * 1024

_NT = (((1,), (1,)), ((), ()))
_TN = (((0,), (0,)), ((), ()))


def _largest_divisor(n, target, multiple=1):
    best = None
    for d in range(multiple, min(n, target) + 1, multiple):
        if n % d == 0:
            best = d
    assert best is not None, (n, target, multiple)
    return best


def _pack(x):
    return pltpu.bitcast(x.astype(BF16), jnp.uint32)


def _unpack(x):
    return pltpu.bitcast(x, BF16)


def _params(*sem):
    return pltpu.CompilerParams(dimension_semantics=sem, vmem_limit_bytes=VMEM_LIMIT)


def _hidden_tile(xp_ref, xs_ref, metap_ref, n_prompt):
    b, i = pl.program_id(0), pl.program_id(1)
    x = jnp.where(b < n_prompt, xp_ref[0], xs_ref[0])
    return jnp.where(i == 0, metap_ref[...], x)


def _x_specs(n_prompt, nt):
    tile = lambda i: jnp.maximum(i - 1, 0)
    prompt = pl.BlockSpec((1, TILE, D_MODEL), lambda b, i: (
        jnp.minimum(b, n_prompt - 1), jnp.where(b < n_prompt, tile(i), nt - 2), 0))
    sample = pl.BlockSpec((1, TILE, D_MODEL), lambda b, i: (
        jnp.maximum(b - n_prompt, 0), jnp.where(b < n_prompt, 0, tile(i)), 0))
    return [prompt, sample]


def _inproj_kernel(xp_ref, xs_ref, metap_ref, g_ref, w_ref, u_ref, q_ref, ff_ref, fb_ref, v_ref, gate_ref, *,
                   n_prompt):
    i = pl.program_id(1)
    hs = _hidden_tile(xp_ref, xs_ref, metap_ref, n_prompt)
    ms = jnp.mean(hs * hs, axis=-1, keepdims=True)
    h = (hs * lax.rsqrt(ms + EPS) * g_ref[...]).astype(BF16)
    row = lax.broadcasted_iota(jnp.int32, (TILE, 1), 0)
    valid = jnp.logical_or(i > 0, row >= FRONT)
    u = jnp.dot(h, w_ref[:, 0:S5_WIDTH], preferred_element_type=F32)
    u_ref[0] = jnp.where(valid, u, 0.0)
    off = S5_WIDTH
    for ref in (q_ref, ff_ref, fb_ref, v_ref):
        ref[0] = jnp.dot(h, w_ref[:, off:off + HG_WIDTH], preferred_element_type=F32)
        off += HG_WIDTH
    gate_ref[0] = jnp.dot(h, w_ref[:, off:], preferred_element_type=F32)


def _inproj(x_prompt, x_sample, metap, g, w_in):
    n_prompt, length, _ = x_prompt.shape
    bsz = n_prompt + x_sample.shape[0]
    nt = length // TILE + 1
    p = nt * TILE
    ncols = w_in.shape[1]
    ngate = ncols - S5_WIDTH - 4 * HG_WIDTH
    tok = lambda width: pl.BlockSpec((1, TILE, width), lambda b, i: (b, i, 0))
    full = lambda shape: pl.BlockSpec(shape, lambda b, i: (0,) * len(shape))
    hg_shape = jax.ShapeDtypeStruct((bsz, p, HG_WIDTH), F32)
    return pl.pallas_call(
        functools.partial(_inproj_kernel, n_prompt=n_prompt),
        grid=(bsz, nt),
        in_specs=_x_specs(n_prompt, nt) + [full((TILE, D_MODEL)), full((1, D_MODEL)), full((D_MODEL, ncols))],
        out_specs=[tok(S5_WIDTH), tok(HG_WIDTH), tok(HG_WIDTH), tok(HG_WIDTH), tok(HG_WIDTH), tok(ngate)],
        out_shape=[hg_shape, hg_shape, hg_shape, hg_shape, hg_shape, jax.ShapeDtypeStruct((bsz, p, ngate), F32)],
        compiler_params=_params("arbitrary", "arbitrary"),
        name="inproj",
    )(x_prompt, x_sample, metap, g, w_in)


def _s5_weights(lam_re, lam_im, log_step, b_re, b_im, c_re, c_im, d_skip):
    hi = lax.Precision.HIGHEST
    f = lambda a: a.astype(F32)
    lam_re, lam_im, log_step = f(lam_re), f(lam_im), f(log_step)
    b_re, b_im, c_re, c_im, d_skip = f(b_re), f(b_im), f(c_re), f(c_im), f(d_skip)
    step = jnp.exp(log_step)[:, :, None]
    pw = jnp.arange(S5_CHUNK + 1, dtype=F32)[:, None, None, None]
    mag = jnp.exp(pw * lam_re[None] * step[None])
    ang = pw * lam_im[None] * step[None]
    a_re, a_im = mag * jnp.cos(ang), mag * jnp.sin(ang)
    den = lam_re * lam_re + lam_im * lam_im
    n_re, n_im = a_re[1] - 1.0, a_im[1]
    coef_re = (n_re * lam_re + n_im * lam_im) / den
    coef_im = (n_im * lam_re - n_re * lam_im) / den
    w_re = coef_re[None] * a_re - coef_im[None] * a_im
    w_im = coef_re[None] * a_im + coef_im[None] * a_re

    def conv_taps(direction):
        wr, wi = w_re[:S5_CHUNK, direction], w_im[:S5_CHUNK, direction]
        m_re = wr[..., None] * b_re[None] - wi[..., None] * b_im[None]
        m_im = wr[..., None] * b_im[None] + wi[..., None] * b_re[None]
        return (jnp.einsum('gon,dgni->dgoi', c_re, m_re, precision=hi)
                - jnp.einsum('gon,dgni->dgoi', c_im, m_im, precision=hi))

    k_f, k_b = conv_taps(0), conv_taps(1)
    center = k_f[0] + k_b[0] + jnp.eye(S5_GROUP, dtype=F32)[None] * d_skip.reshape(S5_GROUPS, S5_GROUP, 1)
    taps = jnp.concatenate([k_b[:0:-1], center[None], k_f[1:]], axis=0)
    t_idx = jnp.arange(S5_CHUNK)
    toe = taps[t_idx[None, :] - t_idx[:, None] + S5_CHUNK - 1]
    split = lambda a, axis: a.reshape(a.shape[:axis] + (S5_NLB, S5_LB) + a.shape[axis + 1:])

    def block_diag(src):
        src = src.astype(BF16)
        width = src.shape[-1]
        lane_pad = lambda g: [(0, 0, 0)] * 4 + [(g * width, (S5_LB - 1 - g) * width, 0)]
        parts = [lax.pad(src[:, g], jnp.zeros((), BF16), lane_pad(g)) for g in range(S5_LB)]
        return jnp.stack(parts, axis=2).reshape(S5_NLB, S5_K, S5_K)

    toe = block_diag(split(toe.transpose(2, 0, 4, 1, 3), 0))

    def end_map(direction, powers):
        pr, pi = a_re[powers, direction], a_im[powers, direction]
        return [pr[..., None] * b_re[None] - pi[..., None] * b_im[None],
                pr[..., None] * b_im[None] + pi[..., None] * b_re[None]]

    ends = jnp.stack(end_map(0, S5_CHUNK - 1 - t_idx) + end_map(1, t_idx))
    ends = block_diag(split(ends.transpose(2, 1, 4, 0, 3), 0))

    def out_map(direction, powers):
        wr, wi = w_re[powers, direction], w_im[powers, direction]
        return [c_re[None] * wr[:, :, None, :] - c_im[None] * wi[:, :, None, :],
                -c_re[None] * wi[:, :, None, :] - c_im[None] * wr[:, :, None, :]]

    outs = jnp.stack(out_map(0, t_idx + 1) + out_map(1, S5_CHUNK - t_idx))
    outs = block_diag(split(outs.transpose(2, 0, 4, 1, 3), 0))

    def decay(direction):
        ar = split(a_re[S5_CHUNK, direction], 0).reshape(S5_NLB, 1, S5_K // 4)
        ai = split(a_im[S5_CHUNK, direction], 0).reshape(S5_NLB, 1, S5_K // 4)
        same = jnp.concatenate([ar, ar], axis=2).reshape(1, S5_NLB * S5_K // 2)
        cross = jnp.concatenate([-ai, ai], axis=2).reshape(1, S5_NLB * S5_K // 2)
        return same, cross

    return toe.astype(BF16), ends.astype(BF16), outs.astype(BF16), decay(0) + decay(1)


def _chunk_rows(u_ref, nc):
    parts = [u_ref[0, pl.ds(s, nc, stride=S5_CHUNK), :] for s in range(S5_CHUNK)]
    return jnp.concatenate(parts, axis=1).astype(BF16)


def _s5_local_kernel(u_ref, w_ref, ef_ref, eb_ref, *, nc):
    e = jnp.dot(_chunk_rows(u_ref, nc), w_ref[0], preferred_element_type=F32)
    ef_ref[0] = e[:, :S5_K // 2]
    eb_ref[0] = e[:, S5_K // 2:]


def _s5_local(u, ends, nc):
    bsz, p, _ = u.shape
    nch = p // S5_CHUNK
    out = lambda: pl.BlockSpec((1, nc, S5_K // 2), lambda lb, b, i: (b, i, lb))
    shape = jax.ShapeDtypeStruct((bsz, nch, S5_NLB * S5_K // 2), F32)
    return pl.pallas_call(
        functools.partial(_s5_local_kernel, nc=nc),
        grid=(S5_NLB, bsz, nch // nc),
        in_specs=[pl.BlockSpec((1, nc * S5_CHUNK, 128), lambda lb, b, i: (b, i, lb)),
                  pl.BlockSpec((1, S5_K, S5_K), lambda lb, b, i: (lb, 0, 0))],
        out_specs=[out(), out()],
        out_shape=[shape, shape],
        compiler_params=_params("arbitrary", "arbitrary", "arbitrary"),
        name="s5_local",
    )(u, ends)


def _swap_re_im(x):
    half = S5_K // 4
    parts = []
    for lb in range(x.shape[1] // (2 * half)):
        parts += [x[:, (2 * lb + 1) * half:(2 * lb + 2) * half], x[:, 2 * lb * half:(2 * lb + 1) * half]]
    return jnp.concatenate(parts, axis=1)


def _s5_scan_kernel(fs_ref, fc_ref, bs_ref, bc_ref, ef_ref, eb_ref, xf_ref, xb_ref, st_ref, *, ct):
    j = pl.program_id(1)

    @pl.when(j == 0)
    def _():
        st_ref[...] = jnp.zeros_like(st_ref)

    width = S5_K
    for part in range(st_ref.shape[2] // width):
        lanes = slice(part * width, (part + 1) * width)
        f_same, f_cross = fs_ref[:, lanes], fc_ref[:, lanes]
        b_same, b_cross = bs_ref[:, lanes], bc_ref[:, lanes]

        def body(c, carry):
            sf, sb = carry
            xf_ref[0, pl.ds(c, 1), lanes] = sf
            sf = f_same * sf + f_cross * _swap_re_im(sf) + ef_ref[0, pl.ds(c, 1), lanes]
            cb = ct - 1 - c
            xb_ref[0, pl.ds(cb, 1), lanes] = sb
            sb = b_same * sb + b_cross * _swap_re_im(sb) + eb_ref[0, pl.ds(cb, 1), lanes]
            return sf, sb

        sf, sb = lax.fori_loop(0, ct, body, (st_ref[0, :, lanes], st_ref[1, :, lanes]))
        st_ref[0, :, lanes] = sf
        st_ref[1, :, lanes] = sb


def _s5_scan(ef, eb, dec):
    bsz, nch, width = ef.shape
    ct = _largest_divisor(nch, 208, 8)
    nj = nch // ct
    fwd = lambda: pl.BlockSpec((1, ct, width), lambda b, j: (b, j, 0))
    bwd = lambda: pl.BlockSpec((1, ct, width), lambda b, j: (b, nj - 1 - j, 0))
    row = lambda: pl.BlockSpec((1, width), lambda b, j: (0, 0))
    shape = jax.ShapeDtypeStruct(ef.shape, F32)
    return pl.pallas_call(
        functools.partial(_s5_scan_kernel, ct=ct),
        grid=(bsz, nj),
        in_specs=[row(), row(), row(), row(), fwd(), bwd()],
        out_specs=[fwd(), bwd()],
        out_shape=[shape, shape],
        scratch_shapes=[pltpu.VMEM((2, 1, width), F32)],
        compiler_params=_params("parallel", "arbitrary"),
        name="s5_scan",
    )(*dec, ef, eb)


def _s5_out_kernel(u_ref, xf_ref, xb_ref, toe_ref, outs_ref, y_ref, *, nc):
    states = jnp.concatenate([xf_ref[0], xb_ref[0]], axis=1).astype(BF16)
    y = jnp.dot(_chunk_rows(u_ref, nc), toe_ref[0], preferred_element_type=F32)
    y += jnp.dot(states, outs_ref[0], preferred_element_type=F32)
    for t in range(S5_CHUNK):
        y_ref[0, pl.ds(t, nc, stride=S5_CHUNK), :] = y[:, t * 128:(t + 1) * 128]


def _s5_out(u, xf, xb, toe, outs, nc):
    bsz, p, _ = u.shape
    nch = p // S5_CHUNK
    tok = lambda: pl.BlockSpec((1, nc * S5_CHUNK, 128), lambda lb, b, i: (b, i, lb))
    state = lambda: pl.BlockSpec((1, nc, S5_K // 2), lambda lb, b, i: (b, i, lb))
    weight = lambda: pl.BlockSpec((1, S5_K, S5_K), lambda lb, b, i: (lb, 0, 0))
    return pl.pallas_call(
        functools.partial(_s5_out_kernel, nc=nc),
        grid=(S5_NLB, bsz, nch // nc),
        in_specs=[tok(), state(), state(), weight(), weight()],
        out_specs=tok(),
        out_shape=jax.ShapeDtypeStruct(u.shape, F32),
        compiler_params=_params("arbitrary", "arbitrary", "arbitrary"),
        name="s5_out",
    )(u, xf, xb, toe, outs)


def _s5_mixer(u, toe, ends, outs, dec):
    nch = u.shape[1] // S5_CHUNK
    nc = _largest_divisor(nch, 260, 8)
    ef, eb = _s5_local(u, ends, nc)
    xf, xb = _s5_scan(ef, eb, dec)
    return _s5_out(u, xf, xb, toe, outs, nc)


def _hg_kernel(lb_ref, qf_ref, ff_ref, vf_ref, qb_ref, fb_ref, vb_ref, of_ref, ob_ref, st_ref):
    i = pl.program_id(1)
    n = pl.num_programs(1)

    @pl.when(i == 0)
    def _():
        st_ref[...] = jnp.zeros_like(st_ref)

    lb = lb_ref[...]
    n_chunks = TILE // HG_CHUNK
    shift = HG_CHUNK.bit_length() - 1
    r = lax.broadcasted_iota(jnp.int32, (TILE, TILE), 0)
    c = lax.broadcasted_iota(jnp.int32, (TILE, TILE), 1)
    same_chunk = jnp.right_shift(r, shift) == jnp.right_shift(c, shift)
    r64 = lax.broadcasted_iota(jnp.int32, (HG_CHUNK, HG_CHUNK), 0)
    c64 = lax.broadcasted_iota(jnp.int32, (HG_CHUNK, HG_CHUNK), 1)
    row = lax.broadcasted_iota(jnp.int32, (TILE, 1), 0)
    passes = (
        (0, qf_ref, ff_ref, vf_ref, of_ref, i, c <= r, c64 <= r64, range(n_chunks), HG_CHUNK - 1),
        (1, qb_ref, fb_ref, vb_ref, ob_ref, n - 1 - i, c >= r, c64 >= r64, reversed(range(n_chunks)), 0),
    )
    for d, q_ref, f_ref, v_ref, o_ref, tile, before, keep, chunks, tot_row in passes:
        q_pre = q_ref[0]
        q = q_pre * jax.nn.sigmoid(q_pre)
        g = lb + (1.0 - lb) * jax.nn.sigmoid(f_ref[0])
        valid = tile * TILE + row >= FRONT
        log_f = jnp.where(valid, jnp.log(g), 0.0)
        k = jnp.where(valid, 1.0 - g, 0.0)
        cum = jnp.logical_and(same_chunk, before).astype(F32)
        b = jnp.dot(cum, log_f, preferred_element_type=F32, precision=lax.Precision.HIGHEST)
        totals = [b[ch * HG_CHUNK + tot_row:ch * HG_CHUNK + tot_row + 1] for ch in range(n_chunks)]
        b_tot = jnp.concatenate([jnp.broadcast_to(t, (HG_CHUNK, HG_WIDTH)) for t in totals], axis=0)
        q_dec = (q * jnp.exp(b)).astype(BF16)
        k_inv = (k * jnp.exp(-b)).astype(BF16)
        k_end = (k * jnp.exp(b_tot - b)).astype(BF16)
        v = v_ref[0].astype(BF16)
        states = [st_ref[d, h] for h in range(HG_HEADS)]
        for ch in chunks:
            rows = slice(ch * HG_CHUNK, (ch + 1) * HG_CHUNK)
            decay = jnp.exp(totals[ch])
            for h in range(HG_HEADS):
                ls = slice(h * HG_DIM, (h + 1) * HG_DIM)
                scores = lax.dot_general(q_dec[rows, ls], k_inv[rows, ls], _NT, preferred_element_type=F32)
                scores = jnp.where(keep, scores, 0.0).astype(BF16)
                o = jnp.dot(scores, v[rows, ls], preferred_element_type=F32)
                o += lax.dot_general(q_dec[rows, ls], states[h].astype(BF16), _NT, preferred_element_type=F32)
                kv = lax.dot_general(v[rows, ls], k_end[rows, ls], _TN, preferred_element_type=F32)
                states[h] = decay[:, ls] * states[h] + kv
                o_ref[0, rows, ls] = o
        for h in range(HG_HEADS):
            st_ref[d, h] = states[h]


def _hgrn2(q_pre, f_fwd, f_bwd, i_in, lb):
    bsz, p, _ = q_pre.shape
    nt = p // TILE
    fwd = lambda: pl.BlockSpec((1, TILE, HG_WIDTH), lambda b, i: (b, i, 0))
    bwd = lambda: pl.BlockSpec((1, TILE, HG_WIDTH), lambda b, i: (b, nt - 1 - i, 0))
    shape = jax.ShapeDtypeStruct((bsz, p, HG_WIDTH), F32)
    return pl.pallas_call(
        _hg_kernel,
        grid=(bsz, nt),
        in_specs=[pl.BlockSpec((1, HG_WIDTH), lambda b, i: (0, 0)), fwd(), fwd(), fwd(), bwd(), bwd(), bwd()],
        out_specs=[fwd(), bwd()],
        out_shape=[shape, shape],
        scratch_shapes=[pltpu.VMEM((2, HG_HEADS, HG_DIM, HG_DIM), F32)],
        compiler_params=_params("parallel", "arbitrary"),
        name="hgrn2",
    )(lb, q_pre, f_fwd, i_in, q_pre, f_bwd, i_in)


def _gelu(x):
    return 0.5 * x * (1.0 + lax.erf(x * (1.0 / math.sqrt(2.0))))


def _merge_kernel(xp_ref, xs_ref, metap_ref, ys_ref, of_ref, ob_ref, gate_ref, wglu_ref, hgn_ref, whg_ref,
                  wout_ref, n2_ref, hs_ref, h2_ref, *, n_prompt):
    hs = _hidden_tile(xp_ref, xs_ref, metap_ref, n_prompt)
    glu = jnp.dot(_gelu(ys_ref[0]).astype(BF16), wglu_ref[...], preferred_element_type=F32)
    y_a = glu[:, :D_MODEL] * jax.nn.sigmoid(glu[:, D_MODEL:])
    o = of_ref[0] + ob_ref[0]
    normed = []
    for h in range(HG_HEADS):
        oh = o[:, h * HG_DIM:(h + 1) * HG_DIM]
        normed.append(oh * lax.rsqrt(jnp.mean(oh * oh, axis=-1, keepdims=True) + EPS))
    o_gate = gate_ref[0, :, 0:HG_WIDTH]
    y_hg = jnp.concatenate(normed, axis=-1) * hgn_ref[...] * (o_gate * jax.nn.sigmoid(o_gate))
    y_b = jnp.dot(y_hg.astype(BF16), whg_ref[...], preferred_element_type=F32)
    gate_a = gate_ref[0, :, HG_WIDTH:HG_WIDTH + D_MODEL]
    gate_b = gate_ref[0, :, HG_WIDTH + D_MODEL:]
    mixed = jax.nn.sigmoid(gate_a) * y_a + jax.nn.sigmoid(gate_b) * y_b
    hs = hs + jnp.dot(mixed.astype(BF16), wout_ref[...], preferred_element_type=F32)
    hs_ref[0] = hs
    ms = jnp.mean(hs * hs, axis=-1, keepdims=True)
    h2_ref[0] = _pack(hs * lax.rsqrt(ms + EPS) * n2_ref[...])


def _merge(x_prompt, x_sample, metap, y_s5, o_f, o_b, gates, w_glu, hg_norm_g, w_hg_out, w_out, norm2_g):
    bsz, p, _ = y_s5.shape
    nt = p // TILE
    n_prompt = x_prompt.shape[0]
    tok = lambda width: pl.BlockSpec((1, TILE, width), lambda b, i: (b, i, 0))
    full = lambda a: pl.BlockSpec(a.shape, lambda b, i: (0,) * a.ndim)
    return pl.pallas_call(
        functools.partial(_merge_kernel, n_prompt=n_prompt),
        grid=(bsz, nt),
        in_specs=_x_specs(n_prompt, nt) + [
            full(metap), tok(S5_WIDTH), tok(HG_WIDTH), tok(HG_WIDTH), tok(gates.shape[-1]),
            full(w_glu), full(hg_norm_g), full(w_hg_out), full(w_out), full(norm2_g),
        ],
        out_specs=[tok(D_MODEL), pl.BlockSpec((1, TILE // 2, D_MODEL), lambda b, i: (b, i, 0))],
        out_shape=[jax.ShapeDtypeStruct((bsz, p, D_MODEL), F32),
                   jax.ShapeDtypeStruct((bsz, p // 2, D_MODEL), jnp.uint32)],
        compiler_params=_params("arbitrary", "arbitrary"),
        name="merge",
    )(x_prompt, x_sample, metap, y_s5, o_f, o_b, gates, w_glu, hg_norm_g, w_hg_out, w_out, norm2_g)


def _staircase():
    return [(i, PEER_TOPK // (i + 1)) for i in range(PEER_TOPK)]


def _top16(s):
    work = s
    rank = jnp.full(s.shape, float(PEER_TOPK), F32)
    tops = []
    for it in range(PEER_TOPK):
        m = jnp.max(work, axis=0, keepdims=True)
        hit = work == m
        rank = jnp.where(hit, float(it), rank)
        work = jnp.where(hit, -jnp.inf, work)
        tops.append(m)
    return tops, rank


def _score_kernel(h2_ref, wq_ref, keys_ref, r2_ref, p2_ref, cnt_ref, p1_ref, cand_ref):
    h2 = _unpack(h2_ref[...])
    for h in range(PEER_HEADS):
        qh = jnp.dot(h2, wq_ref[:, h * PEER_QDIM:(h + 1) * PEER_QDIM], preferred_element_type=F32).astype(BF16)
        s1_all = lax.dot_general(keys_ref[h, 0], qh[:, :PEER_HALF], _NT, preferred_element_type=F32)
        s2_all = lax.dot_general(keys_ref[h, 1], qh[:, PEER_HALF:], _NT, preferred_element_type=F32)
        for part in range(TILE // 128):
            lanes = slice(part * 128, (part + 1) * 128)
            s1, s2 = s1_all[:, lanes], s2_all[:, lanes]
            t1, rank1 = _top16(s1)
            t2, rank2 = _top16(s2)
            t2_all = jnp.concatenate(t2, axis=0)
            cand_ref[:, lanes] = jnp.full((PEER_NCAND, 128), -jnp.inf, F32)
            off = 0
            for i, n_i in _staircase():
                cand_ref[off:off + n_i, lanes] = t1[i] + t2_all[0:n_i]
                off += n_i
            cand = cand_ref[:, lanes]
            c_max = t1[0] + t2[0]
            work = cand
            z = jnp.zeros_like(c_max)
            tau = c_max
            for it in range(PEER_TOPK):
                tau = jnp.max(work, axis=0, keepdims=True)
                z = z + jnp.exp(tau - c_max)
                work = jnp.where(work == tau, -jnp.inf, work)
            chosen = (cand >= tau).astype(F32)
            cnt = jnp.zeros_like(s1)
            off = 0
            for i, n_i in _staircase():
                cnt_i = jnp.sum(chosen[off:off + n_i], axis=0, keepdims=True)
                cnt = jnp.where(rank1 == float(i), cnt_i, cnt)
                off += n_i
            r2_ref[h, 0, :, lanes] = _pack(rank2)
            p2_ref[h, 0, :, lanes] = _pack(jnp.exp(s2 - t2[0]))
            cnt_ref[h, 0, :, lanes] = cnt
            p1_ref[h, 0, :, lanes] = 0.5 * jnp.exp(s1 - t1[0]) / z


def _peer_scores(h2, w_q, keys):
    ntok = 2 * h2.shape[0]
    blk = lambda rows: pl.BlockSpec((PEER_HEADS, 1, rows, TILE), lambda i: (0, i, 0, 0))
    packed = jax.ShapeDtypeStruct((PEER_HEADS, ntok // TILE, PEER_KEYS // 2, TILE), jnp.uint32)
    plain = jax.ShapeDtypeStruct((PEER_HEADS, ntok // TILE, PEER_KEYS, TILE), F32)
    return pl.pallas_call(
        _score_kernel,
        grid=(ntok // TILE,),
        in_specs=[
            pl.BlockSpec((TILE // 2, D_MODEL), lambda i: (i, 0)),
            pl.BlockSpec(w_q.shape, lambda i: (0, 0)),
            pl.BlockSpec(keys.shape, lambda i: (0, 0, 0, 0)),
        ],
        out_specs=[blk(PEER_KEYS // 2), blk(PEER_KEYS // 2), blk(PEER_KEYS), blk(PEER_KEYS)],
        out_shape=[packed, packed, plain, plain],
        scratch_shapes=[pltpu.VMEM((PEER_NCAND, TILE), F32)],
        compiler_params=_params("parallel"),
        name="peer_scores",
    )(h2, w_q, keys)


def _pack_tables_kernel(u_ref, v_ref, up_ref, vtp_ref):
    up_ref[...] = _pack(u_ref[...])
    vtp_ref[...] = _pack(v_ref[...].T)


def _pack_tables(u_tab, v_tab):
    nexp, d = u_tab.shape
    te = _largest_divisor(nexp, 512, 128)
    tile = lambda: pl.BlockSpec((te, d), lambda e: (e, 0))
    return pl.pallas_call(
        _pack_tables_kernel,
        grid=(nexp // te,),
        in_specs=[tile(), tile()],
        out_specs=[pl.BlockSpec((te // 2, d), lambda e: (e, 0)), pl.BlockSpec((d // 2, te), lambda e: (0, e))],
        out_shape=[jax.ShapeDtypeStruct((nexp // 2, d), jnp.uint32), jax.ShapeDtypeStruct((d // 2, nexp), jnp.uint32)],
        compiler_params=_params("parallel"),
        name="pack_tables",
    )(u_tab.astype(F32), v_tab.astype(F32))


def _dense_kernel(h2_ref, u_ref, vt_ref, r2_ref, p2_ref, cnt_ref, p1_ref, o_ref, act0_ref, act1_ref, wa0_ref,
                  wa1_ref, *, n_sub, n_etiles):
    s = pl.program_id(1)

    @pl.when(s == 0)
    def _():
        o_ref[...] = jnp.zeros_like(o_ref)

    n_first = PEER_ETILE // PEER_KEYS
    zero = jnp.zeros((), BF16)
    spread = lambda row: jnp.broadcast_to(row, (PEER_KEYS, 128)).astype(BF16)

    def sub_tile(c, carry, *, stages, act_a, act_b, wa_b, wa_c):
        if "A" in stages:
            h2 = _unpack(h2_ref[pl.ds(pl.multiple_of(c * (TILE // 2), TILE // 2), TILE // 2), :])
            act_a[c] = lax.dot_general(_unpack(u_ref[...]), h2, _NT, preferred_element_type=F32)
        for lc in range(TILE // 128 if "B" in stages else 0):
            lanes = slice(lc * 128, (lc + 1) * 128)
            for j in range(0, n_first, 2):
                w0 = jnp.zeros((PEER_KEYS, 128), BF16)
                w1 = jnp.zeros((PEER_KEYS, 128), BF16)
                for h in range(PEER_HEADS):
                    r2 = _unpack(r2_ref[h, c, :, lanes])
                    p2 = _unpack(p2_ref[h, c, :, lanes])
                    cnt = cnt_ref[h, c, :, lanes]
                    p1 = p1_ref[h, c, :, lanes]
                    w0 += jnp.where(r2 < spread(cnt[j:j + 1]), p2, zero) * spread(p1[j:j + 1])
                    w1 += jnp.where(r2 < spread(cnt[j + 1:j + 2]), p2, zero) * spread(p1[j + 1:j + 2])
                for k, w in ((j, w0), (j + 1, w1)):
                    rows = slice(k * PEER_KEYS, (k + 1) * PEER_KEYS)
                    act = act_b[c, rows, lanes]
                    twice_gelu = act * (1.0 + lax.erf(act * (1.0 / math.sqrt(2.0))))
                    wa_b[c, rows, lanes] = w * twice_gelu.astype(BF16)
        if "C" in stages:
            o_ref[c] += jnp.dot(_unpack(vt_ref[...]), wa_c[c], preferred_element_type=F32)
        return carry

    def run(condition, stages, parity):
        bufs = (dict(act_a=act0_ref, act_b=act1_ref, wa_b=wa1_ref, wa_c=wa0_ref) if parity == 0 else
                dict(act_a=act1_ref, act_b=act0_ref, wa_b=wa0_ref, wa_c=wa1_ref))

        @pl.when(condition)
        def _():
            lax.fori_loop(0, n_sub, functools.partial(sub_tile, stages=stages, **bufs), 0)

    steady = jnp.logical_and(s >= 2, s < n_etiles)
    run(s == 0, "A", 0)
    run(s == 1, "AB", 1)
    run(jnp.logical_and(steady, lax.rem(s, 2) == 0), "ABC", 0)
    run(jnp.logical_and(steady, lax.rem(s, 2) == 1), "ABC", 1)
    run(s == n_etiles, "BC", n_etiles % 2)
    run(s == n_etiles + 1, "C", (n_etiles + 1) % 2)


def _peer_dense(h2, u_tab, vt_tab, r2, p2, cnt, p1):
    ntok = 2 * h2.shape[0]
    n_sub = _largest_divisor(ntok // TILE, PEER_TTILE // TILE)
    tn = n_sub * TILE
    n_etiles = 2 * u_tab.shape[0] // PEER_ETILE
    last = n_etiles - 1
    stage = lambda s, lag: jnp.clip(s - lag, 0, last)
    second = lambda: pl.BlockSpec((PEER_HEADS, n_sub, PEER_KEYS // 2, TILE), lambda t, s: (0, t, 0, 0))
    first = lambda: pl.BlockSpec((PEER_HEADS, n_sub, PEER_ETILE // PEER_KEYS, TILE),
                                 lambda t, s: (0, t, stage(s, 1), 0))
    return pl.pallas_call(
        functools.partial(_dense_kernel, n_sub=n_sub, n_etiles=n_etiles),
        grid=(ntok // tn, n_etiles + 2),
        in_specs=[pl.BlockSpec((tn // 2, D_MODEL), lambda t, s: (t, 0)),
                  pl.BlockSpec((PEER_ETILE // 2, D_MODEL), lambda t, s: (stage(s, 0), 0)),
                  pl.BlockSpec((D_MODEL // 2, PEER_ETILE), lambda t, s: (0, stage(s, 2))),
                  second(), second(), first(), first()],
        out_specs=pl.BlockSpec((n_sub, D_MODEL, TILE), lambda t, s: (t, 0, 0)),
        out_shape=jax.ShapeDtypeStruct((ntok // TILE, D_MODEL, TILE), F32),
        scratch_shapes=[pltpu.VMEM((n_sub, PEER_ETILE, TILE), F32), pltpu.VMEM((n_sub, PEER_ETILE, TILE), F32),
                        pltpu.VMEM((n_sub, PEER_ETILE, TILE), BF16), pltpu.VMEM((n_sub, PEER_ETILE, TILE), BF16)],
        compiler_params=_params("parallel", "arbitrary"),
        name="peer_dense",
    )(h2, u_tab, vt_tab, r2, p2, cnt, p1)


def _final_kernel(ot_ref, hs_ref, g_ref, y_ref):
    hs = hs_ref[...] + ot_ref[0].T
    ms = jnp.mean(hs * hs, axis=-1, keepdims=True)
    y_ref[0] = hs * lax.rsqrt(ms + EPS) * g_ref[...]


def _final(o_t, hs, final_g, first_seq, n_seq, nt):
    tile_of = lambda b, i: (b + first_seq) * nt + i + 1
    return pl.pallas_call(
        _final_kernel,
        grid=(n_seq, nt - 1),
        in_specs=[
            pl.BlockSpec((1, D_MODEL, TILE), lambda b, i: (tile_of(b, i), 0, 0)),
            pl.BlockSpec((TILE, D_MODEL), lambda b, i: (tile_of(b, i), 0)),
            pl.BlockSpec((1, D_MODEL), lambda b, i: (0, 0)),
        ],
        out_specs=pl.BlockSpec((1, TILE, D_MODEL), lambda b, i: (b, i, 0)),
        out_shape=jax.ShapeDtypeStruct((n_seq, (nt - 1) * TILE, D_MODEL), F32),
        compiler_params=_params("parallel", "parallel"),
        name="final",
    )(o_t, hs, final_g)


def kernel(x_prompt, x_sample, meta, norm1_g, w_in, s5_lam_re, s5_lam_im, s5_log_step, s5_b_re, s5_b_im,
           s5_c_re, s5_c_im, s5_d, w_glu, hg_lb, hg_norm_g, w_hg_out, w_out, norm2_g, peer_wq, peer_keys,
           peer_u, peer_v, final_g):
    assert x_prompt.shape[1] == x_sample.shape[1] and x_prompt.shape[1] % TILE == 0
    assert norm1_g.shape[0] == 1, "single-layer trunk"
    n_prompt = x_prompt.shape[0]
    x_prompt, x_sample = x_prompt.astype(F32), x_sample.astype(F32)
    metap = jnp.concatenate([jnp.zeros((FRONT, D_MODEL), F32), meta.astype(F32)], axis=0)
    row = lambda a: a.astype(F32).reshape(1, -1)

    u, q_pre, f_fwd, f_bwd, i_in, gates = _inproj(x_prompt, x_sample, metap, row(norm1_g[0]),
                                                  w_in[0].astype(BF16))

    y_s5 = _s5_mixer(u, *_s5_weights(s5_lam_re[0], s5_lam_im[0], s5_log_step[0], s5_b_re[0], s5_b_im[0],
                                     s5_c_re[0], s5_c_im[0], s5_d[0]))

    lb = jax.nn.softmax(hg_lb.astype(F32), axis=0)[0]
    o_f, o_b = _hgrn2(q_pre, f_fwd, f_bwd, i_in, row(lb))

    hs, h2 = _merge(x_prompt, x_sample, metap, y_s5, o_f, o_b, gates, w_glu[0].astype(BF16), row(hg_norm_g[0]),
                    w_hg_out[0].astype(BF16), w_out[0].astype(BF16), row(norm2_g[0]))

    bsz, p, _ = hs.shape
    hs = hs.reshape(bsz * p, D_MODEL)
    h2 = h2.reshape(bsz * p // 2, D_MODEL)
    r2, p2, cnt, p1 = _peer_scores(h2, peer_wq[0].astype(BF16), peer_keys[0].astype(BF16))
    u_tab, vt_tab = _pack_tables(peer_u[0], peer_v[0])
    o_t = _peer_dense(h2, u_tab, vt_tab, r2, p2, cnt, p1)
    fin = functools.partial(_final, o_t, hs, row(final_g), nt=p // TILE)
    return (fin(first_seq=0, n_seq=n_prompt), fin(first_seq=n_prompt, n_seq=bsz - n_prompt))
```

```python
import functools
import math

import jax
import jax.numpy as jnp
from jax import lax
from jax.experimental import pallas as pl
from jax.experimental.pallas import tpu as pltpu

F32 = jnp.float32
BF16 = jnp.bfloat16

D_MODEL = 1024
N_META = 16
EPS = 1e-6
TILE = 256
FRONT = TILE - N_META

S5_WIDTH = 512
S5_GROUP = 16
S5_GROUPS = 32
S5_STATE = 64
S5_CHUNK = 16
S5_LB = 128 // S5_GROUP
S5_NLB = S5_GROUPS // S5_LB
S5_K = S5_CHUNK * 128

HG_WIDTH = 512
HG_HEADS = 4
HG_DIM = 128
HG_CHUNK = 64

PEER_HEADS = 8
PEER_KEYS = 128
PEER_TOPK = 16
PEER_QDIM = 256
PEER_HALF = 128
PEER_ETILE = 1024
PEER_TTILE = 1280
PEER_NCAND = 56
PEER_HEAD_GROUP = 8

VMEM_LIMIT = 56 * 1024 * 1024

_NT = (((1,), (1,)), ((), ()))
_TN = (((0,), (0,)), ((), ()))


def _largest_divisor(n, target, multiple=1):
    best = None
    for d in range(multiple, min(n, target) + 1, multiple):
        if n % d == 0:
            best = d
    assert best is not None, (n, target, multiple)
    return best


def _pack(x):
    return pltpu.bitcast(x.astype(BF16), jnp.uint32)


def _unpack(x):
    return pltpu.bitcast(x, BF16)


def _params(*sem):
    return pltpu.CompilerParams(dimension_semantics=sem, vmem_limit_bytes=VMEM_LIMIT)


def _hidden_tile(xp_ref, xs_ref, metap_ref, n_prompt):
    b, i = pl.program_id(0), pl.program_id(1)
    x = jnp.where(b < n_prompt, xp_ref[0], xs_ref[0])
    return jnp.where(i == 0, metap_ref[...], x)


def _x_specs(n_prompt, nt):
    tile = lambda i: jnp.maximum(i - 1, 0)
    prompt = pl.BlockSpec((1, TILE, D_MODEL), lambda b, i: (
        jnp.minimum(b, n_prompt - 1), jnp.where(b < n_prompt, tile(i), nt - 2), 0))
    sample = pl.BlockSpec((1, TILE, D_MODEL), lambda b, i: (
        jnp.maximum(b - n_prompt, 0), jnp.where(b < n_prompt, 0, tile(i)), 0))
    return [prompt, sample]


def _inproj_kernel(xp_ref, xs_ref, metap_ref, g_ref, w_ref, u_ref, q_ref, ff_ref, fb_ref, v_ref, gate_ref, *,
                   n_prompt):
    i = pl.program_id(1)
    hs = _hidden_tile(xp_ref, xs_ref, metap_ref, n_prompt)
    ms = jnp.mean(hs * hs, axis=-1, keepdims=True)
    h = (hs * lax.rsqrt(ms + EPS) * g_ref[...]).astype(BF16)
    row = lax.broadcasted_iota(jnp.int32, (TILE, 1), 0)
    valid = jnp.logical_or(i > 0, row >= FRONT)
    u = jnp.dot(h, w_ref[:, 0:S5_WIDTH], preferred_element_type=F32)
    u_ref[0] = jnp.where(valid, u, 0.0)
    off = S5_WIDTH
    for ref in (q_ref, ff_ref, fb_ref, v_ref):
        ref[0] = jnp.dot(h, w_ref[:, off:off + HG_WIDTH], preferred_element_type=F32)
        off += HG_WIDTH
    gate_ref[0] = jnp.dot(h, w_ref[:, off:], preferred_element_type=F32)


def _inproj(x_prompt, x_sample, metap, g, w_in):
    n_prompt, length, _ = x_prompt.shape
    bsz = n_prompt + x_sample.shape[0]
    nt = length // TILE + 1
    p = nt * TILE
    ncols = w_in.shape[1]
    ngate = ncols - S5_WIDTH - 4 * HG_WIDTH
    tok = lambda width: pl.BlockSpec((1, TILE, width), lambda b, i: (b, i, 0))
    full = lambda shape: pl.BlockSpec(shape, lambda b, i: (0,) * len(shape))
    hg_shape = jax.ShapeDtypeStruct((bsz, p, HG_WIDTH), F32)
    return pl.pallas_call(
        functools.partial(_inproj_kernel, n_prompt=n_prompt),
        grid=(bsz, nt),
        in_specs=_x_specs(n_prompt, nt) + [full((TILE, D_MODEL)), full((1, D_MODEL)), full((D_MODEL, ncols))],
        out_specs=[tok(S5_WIDTH), tok(HG_WIDTH), tok(HG_WIDTH), tok(HG_WIDTH), tok(HG_WIDTH), tok(ngate)],
        out_shape=[hg_shape, hg_shape, hg_shape, hg_shape, hg_shape, jax.ShapeDtypeStruct((bsz, p, ngate), F32)],
        compiler_params=_params("arbitrary", "arbitrary"),
        name="inproj",
    )(x_prompt, x_sample, metap, g, w_in)


def _s5_weights(lam_re, lam_im, log_step, b_re, b_im, c_re, c_im, d_skip):
    hi = lax.Precision.HIGHEST
    f = lambda a: a.astype(F32)
    lam_re, lam_im, log_step = f(lam_re), f(lam_im), f(log_step)
    b_re, b_im, c_re, c_im, d_skip = f(b_re), f(b_im), f(c_re), f(c_im), f(d_skip)
    step = jnp.exp(log_step)[:, :, None]
    pw = jnp.arange(S5_CHUNK + 1, dtype=F32)[:, None, None, None]
    mag = jnp.exp(pw * lam_re[None] * step[None])
    ang = pw * lam_im[None] * step[None]
    a_re, a_im = mag * jnp.cos(ang), mag * jnp.sin(ang)
    den = lam_re * lam_re + lam_im * lam_im
    n_re, n_im = a_re[1] - 1.0, a_im[1]
    coef_re = (n_re * lam_re + n_im * lam_im) / den
    coef_im = (n_im * lam_re - n_re * lam_im) / den
    w_re = coef_re[None] * a_re - coef_im[None] * a_im
    w_im = coef_re[None] * a_im + coef_im[None] * a_re

    def conv_taps(direction):
        wr, wi = w_re[:S5_CHUNK, direction], w_im[:S5_CHUNK, direction]
        m_re = wr[..., None] * b_re[None] - wi[..., None] * b_im[None]
        m_im = wr[..., None] * b_im[None] + wi[..., None] * b_re[None]
        return (jnp.einsum('gon,dgni->dgoi', c_re, m_re, precision=hi)
                - jnp.einsum('gon,dgni->dgoi', c_im, m_im, precision=hi))

    k_f, k_b = conv_taps(0), conv_taps(1)
    center = k_f[0] + k_b[0] + jnp.eye(S5_GROUP, dtype=F32)[None] * d_skip.reshape(S5_GROUPS, S5_GROUP, 1)
    taps = jnp.concatenate([k_b[:0:-1], center[None], k_f[1:]], axis=0)
    t_idx = jnp.arange(S5_CHUNK)
    toe = taps[t_idx[None, :] - t_idx[:, None] + S5_CHUNK - 1]
    split = lambda a, axis: a.reshape(a.shape[:axis] + (S5_NLB, S5_LB) + a.shape[axis + 1:])

    def block_diag(src):
        src = src.astype(BF16)
        width = src.shape[-1]
        lane_pad = lambda g: [(0, 0, 0)] * 4 + [(g * width, (S5_LB - 1 - g) * width, 0)]
        parts = [lax.pad(src[:, g], jnp.zeros((), BF16), lane_pad(g)) for g in range(S5_LB)]
        return jnp.stack(parts, axis=2).reshape(S5_NLB, S5_K, S5_K)

    toe = block_diag(split(toe.transpose(2, 0, 4, 1, 3), 0))

    def end_map(direction, powers):
        pr, pi = a_re[powers, direction], a_im[powers, direction]
        return [pr[..., None] * b_re[None] - pi[..., None] * b_im[None],
                pr[..., None] * b_im[None] + pi[..., None] * b_re[None]]

    ends = jnp.stack(end_map(0, S5_CHUNK - 1 - t_idx) + end_map(1, t_idx))
    ends = block_diag(split(ends.transpose(2, 1, 4, 0, 3), 0))

    def out_map(direction, powers):
        wr, wi = w_re[powers, direction], w_im[powers, direction]
        return [c_re[None] * wr[:, :, None, :] - c_im[None] * wi[:, :, None, :],
                -c_re[None] * wi[:, :, None, :] - c_im[None] * wr[:, :, None, :]]

    outs = jnp.stack(out_map(0, t_idx + 1) + out_map(1, S5_CHUNK - t_idx))
    outs = block_diag(split(outs.transpose(2, 0, 4, 1, 3), 0))

    def decay(direction):
        ar = split(a_re[S5_CHUNK, direction], 0).reshape(S5_NLB, 1, S5_K // 4)
        ai = split(a_im[S5_CHUNK, direction], 0).reshape(S5_NLB, 1, S5_K // 4)
        same = jnp.concatenate([ar, ar], axis=2).reshape(1, S5_NLB * S5_K // 2)
        cross = jnp.concatenate([-ai, ai], axis=2).reshape(1, S5_NLB * S5_K // 2)
        return same, cross

    return toe.astype(BF16), ends.astype(BF16), outs.astype(BF16), decay(0) + decay(1)


def _chunk_rows(u_ref, nc):
    parts = [u_ref[0, pl.ds(s, nc, stride=S5_CHUNK), :] for s in range(S5_CHUNK)]
    return jnp.concatenate(parts, axis=1).astype(BF16)


def _s5_local_kernel(u_ref, w_ref, ef_ref, eb_ref, *, nc):
    e = jnp.dot(_chunk_rows(u_ref, nc), w_ref[0], preferred_element_type=F32)
    ef_ref[0] = e[:, :S5_K // 2]
    eb_ref[0] = e[:, S5_K // 2:]


def _s5_local(u, ends, nc):
    bsz, p, _ = u.shape
    nch = p // S5_CHUNK
    out = lambda: pl.BlockSpec((1, nc, S5_K // 2), lambda lb, b, i: (b, i, lb))
    shape = jax.ShapeDtypeStruct((bsz, nch, S5_NLB * S5_K // 2), F32)
    return pl.pallas_call(
        functools.partial(_s5_local_kernel, nc=nc),
        grid=(S5_NLB, bsz, nch // nc),
        in_specs=[pl.BlockSpec((1, nc * S5_CHUNK, 128), lambda lb, b, i: (b, i, lb)),
                  pl.BlockSpec((1, S5_K, S5_K), lambda lb, b, i: (lb, 0, 0))],
        out_specs=[out(), out()],
        out_shape=[shape, shape],
        compiler_params=_params("arbitrary", "arbitrary", "arbitrary"),
        name="s5_local",
    )(u, ends)


def _swap_re_im(x):
    half = S5_K // 4
    parts = []
    for lb in range(x.shape[1] // (2 * half)):
        parts += [x[:, (2 * lb + 1) * half:(2 * lb + 2) * half], x[:, 2 * lb * half:(2 * lb + 1) * half]]
    return jnp.concatenate(parts, axis=1)


def _s5_scan_kernel(fs_ref, fc_ref, bs_ref, bc_ref, ef_ref, eb_ref, xf_ref, xb_ref, st_ref, *, ct):
    j = pl.program_id(1)

    @pl.when(j == 0)
    def _():
        st_ref[...] = jnp.zeros_like(st_ref)

    width = S5_K
    for part in range(st_ref.shape[2] // width):
        lanes = slice(part * width, (part + 1) * width)
        f_same, f_cross = fs_ref[:, lanes], fc_ref[:, lanes]
        b_same, b_cross = bs_ref[:, lanes], bc_ref[:, lanes]

        def body(c, carry):
            sf, sb = carry
            xf_ref[0, pl.ds(c, 1), lanes] = sf
            sf = f_same * sf + f_cross * _swap_re_im(sf) + ef_ref[0, pl.ds(c, 1), lanes]
            cb = ct - 1 - c
            xb_ref[0, pl.ds(cb, 1), lanes] = sb
            sb = b_same * sb + b_cross * _swap_re_im(sb) + eb_ref[0, pl.ds(cb, 1), lanes]
            return sf, sb

        sf, sb = lax.fori_loop(0, ct, body, (st_ref[0, :, lanes], st_ref[1, :, lanes]))
        st_ref[0, :, lanes] = sf
        st_ref[1, :, lanes] = sb


def _s5_scan(ef, eb, dec):
    bsz, nch, width = ef.shape
    ct = _largest_divisor(nch, 208, 8)
    nj = nch // ct
    fwd = lambda: pl.BlockSpec((1, ct, width), lambda b, j: (b, j, 0))
    bwd = lambda: pl.BlockSpec((1, ct, width), lambda b, j: (b, nj - 1 - j, 0))
    row = lambda: pl.BlockSpec((1, width), lambda b, j: (0, 0))
    shape = jax.ShapeDtypeStruct(ef.shape, F32)
    return pl.pallas_call(
        functools.partial(_s5_scan_kernel, ct=ct),
        grid=(bsz, nj),
        in_specs=[row(), row(), row(), row(), fwd(), bwd()],
        out_specs=[fwd(), bwd()],
        out_shape=[shape, shape],
        scratch_shapes=[pltpu.VMEM((2, 1, width), F32)],
        compiler_params=_params("parallel", "arbitrary"),
        name="s5_scan",
    )(*dec, ef, eb)


def _s5_out_kernel(u_ref, xf_ref, xb_ref, toe_ref, outs_ref, y_ref, *, nc):
    states = jnp.concatenate([xf_ref[0], xb_ref[0]], axis=1).astype(BF16)
    y = jnp.dot(_chunk_rows(u_ref, nc), toe_ref[0], preferred_element_type=F32)
    y += jnp.dot(states, outs_ref[0], preferred_element_type=F32)
    for t in range(S5_CHUNK):
        y_ref[0, pl.ds(t, nc, stride=S5_CHUNK), :] = y[:, t * 128:(t + 1) * 128]


def _s5_out(u, xf, xb, toe, outs, nc):
    bsz, p, _ = u.shape
    nch = p // S5_CHUNK
    tok = lambda: pl.BlockSpec((1, nc * S5_CHUNK, 128), lambda lb, b, i: (b, i, lb))
    state = lambda: pl.BlockSpec((1, nc, S5_K // 2), lambda lb, b, i: (b, i, lb))
    weight = lambda: pl.BlockSpec((1, S5_K, S5_K), lambda lb, b, i: (lb, 0, 0))
    return pl.pallas_call(
        functools.partial(_s5_out_kernel, nc=nc),
        grid=(S5_NLB, bsz, nch // nc),
        in_specs=[tok(), state(), state(), weight(), weight()],
        out_specs=tok(),
        out_shape=jax.ShapeDtypeStruct(u.shape, F32),
        compiler_params=_params("arbitrary", "arbitrary", "arbitrary"),
        name="s5_out",
    )(u, xf, xb, toe, outs)


def _s5_mixer(u, toe, ends, outs, dec):
    nch = u.shape[1] // S5_CHUNK
    nc = _largest_divisor(nch, 260, 8)
    ef, eb = _s5_local(u, ends, nc)
    xf, xb = _s5_scan(ef, eb, dec)
    return _s5_out(u, xf, xb, toe, outs, nc)


def _hg_kernel(lb_ref, qf_ref, ff_ref, vf_ref, qb_ref, fb_ref, vb_ref, of_ref, ob_ref, st_ref):
    i = pl.program_id(1)
    n = pl.num_programs(1)

    @pl.when(i == 0)
    def _():
        st_ref[...] = jnp.zeros_like(st_ref)

    lb = lb_ref[...]
    n_chunks = TILE // HG_CHUNK
    shift = HG_CHUNK.bit_length() - 1
    r = lax.broadcasted_iota(jnp.int32, (TILE, TILE), 0)
    c = lax.broadcasted_iota(jnp.int32, (TILE, TILE), 1)
    same_chunk = jnp.right_shift(r, shift) == jnp.right_shift(c, shift)
    r64 = lax.broadcasted_iota(jnp.int32, (HG_CHUNK, HG_CHUNK), 0)
    c64 = lax.broadcasted_iota(jnp.int32, (HG_CHUNK, HG_CHUNK), 1)
    row = lax.broadcasted_iota(jnp.int32, (TILE, 1), 0)
    passes = (
        (0, qf_ref, ff_ref, vf_ref, of_ref, i, c <= r, c64 <= r64, range(n_chunks), HG_CHUNK - 1),
        (1, qb_ref, fb_ref, vb_ref, ob_ref, n - 1 - i, c >= r, c64 >= r64, reversed(range(n_chunks)), 0),
    )
    for d, q_ref, f_ref, v_ref, o_ref, tile, before, keep, chunks, tot_row in passes:
        q_pre = q_ref[0]
        q = q_pre * jax.nn.sigmoid(q_pre)
        g = lb + (1.0 - lb) * jax.nn.sigmoid(f_ref[0])
        valid = tile * TILE + row >= FRONT
        log_f = jnp.where(valid, jnp.log(g), 0.0)
        k = jnp.where(valid, 1.0 - g, 0.0)
        cum = jnp.logical_and(same_chunk, before).astype(F32)
        b = jnp.dot(cum, log_f, preferred_element_type=F32, precision=lax.Precision.HIGHEST)
        totals = [b[ch * HG_CHUNK + tot_row:ch * HG_CHUNK + tot_row + 1] for ch in range(n_chunks)]
        b_tot = jnp.concatenate([jnp.broadcast_to(t, (HG_CHUNK, HG_WIDTH)) for t in totals], axis=0)
        q_dec = (q * jnp.exp(b)).astype(BF16)
        k_inv = (k * jnp.exp(-b)).astype(BF16)
        k_end = (k * jnp.exp(b_tot - b)).astype(BF16)
        v = v_ref[0].astype(BF16)
        states = [st_ref[d, h] for h in range(HG_HEADS)]
        for ch in chunks:
            rows = slice(ch * HG_CHUNK, (ch + 1) * HG_CHUNK)
            decay = jnp.exp(totals[ch])
            for h in range(HG_HEADS):
                ls = slice(h * HG_DIM, (h + 1) * HG_DIM)
                scores = lax.dot_general(q_dec[rows, ls], k_inv[rows, ls], _NT, preferred_element_type=F32)
                scores = jnp.where(keep, scores, 0.0).astype(BF16)
                o = jnp.dot(scores, v[rows, ls], preferred_element_type=F32)
                o += lax.dot_general(q_dec[rows, ls], states[h].astype(BF16), _NT, preferred_element_type=F32)
                kv = lax.dot_general(v[rows, ls], k_end[rows, ls], _TN, preferred_element_type=F32)
                states[h] = decay[:, ls] * states[h] + kv
                o_ref[0, rows, ls] = o
        for h in range(HG_HEADS):
            st_ref[d, h] = states[h]


def _hgrn2(q_pre, f_fwd, f_bwd, i_in, lb):
    bsz, p, _ = q_pre.shape
    nt = p // TILE
    fwd = lambda: pl.BlockSpec((1, TILE, HG_WIDTH), lambda b, i: (b, i, 0))
    bwd = lambda: pl.BlockSpec((1, TILE, HG_WIDTH), lambda b, i: (b, nt - 1 - i, 0))
    shape = jax.ShapeDtypeStruct((bsz, p, HG_WIDTH), F32)
    return pl.pallas_call(
        _hg_kernel,
        grid=(bsz, nt),
        in_specs=[pl.BlockSpec((1, HG_WIDTH), lambda b, i: (0, 0)), fwd(), fwd(), fwd(), bwd(), bwd(), bwd()],
        out_specs=[fwd(), bwd()],
        out_shape=[shape, shape],
        scratch_shapes=[pltpu.VMEM((2, HG_HEADS, HG_DIM, HG_DIM), F32)],
        compiler_params=_params("parallel", "arbitrary"),
        name="hgrn2",
    )(lb, q_pre, f_fwd, i_in, q_pre, f_bwd, i_in)


def _gelu(x):
    return 0.5 * x * (1.0 + lax.erf(x * (1.0 / math.sqrt(2.0))))


def _merge_kernel(xp_ref, xs_ref, metap_ref, ys_ref, of_ref, ob_ref, gate_ref, wglu_ref, hgn_ref, whg_ref,
                  wout_ref, n2_ref, hs_ref, h2_ref, *, n_prompt):
    hs = _hidden_tile(xp_ref, xs_ref, metap_ref, n_prompt)
    glu = jnp.dot(_gelu(ys_ref[0]).astype(BF16), wglu_ref[...], preferred_element_type=F32)
    y_a = glu[:, :D_MODEL] * jax.nn.sigmoid(glu[:, D_MODEL:])
    o = of_ref[0] + ob_ref[0]
    normed = []
    for h in range(HG_HEADS):
        oh = o[:, h * HG_DIM:(h + 1) * HG_DIM]
        normed.append(oh * lax.rsqrt(jnp.mean(oh * oh, axis=-1, keepdims=True) + EPS))
    o_gate = gate_ref[0, :, 0:HG_WIDTH]
    y_hg = jnp.concatenate(normed, axis=-1) * hgn_ref[...] * (o_gate * jax.nn.sigmoid(o_gate))
    y_b = jnp.dot(y_hg.astype(BF16), whg_ref[...], preferred_element_type=F32)
    gate_a = gate_ref[0, :, HG_WIDTH:HG_WIDTH + D_MODEL]
    gate_b = gate_ref[0, :, HG_WIDTH + D_MODEL:]
    mixed = jax.nn.sigmoid(gate_a) * y_a + jax.nn.sigmoid(gate_b) * y_b
    hs = hs + jnp.dot(mixed.astype(BF16), wout_ref[...], preferred_element_type=F32)
    hs_ref[0] = hs
    ms = jnp.mean(hs * hs, axis=-1, keepdims=True)
    h2_ref[0] = _pack(hs * lax.rsqrt(ms + EPS) * n2_ref[...])


def _merge(x_prompt, x_sample, metap, y_s5, o_f, o_b, gates, w_glu, hg_norm_g, w_hg_out, w_out, norm2_g):
    bsz, p, _ = y_s5.shape
    nt = p // TILE
    n_prompt = x_prompt.shape[0]
    tok = lambda width: pl.BlockSpec((1, TILE, width), lambda b, i: (b, i, 0))
    full = lambda a: pl.BlockSpec(a.shape, lambda b, i: (0,) * a.ndim)
    return pl.pallas_call(
        functools.partial(_merge_kernel, n_prompt=n_prompt),
        grid=(bsz, nt),
        in_specs=_x_specs(n_prompt, nt) + [
            full(metap), tok(S5_WIDTH), tok(HG_WIDTH), tok(HG_WIDTH), tok(gates.shape[-1]),
            full(w_glu), full(hg_norm_g), full(w_hg_out), full(w_out), full(norm2_g),
        ],
        out_specs=[tok(D_MODEL), pl.BlockSpec((1, TILE // 2, D_MODEL), lambda b, i: (b, i, 0))],
        out_shape=[jax.ShapeDtypeStruct((bsz, p, D_MODEL), F32),
                   jax.ShapeDtypeStruct((bsz, p // 2, D_MODEL), jnp.uint32)],
        compiler_params=_params("arbitrary", "arbitrary"),
        name="merge",
    )(x_prompt, x_sample, metap, y_s5, o_f, o_b, gates, w_glu, hg_norm_g, w_hg_out, w_out, norm2_g)


def _staircase():
    return [(i, PEER_TOPK // (i + 1)) for i in range(PEER_TOPK)]


def _top16(s):
    work = s
    tops = []
    for it in range(PEER_TOPK):
        m = jnp.max(work, axis=0, keepdims=True)
        work = jnp.where(work == m, -jnp.inf, work)
        tops.append(m)
    return tops


def _spread_by_rank(s, tops, values, default):
    out = jnp.full(s.shape, default, F32)
    for t, v in zip(tops, values):
        out = jnp.where(s == t, v, out)
    return out


def _score_kernel(h2_ref, wq_ref, keys_ref, r2_ref, p2_ref, cnt_ref, p1_ref, cand_ref):
    def head_group(g, carry):
        for k in range(PEER_HEAD_GROUP):
            select_head(g * PEER_HEAD_GROUP + k, cand_ref.at[k])
        return carry

    def select_head(h, cand_ref):
        qh = jnp.dot(_unpack(h2_ref[...]), wq_ref[h], preferred_element_type=F32).astype(BF16)
        s1_all = lax.dot_general(keys_ref[h, 0], qh[:, :PEER_HALF], _NT, preferred_element_type=F32)
        s2_all = lax.dot_general(keys_ref[h, 1], qh[:, PEER_HALF:], _NT, preferred_element_type=F32)
        for part in range(TILE // 128):
            lanes = slice(part * 128, (part + 1) * 128)
            s1, s2 = s1_all[:, lanes], s2_all[:, lanes]
            t1 = _top16(s1)
            t2 = _top16(s2)
            t2_all = jnp.concatenate(t2, axis=0)
            cand_ref[:, lanes] = jnp.full((PEER_NCAND, 128), -jnp.inf, F32)
            off = 0
            for i, n_i in _staircase():
                cand_ref[off:off + n_i, lanes] = t1[i] + t2_all[0:n_i]
                off += n_i
            cand = cand_ref[:, lanes]
            c_max = t1[0] + t2[0]
            work = cand
            z = jnp.zeros_like(c_max)
            tau = c_max
            for it in range(PEER_TOPK):
                tau = jnp.max(work, axis=0, keepdims=True)
                z = z + jnp.exp(tau - c_max)
                work = jnp.where(work == tau, -jnp.inf, work)
            chosen = (cand >= tau).astype(F32)
            counts, off = [], 0
            for i, n_i in _staircase():
                counts.append(jnp.sum(chosen[off:off + n_i], axis=0, keepdims=True))
                off += n_i
            r2_ref[h, 0, :, lanes] = _pack(_spread_by_rank(s2, t2, [float(i) for i in range(PEER_TOPK)],
                                                           float(PEER_TOPK)))
            p2_ref[h, 0, :, lanes] = _pack(jnp.exp(s2 - t2[0]))
            cnt_ref[h, 0, :, lanes] = _spread_by_rank(s1, t1, counts, 0.0)
            p1_ref[h, 0, :, lanes] = 0.5 * jnp.exp(s1 - t1[0]) / z

    lax.fori_loop(0, PEER_HEADS // PEER_HEAD_GROUP, head_group, 0)


def _peer_scores(h2, w_q, keys):
    ntok = 2 * h2.shape[0]
    blk = lambda rows: pl.BlockSpec((PEER_HEADS, 1, rows, TILE), lambda i: (0, i, 0, 0))
    packed = jax.ShapeDtypeStruct((PEER_HEADS, ntok // TILE, PEER_KEYS // 2, TILE), jnp.uint32)
    plain = jax.ShapeDtypeStruct((PEER_HEADS, ntok // TILE, PEER_KEYS, TILE), F32)
    return pl.pallas_call(
        _score_kernel,
        grid=(ntok // TILE,),
        in_specs=[
            pl.BlockSpec((TILE // 2, D_MODEL), lambda i: (i, 0)),
            pl.BlockSpec(w_q.shape, lambda i: (0, 0, 0)),
            pl.BlockSpec(keys.shape, lambda i: (0, 0, 0, 0)),
        ],
        out_specs=[blk(PEER_KEYS // 2), blk(PEER_KEYS // 2), blk(PEER_KEYS), blk(PEER_KEYS)],
        out_shape=[packed, packed, plain, plain],
        scratch_shapes=[pltpu.VMEM((PEER_HEAD_GROUP, PEER_NCAND, TILE), F32)],
        compiler_params=_params("parallel"),
        name="peer_scores",
    )(h2, w_q, keys)


def _pack_tables_kernel(u_ref, v_ref, up_ref, vtp_ref):
    up_ref[...] = _pack(u_ref[...])
    vtp_ref[...] = _pack(v_ref[...].T)


def _pack_tables(u_tab, v_tab):
    nexp, d = u_tab.shape
    te = _largest_divisor(nexp, 512, 128)
    tile = lambda: pl.BlockSpec((te, d), lambda e: (e, 0))
    return pl.pallas_call(
        _pack_tables_kernel,
        grid=(nexp // te,),
        in_specs=[tile(), tile()],
        out_specs=[pl.BlockSpec((te // 2, d), lambda e: (e, 0)), pl.BlockSpec((d // 2, te), lambda e: (0, e))],
        out_shape=[jax.ShapeDtypeStruct((nexp // 2, d), jnp.uint32), jax.ShapeDtypeStruct((d // 2, nexp), jnp.uint32)],
        compiler_params=_params("parallel"),
        name="pack_tables",
    )(u_tab.astype(F32), v_tab.astype(F32))


def _dense_kernel(h2_ref, u_ref, vt_ref, r2_ref, p2_ref, cnt_ref, p1_ref, o_ref, act0_ref, act1_ref, wa0_ref,
                  wa1_ref, *, n_sub, n_etiles):
    s = pl.program_id(1)

    @pl.when(s == 0)
    def _():
        o_ref[...] = jnp.zeros_like(o_ref)

    n_first = PEER_ETILE // PEER_KEYS
    zero = jnp.zeros((), BF16)
    spread = lambda row: jnp.broadcast_to(row, (PEER_KEYS, 128)).astype(BF16)

    def sub_tile(c, carry, *, stages, act_a, act_b, wa_b, wa_c):
        if "A" in stages:
            h2 = _unpack(h2_ref[pl.ds(pl.multiple_of(c * (TILE // 2), TILE // 2), TILE // 2), :])
            act_a[c] = lax.dot_general(_unpack(u_ref[...]), h2, _NT, preferred_element_type=F32)
        for lc in range(TILE // 128 if "B" in stages else 0):
            lanes = slice(lc * 128, (lc + 1) * 128)
            for j in range(0, n_first, 2):
                w0 = jnp.zeros((PEER_KEYS, 128), BF16)
                w1 = jnp.zeros((PEER_KEYS, 128), BF16)
                for h in range(PEER_HEADS):
                    r2 = _unpack(r2_ref[h, c, :, lanes])
                    p2 = _unpack(p2_ref[h, c, :, lanes])
                    cnt = cnt_ref[h, c, :, lanes]
                    p1 = p1_ref[h, c, :, lanes]
                    w0 += jnp.where(r2 < spread(cnt[j:j + 1]), p2, zero) * spread(p1[j:j + 1])
                    w1 += jnp.where(r2 < spread(cnt[j + 1:j + 2]), p2, zero) * spread(p1[j + 1:j + 2])
                for k, w in ((j, w0), (j + 1, w1)):
                    rows = slice(k * PEER_KEYS, (k + 1) * PEER_KEYS)
                    act = act_b[c, rows, lanes]
                    twice_gelu = act * (1.0 + lax.erf(act * (1.0 / math.sqrt(2.0))))
                    wa_b[c, rows, lanes] = w * twice_gelu.astype(BF16)
        if "C" in stages:
            o_ref[c] += jnp.dot(_unpack(vt_ref[...]), wa_c[c], preferred_element_type=F32)
        return carry

    def run(condition, stages, parity):
        bufs = (dict(act_a=act0_ref, act_b=act1_ref, wa_b=wa1_ref, wa_c=wa0_ref) if parity == 0 else
                dict(act_a=act1_ref, act_b=act0_ref, wa_b=wa0_ref, wa_c=wa1_ref))

        @pl.when(condition)
        def _():
            lax.fori_loop(0, n_sub, functools.partial(sub_tile, stages=stages, **bufs), 0)

    steady = jnp.logical_and(s >= 2, s < n_etiles)
    run(s == 0, "A", 0)
    run(s == 1, "AB", 1)
    run(jnp.logical_and(steady, lax.rem(s, 2) == 0), "ABC", 0)
    run(jnp.logical_and(steady, lax.rem(s, 2) == 1), "ABC", 1)
    run(s == n_etiles, "BC", n_etiles % 2)
    run(s == n_etiles + 1, "C", (n_etiles + 1) % 2)


def _peer_dense(h2, u_tab, vt_tab, r2, p2, cnt, p1):
    ntok = 2 * h2.shape[0]
    n_sub = _largest_divisor(ntok // TILE, PEER_TTILE // TILE)
    tn = n_sub * TILE
    n_etiles = 2 * u_tab.shape[0] // PEER_ETILE
    last = n_etiles - 1
    stage = lambda s, lag: jnp.clip(s - lag, 0, last)
    second = lambda: pl.BlockSpec((PEER_HEADS, n_sub, PEER_KEYS // 2, TILE), lambda t, s: (0, t, 0, 0))
    first = lambda: pl.BlockSpec((PEER_HEADS, n_sub, PEER_ETILE // PEER_KEYS, TILE),
                                 lambda t, s: (0, t, stage(s, 1), 0))
    return pl.pallas_call(
        functools.partial(_dense_kernel, n_sub=n_sub, n_etiles=n_etiles),
        grid=(ntok // tn, n_etiles + 2),
        in_specs=[pl.BlockSpec((tn // 2, D_MODEL), lambda t, s: (t, 0)),
                  pl.BlockSpec((PEER_ETILE // 2, D_MODEL), lambda t, s: (stage(s, 0), 0)),
                  pl.BlockSpec((D_MODEL // 2, PEER_ETILE), lambda t, s: (0, stage(s, 2))),
                  second(), second(), first(), first()],
        out_specs=pl.BlockSpec((n_sub, D_MODEL, TILE), lambda t, s: (t, 0, 0)),
        out_shape=jax.ShapeDtypeStruct((ntok // TILE, D_MODEL, TILE), F32),
        scratch_shapes=[pltpu.VMEM((n_sub, PEER_ETILE, TILE), F32), pltpu.VMEM((n_sub, PEER_ETILE, TILE), F32),
                        pltpu.VMEM((n_sub, PEER_ETILE, TILE), BF16), pltpu.VMEM((n_sub, PEER_ETILE, TILE), BF16)],
        compiler_params=_params("parallel", "arbitrary"),
        name="peer_dense",
    )(h2, u_tab, vt_tab, r2, p2, cnt, p1)


def _final_kernel(ot_ref, hs_ref, g_ref, y_ref):
    hs = hs_ref[...] + ot_ref[0].T
    ms = jnp.mean(hs * hs, axis=-1, keepdims=True)
    y_ref[0] = hs * lax.rsqrt(ms + EPS) * g_ref[...]


def _final(o_t, hs, final_g, first_seq, n_seq, nt):
    tile_of = lambda b, i: (b + first_seq) * nt + i + 1
    return pl.pallas_call(
        _final_kernel,
        grid=(n_seq, nt - 1),
        in_specs=[
            pl.BlockSpec((1, D_MODEL, TILE), lambda b, i: (tile_of(b, i), 0, 0)),
            pl.BlockSpec((TILE, D_MODEL), lambda b, i: (tile_of(b, i), 0)),
            pl.BlockSpec((1, D_MODEL), lambda b, i: (0, 0)),
        ],
        out_specs=pl.BlockSpec((1, TILE, D_MODEL), lambda b, i: (b, i, 0)),
        out_shape=jax.ShapeDtypeStruct((n_seq, (nt - 1) * TILE, D_MODEL), F32),
        compiler_params=_params("parallel", "parallel"),
        name="final",
    )(o_t, hs, final_g)


def kernel(x_prompt, x_sample, meta, norm1_g, w_in, s5_lam_re, s5_lam_im, s5_log_step, s5_b_re, s5_b_im,
           s5_c_re, s5_c_im, s5_d, w_glu, hg_lb, hg_norm_g, w_hg_out, w_out, norm2_g, peer_wq, peer_keys,
           peer_u, peer_v, final_g):
    assert x_prompt.shape[1] == x_sample.shape[1] and x_prompt.shape[1] % TILE == 0
    assert norm1_g.shape[0] == 1, "single-layer trunk"
    n_prompt = x_prompt.shape[0]
    x_prompt, x_sample = x_prompt.astype(F32), x_sample.astype(F32)
    metap = jnp.concatenate([jnp.zeros((FRONT, D_MODEL), F32), meta.astype(F32)], axis=0)
    row = lambda a: a.astype(F32).reshape(1, -1)

    u, q_pre, f_fwd, f_bwd, i_in, gates = _inproj(x_prompt, x_sample, metap, row(norm1_g[0]),
                                                  w_in[0].astype(BF16))

    y_s5 = _s5_mixer(u, *_s5_weights(s5_lam_re[0], s5_lam_im[0], s5_log_step[0], s5_b_re[0], s5_b_im[0],
                                     s5_c_re[0], s5_c_im[0], s5_d[0]))

    lb = jax.nn.softmax(hg_lb.astype(F32), axis=0)[0]
    o_f, o_b = _hgrn2(q_pre, f_fwd, f_bwd, i_in, row(lb))

    hs, h2 = _merge(x_prompt, x_sample, metap, y_s5, o_f, o_b, gates, w_glu[0].astype(BF16), row(hg_norm_g[0]),
                    w_hg_out[0].astype(BF16), w_out[0].astype(BF16), row(norm2_g[0]))

    bsz, p, _ = hs.shape
    hs = hs.reshape(bsz * p, D_MODEL)
    h2 = h2.reshape(bsz * p // 2, D_MODEL)
    w_q = peer_wq[0].astype(BF16).reshape(D_MODEL, PEER_HEADS, PEER_QDIM).transpose(1, 0, 2)
    r2, p2, cnt, p1 = _peer_scores(h2, w_q, peer_keys[0].astype(BF16))
    u_tab, vt_tab = _pack_tables(peer_u[0], peer_v[0])
    o_t = _peer_dense(h2, u_tab, vt_tab, r2, p2, cnt, p1)
    fin = functools.partial(_final, o_t, hs, row(final_g), nt=p // TILE)
    return (fin(first_seq=0, n_seq=n_prompt), fin(first_seq=n_prompt, n_seq=bsz - n_prompt))
```

```python
import functools
import math

import jax
import jax.numpy as jnp
from jax import lax
from jax.experimental import pallas as pl
from jax.experimental.pallas import tpu as pltpu

F32 = jnp.float32
BF16 = jnp.bfloat16

D_MODEL = 1024
N_META = 16
EPS = 1e-6
TILE = 256
FRONT = TILE - N_META

S5_WIDTH = 512
S5_GROUP = 16
S5_GROUPS = 32
S5_STATE = 64
S5_CHUNK = 16
S5_LB = 128 // S5_GROUP
S5_NLB = S5_GROUPS // S5_LB
S5_K = S5_CHUNK * 128

HG_WIDTH = 512
HG_HEADS = 4
HG_DIM = 128
HG_CHUNK = 64

PEER_HEADS = 8
PEER_KEYS = 128
PEER_TOPK = 16
PEER_QDIM = 256
PEER_HALF = 128
PEER_ETILE = 1024
PEER_TTILE = 1280
PEER_NCAND = 56
PEER_HEAD_GROUP = 8

VMEM_LIMIT = 56 * 1024 * 1024

_NT = (((1,), (1,)), ((), ()))
_TN = (((0,), (0,)), ((), ()))


def _largest_divisor(n, target, multiple=1):
    best = None
    for d in range(multiple, min(n, target) + 1, multiple):
        if n % d == 0:
            best = d
    assert best is not None, (n, target, multiple)
    return best


def _pack(x):
    return pltpu.bitcast(x.astype(BF16), jnp.uint32)


def _unpack(x):
    return pltpu.bitcast(x, BF16)


def _params(*sem):
    return pltpu.CompilerParams(dimension_semantics=sem, vmem_limit_bytes=VMEM_LIMIT)


def _hidden_tile(xp_ref, xs_ref, metap_ref, n_prompt):
    b, i = pl.program_id(0), pl.program_id(1)
    x = jnp.where(b < n_prompt, xp_ref[0], xs_ref[0])
    return jnp.where(i == 0, metap_ref[...], x)


def _x_specs(n_prompt, nt):
    tile = lambda i: jnp.maximum(i - 1, 0)
    prompt = pl.BlockSpec((1, TILE, D_MODEL), lambda b, i: (
        jnp.minimum(b, n_prompt - 1), jnp.where(b < n_prompt, tile(i), nt - 2), 0))
    sample = pl.BlockSpec((1, TILE, D_MODEL), lambda b, i: (
        jnp.maximum(b - n_prompt, 0), jnp.where(b < n_prompt, 0, tile(i)), 0))
    return [prompt, sample]


def _inproj_kernel(xp_ref, xs_ref, metap_ref, g_ref, w_ref, u_ref, q_ref, ff_ref, fb_ref, v_ref, gate_ref, *,
                   n_prompt):
    i = pl.program_id(1)
    hs = _hidden_tile(xp_ref, xs_ref, metap_ref, n_prompt)
    ms = jnp.mean(hs * hs, axis=-1, keepdims=True)
    h = (hs * lax.rsqrt(ms + EPS) * g_ref[...]).astype(BF16)
    row = lax.broadcasted_iota(jnp.int32, (TILE, 1), 0)
    valid = jnp.logical_or(i > 0, row >= FRONT)
    u = jnp.dot(h, w_ref[:, 0:S5_WIDTH], preferred_element_type=F32)
    u_ref[0] = jnp.where(valid, u, 0.0)
    off = S5_WIDTH
    for ref in (q_ref, ff_ref, fb_ref, v_ref):
        ref[0] = jnp.dot(h, w_ref[:, off:off + HG_WIDTH], preferred_element_type=F32)
        off += HG_WIDTH
    gate_ref[0] = jnp.dot(h, w_ref[:, off:], preferred_element_type=F32)


def _inproj(x_prompt, x_sample, metap, g, w_in):
    n_prompt, length, _ = x_prompt.shape
    bsz = n_prompt + x_sample.shape[0]
    nt = length // TILE + 1
    p = nt * TILE
    ncols = w_in.shape[1]
    ngate = ncols - S5_WIDTH - 4 * HG_WIDTH
    tok = lambda width: pl.BlockSpec((1, TILE, width), lambda b, i: (b, i, 0))
    full = lambda shape: pl.BlockSpec(shape, lambda b, i: (0,) * len(shape))
    hg_shape = jax.ShapeDtypeStruct((bsz, p, HG_WIDTH), F32)
    return pl.pallas_call(
        functools.partial(_inproj_kernel, n_prompt=n_prompt),
        grid=(bsz, nt),
        in_specs=_x_specs(n_prompt, nt) + [full((TILE, D_MODEL)), full((1, D_MODEL)), full((D_MODEL, ncols))],
        out_specs=[tok(S5_WIDTH), tok(HG_WIDTH), tok(HG_WIDTH), tok(HG_WIDTH), tok(HG_WIDTH), tok(ngate)],
        out_shape=[hg_shape, hg_shape, hg_shape, hg_shape, hg_shape, jax.ShapeDtypeStruct((bsz, p, ngate), F32)],
        compiler_params=_params("arbitrary", "arbitrary"),
        name="inproj",
    )(x_prompt, x_sample, metap, g, w_in)


def _s5_weights(lam_re, lam_im, log_step, b_re, b_im, c_re, c_im, d_skip):
    hi = lax.Precision.HIGHEST
    f = lambda a: a.astype(F32)
    lam_re, lam_im, log_step = f(lam_re), f(lam_im), f(log_step)
    b_re, b_im, c_re, c_im, d_skip = f(b_re), f(b_im), f(c_re), f(c_im), f(d_skip)
    step = jnp.exp(log_step)[:, :, None]
    pw = jnp.arange(S5_CHUNK + 1, dtype=F32)[:, None, None, None]
    mag = jnp.exp(pw * lam_re[None] * step[None])
    ang = pw * lam_im[None] * step[None]
    a_re, a_im = mag * jnp.cos(ang), mag * jnp.sin(ang)
    den = lam_re * lam_re + lam_im * lam_im
    n_re, n_im = a_re[1] - 1.0, a_im[1]
    coef_re = (n_re * lam_re + n_im * lam_im) / den
    coef_im = (n_im * lam_re - n_re * lam_im) / den
    w_re = coef_re[None] * a_re - coef_im[None] * a_im
    w_im = coef_re[None] * a_im + coef_im[None] * a_re

    def conv_taps(direction):
        wr, wi = w_re[:S5_CHUNK, direction], w_im[:S5_CHUNK, direction]
        m_re = wr[..., None] * b_re[None] - wi[..., None] * b_im[None]
        m_im = wr[..., None] * b_im[None] + wi[..., None] * b_re[None]
        return (jnp.einsum('gon,dgni->dgoi', c_re, m_re, precision=hi)
                - jnp.einsum('gon,dgni->dgoi', c_im, m_im, precision=hi))

    k_f, k_b = conv_taps(0), conv_taps(1)
    center = k_f[0] + k_b[0] + jnp.eye(S5_GROUP, dtype=F32)[None] * d_skip.reshape(S5_GROUPS, S5_GROUP, 1)
    taps = jnp.concatenate([k_b[:0:-1], center[None], k_f[1:]], axis=0)
    t_idx = jnp.arange(S5_CHUNK)
    toe = taps[t_idx[None, :] - t_idx[:, None] + S5_CHUNK - 1]
    split = lambda a, axis: a.reshape(a.shape[:axis] + (S5_NLB, S5_LB) + a.shape[axis + 1:])

    def block_diag(src):
        src = src.astype(BF16)
        width = src.shape[-1]
        lane_pad = lambda g: [(0, 0, 0)] * 4 + [(g * width, (S5_LB - 1 - g) * width, 0)]
        parts = [lax.pad(src[:, g], jnp.zeros((), BF16), lane_pad(g)) for g in range(S5_LB)]
        return jnp.stack(parts, axis=2).reshape(S5_NLB, S5_K, S5_K)

    toe = block_diag(split(toe.transpose(2, 0, 4, 1, 3), 0))

    def end_map(direction, powers):
        pr, pi = a_re[powers, direction], a_im[powers, direction]
        return [pr[..., None] * b_re[None] - pi[..., None] * b_im[None],
                pr[..., None] * b_im[None] + pi[..., None] * b_re[None]]

    ends = jnp.stack(end_map(0, S5_CHUNK - 1 - t_idx) + end_map(1, t_idx))
    ends = block_diag(split(ends.transpose(2, 1, 4, 0, 3), 0))

    def out_map(direction, powers):
        wr, wi = w_re[powers, direction], w_im[powers, direction]
        return [c_re[None] * wr[:, :, None, :] - c_im[None] * wi[:, :, None, :],
                -c_re[None] * wi[:, :, None, :] - c_im[None] * wr[:, :, None, :]]

    outs = jnp.stack(out_map(0, t_idx + 1) + out_map(1, S5_CHUNK - t_idx))
    outs = block_diag(split(outs.transpose(2, 0, 4, 1, 3), 0))

    def decay(direction):
        ar = split(a_re[S5_CHUNK, direction], 0).reshape(S5_NLB, 1, S5_K // 4)
        ai = split(a_im[S5_CHUNK, direction], 0).reshape(S5_NLB, 1, S5_K // 4)
        same = jnp.concatenate([ar, ar], axis=2).reshape(1, S5_NLB * S5_K // 2)
        cross = jnp.concatenate([-ai, ai], axis=2).reshape(1, S5_NLB * S5_K // 2)
        return same, cross

    return toe.astype(BF16), ends.astype(BF16), outs.astype(BF16), decay(0) + decay(1)


def _chunk_rows(u_ref, nc):
    parts = [u_ref[0, pl.ds(s, nc, stride=S5_CHUNK), :] for s in range(S5_CHUNK)]
    return jnp.concatenate(parts, axis=1).astype(BF16)


def _s5_local_kernel(u_ref, w_ref, ef_ref, eb_ref, *, nc):
    e = jnp.dot(_chunk_rows(u_ref, nc), w_ref[0], preferred_element_type=F32)
    ef_ref[0] = e[:, :S5_K // 2]
    eb_ref[0] = e[:, S5_K // 2:]


def _s5_local(u, ends, nc):
    bsz, p, _ = u.shape
    nch = p // S5_CHUNK
    out = lambda: pl.BlockSpec((1, nc, S5_K // 2), lambda lb, b, i: (b, i, lb))
    shape = jax.ShapeDtypeStruct((bsz, nch, S5_NLB * S5_K // 2), F32)
    return pl.pallas_call(
        functools.partial(_s5_local_kernel, nc=nc),
        grid=(S5_NLB, bsz, nch // nc),
        in_specs=[pl.BlockSpec((1, nc * S5_CHUNK, 128), lambda lb, b, i: (b, i, lb)),
                  pl.BlockSpec((1, S5_K, S5_K), lambda lb, b, i: (lb, 0, 0))],
        out_specs=[out(), out()],
        out_shape=[shape, shape],
        compiler_params=_params("arbitrary", "arbitrary", "arbitrary"),
        name="s5_local",
    )(u, ends)


def _swap_re_im(x):
    half = S5_K // 4
    parts = []
    for lb in range(x.shape[1] // (2 * half)):
        parts += [x[:, (2 * lb + 1) * half:(2 * lb + 2) * half], x[:, 2 * lb * half:(2 * lb + 1) * half]]
    return jnp.concatenate(parts, axis=1)


def _s5_scan_kernel(fs_ref, fc_ref, bs_ref, bc_ref, ef_ref, eb_ref, xf_ref, xb_ref, st_ref, *, ct):
    j = pl.program_id(1)

    @pl.when(j == 0)
    def _():
        st_ref[...] = jnp.zeros_like(st_ref)

    width = S5_K
    for part in range(st_ref.shape[2] // width):
        lanes = slice(part * width, (part + 1) * width)
        f_same, f_cross = fs_ref[:, lanes], fc_ref[:, lanes]
        b_same, b_cross = bs_ref[:, lanes], bc_ref[:, lanes]

        def body(c, carry):
            sf, sb = carry
            xf_ref[0, pl.ds(c, 1), lanes] = sf
            sf = f_same * sf + f_cross * _swap_re_im(sf) + ef_ref[0, pl.ds(c, 1), lanes]
            cb = ct - 1 - c
            xb_ref[0, pl.ds(cb, 1), lanes] = sb
            sb = b_same * sb + b_cross * _swap_re_im(sb) + eb_ref[0, pl.ds(cb, 1), lanes]
            return sf, sb

        sf, sb = lax.fori_loop(0, ct, body, (st_ref[0, :, lanes], st_ref[1, :, lanes]))
        st_ref[0, :, lanes] = sf
        st_ref[1, :, lanes] = sb


def _s5_scan(ef, eb, dec):
    bsz, nch, width = ef.shape
    ct = _largest_divisor(nch, 208, 8)
    nj = nch // ct
    fwd = lambda: pl.BlockSpec((1, ct, width), lambda b, j: (b, j, 0))
    bwd = lambda: pl.BlockSpec((1, ct, width), lambda b, j: (b, nj - 1 - j, 0))
    row = lambda: pl.BlockSpec((1, width), lambda b, j: (0, 0))
    shape = jax.ShapeDtypeStruct(ef.shape, F32)
    return pl.pallas_call(
        functools.partial(_s5_scan_kernel, ct=ct),
        grid=(bsz, nj),
        in_specs=[row(), row(), row(), row(), fwd(), bwd()],
        out_specs=[fwd(), bwd()],
        out_shape=[shape, shape],
        scratch_shapes=[pltpu.VMEM((2, 1, width), F32)],
        compiler_params=_params("parallel", "arbitrary"),
        name="s5_scan",
    )(*dec, ef, eb)


def _s5_out_kernel(u_ref, xf_ref, xb_ref, toe_ref, outs_ref, y_ref, *, nc):
    states = jnp.concatenate([xf_ref[0], xb_ref[0]], axis=1).astype(BF16)
    y = jnp.dot(_chunk_rows(u_ref, nc), toe_ref[0], preferred_element_type=F32)
    y += jnp.dot(states, outs_ref[0], preferred_element_type=F32)
    for t in range(S5_CHUNK):
        y_ref[0, pl.ds(t, nc, stride=S5_CHUNK), :] = y[:, t * 128:(t + 1) * 128]


def _s5_out(u, xf, xb, toe, outs, nc):
    bsz, p, _ = u.shape
    nch = p // S5_CHUNK
    tok = lambda: pl.BlockSpec((1, nc * S5_CHUNK, 128), lambda lb, b, i: (b, i, lb))
    state = lambda: pl.BlockSpec((1, nc, S5_K // 2), lambda lb, b, i: (b, i, lb))
    weight = lambda: pl.BlockSpec((1, S5_K, S5_K), lambda lb, b, i: (lb, 0, 0))
    return pl.pallas_call(
        functools.partial(_s5_out_kernel, nc=nc),
        grid=(S5_NLB, bsz, nch // nc),
        in_specs=[tok(), state(), state(), weight(), weight()],
        out_specs=tok(),
        out_shape=jax.ShapeDtypeStruct(u.shape, F32),
        compiler_params=_params("arbitrary", "arbitrary", "arbitrary"),
        name="s5_out",
    )(u, xf, xb, toe, outs)


def _s5_mixer(u, toe, ends, outs, dec):
    nch = u.shape[1] // S5_CHUNK
    nc = _largest_divisor(nch, 260, 8)
    ef, eb = _s5_local(u, ends, nc)
    xf, xb = _s5_scan(ef, eb, dec)
    return _s5_out(u, xf, xb, toe, outs, nc)


def _hg_kernel(lb_ref, qf_ref, ff_ref, vf_ref, qb_ref, fb_ref, vb_ref, of_ref, ob_ref, st_ref):
    i = pl.program_id(1)
    n = pl.num_programs(1)

    @pl.when(i == 0)
    def _():
        st_ref[...] = jnp.zeros_like(st_ref)

    lb = lb_ref[...]
    n_chunks = TILE // HG_CHUNK
    shift = HG_CHUNK.bit_length() - 1
    r = lax.broadcasted_iota(jnp.int32, (TILE, TILE), 0)
    c = lax.broadcasted_iota(jnp.int32, (TILE, TILE), 1)
    same_chunk = jnp.right_shift(r, shift) == jnp.right_shift(c, shift)
    r64 = lax.broadcasted_iota(jnp.int32, (HG_CHUNK, HG_CHUNK), 0)
    c64 = lax.broadcasted_iota(jnp.int32, (HG_CHUNK, HG_CHUNK), 1)
    row = lax.broadcasted_iota(jnp.int32, (TILE, 1), 0)
    passes = (
        (0, qf_ref, ff_ref, vf_ref, of_ref, i, c <= r, c64 <= r64, range(n_chunks), HG_CHUNK - 1),
        (1, qb_ref, fb_ref, vb_ref, ob_ref, n - 1 - i, c >= r, c64 >= r64, reversed(range(n_chunks)), 0),
    )
    for d, q_ref, f_ref, v_ref, o_ref, tile, before, keep, chunks, tot_row in passes:
        q_pre = q_ref[0]
        q = q_pre * jax.nn.sigmoid(q_pre)
        g = lb + (1.0 - lb) * jax.nn.sigmoid(f_ref[0])
        valid = tile * TILE + row >= FRONT
        log_f = jnp.where(valid, jnp.log(g), 0.0)
        k = jnp.where(valid, 1.0 - g, 0.0)
        cum = jnp.logical_and(same_chunk, before).astype(BF16)
        b, rest = None, log_f
        for _ in range(3):
            piece = rest.astype(BF16)
            rest = rest - piece.astype(F32)
            part = jnp.dot(cum, piece, preferred_element_type=F32)
            b = part if b is None else b + part
        totals = [b[ch * HG_CHUNK + tot_row:ch * HG_CHUNK + tot_row + 1] for ch in range(n_chunks)]
        b_tot = jnp.concatenate([jnp.broadcast_to(t, (HG_CHUNK, HG_WIDTH)) for t in totals], axis=0)
        q_dec = (q * jnp.exp(b)).astype(BF16)
        k_inv = (k * jnp.exp(-b)).astype(BF16)
        k_end = (k * jnp.exp(b_tot - b)).astype(BF16)
        v = v_ref[0].astype(BF16)
        states = [st_ref[d, h] for h in range(HG_HEADS)]
        for ch in chunks:
            rows = slice(ch * HG_CHUNK, (ch + 1) * HG_CHUNK)
            decay = jnp.exp(totals[ch])
            for h in range(HG_HEADS):
                ls = slice(h * HG_DIM, (h + 1) * HG_DIM)
                scores = lax.dot_general(q_dec[rows, ls], k_inv[rows, ls], _NT, preferred_element_type=F32)
                scores = jnp.where(keep, scores, 0.0).astype(BF16)
                o = jnp.dot(scores, v[rows, ls], preferred_element_type=F32)
                o += lax.dot_general(q_dec[rows, ls], states[h].astype(BF16), _NT, preferred_element_type=F32)
                kv = lax.dot_general(v[rows, ls], k_end[rows, ls], _TN, preferred_element_type=F32)
                states[h] = decay[:, ls] * states[h] + kv
                o_ref[0, rows, ls] = o
        for h in range(HG_HEADS):
            st_ref[d, h] = states[h]


def _hgrn2(q_pre, f_fwd, f_bwd, i_in, lb):
    bsz, p, _ = q_pre.shape
    nt = p // TILE
    fwd = lambda: pl.BlockSpec((1, TILE, HG_WIDTH), lambda b, i: (b, i, 0))
    bwd = lambda: pl.BlockSpec((1, TILE, HG_WIDTH), lambda b, i: (b, nt - 1 - i, 0))
    shape = jax.ShapeDtypeStruct((bsz, p, HG_WIDTH), F32)
    return pl.pallas_call(
        _hg_kernel,
        grid=(bsz, nt),
        in_specs=[pl.BlockSpec((1, HG_WIDTH), lambda b, i: (0, 0)), fwd(), fwd(), fwd(), bwd(), bwd(), bwd()],
        out_specs=[fwd(), bwd()],
        out_shape=[shape, shape],
        scratch_shapes=[pltpu.VMEM((2, HG_HEADS, HG_DIM, HG_DIM), F32)],
        compiler_params=_params("parallel", "arbitrary"),
        name="hgrn2",
    )(lb, q_pre, f_fwd, i_in, q_pre, f_bwd, i_in)


def _gelu(x):
    return 0.5 * x * (1.0 + lax.erf(x * (1.0 / math.sqrt(2.0))))


def _merge_kernel(xp_ref, xs_ref, metap_ref, ys_ref, of_ref, ob_ref, gate_ref, wglu_ref, hgn_ref, whg_ref,
                  wout_ref, n2_ref, hs_ref, h2_ref, *, n_prompt):
    hs = _hidden_tile(xp_ref, xs_ref, metap_ref, n_prompt)
    glu = jnp.dot(_gelu(ys_ref[0]).astype(BF16), wglu_ref[...], preferred_element_type=F32)
    y_a = glu[:, :D_MODEL] * jax.nn.sigmoid(glu[:, D_MODEL:])
    o = of_ref[0] + ob_ref[0]
    normed = []
    for h in range(HG_HEADS):
        oh = o[:, h * HG_DIM:(h + 1) * HG_DIM]
        normed.append(oh * lax.rsqrt(jnp.mean(oh * oh, axis=-1, keepdims=True) + EPS))
    o_gate = gate_ref[0, :, 0:HG_WIDTH]
    y_hg = jnp.concatenate(normed, axis=-1) * hgn_ref[...] * (o_gate * jax.nn.sigmoid(o_gate))
    y_b = jnp.dot(y_hg.astype(BF16), whg_ref[...], preferred_element_type=F32)
    gate_a = gate_ref[0, :, HG_WIDTH:HG_WIDTH + D_MODEL]
    gate_b = gate_ref[0, :, HG_WIDTH + D_MODEL:]
    mixed = jax.nn.sigmoid(gate_a) * y_a + jax.nn.sigmoid(gate_b) * y_b
    hs = hs + jnp.dot(mixed.astype(BF16), wout_ref[...], preferred_element_type=F32)
    hs_ref[0] = hs
    ms = jnp.mean(hs * hs, axis=-1, keepdims=True)
    h2_ref[0] = _pack(hs * lax.rsqrt(ms + EPS) * n2_ref[...])


def _merge(x_prompt, x_sample, metap, y_s5, o_f, o_b, gates, w_glu, hg_norm_g, w_hg_out, w_out, norm2_g):
    bsz, p, _ = y_s5.shape
    nt = p // TILE
    n_prompt = x_prompt.shape[0]
    tok = lambda width: pl.BlockSpec((1, TILE, width), lambda b, i: (b, i, 0))
    full = lambda a: pl.BlockSpec(a.shape, lambda b, i: (0,) * a.ndim)
    return pl.pallas_call(
        functools.partial(_merge_kernel, n_prompt=n_prompt),
        grid=(bsz, nt),
        in_specs=_x_specs(n_prompt, nt) + [
            full(metap), tok(S5_WIDTH), tok(HG_WIDTH), tok(HG_WIDTH), tok(gates.shape[-1]),
            full(w_glu), full(hg_norm_g), full(w_hg_out), full(w_out), full(norm2_g),
        ],
        out_specs=[tok(D_MODEL), pl.BlockSpec((1, TILE // 2, D_MODEL), lambda b, i: (b, i, 0))],
        out_shape=[jax.ShapeDtypeStruct((bsz, p, D_MODEL), F32),
                   jax.ShapeDtypeStruct((bsz, p // 2, D_MODEL), jnp.uint32)],
        compiler_params=_params("arbitrary", "arbitrary"),
        name="merge",
    )(x_prompt, x_sample, metap, y_s5, o_f, o_b, gates, w_glu, hg_norm_g, w_hg_out, w_out, norm2_g)


def _staircase():
    return [(i, PEER_TOPK // (i + 1)) for i in range(PEER_TOPK)]


def _exchange(v, i, l):
    v[i], v[l] = jnp.maximum(v[i], v[l]), jnp.minimum(v[i], v[l])


def _bitonic_merge_desc(v):
    j = len(v) // 2
    while j >= 1:
        for i in range(len(v)):
            if i ^ j > i:
                _exchange(v, i, i ^ j)
        j //= 2


def _top16(s):
    v = [s[8 * i:8 * (i + 1)] for i in range(PEER_KEYS // 8)]
    k = 2
    while k <= len(v):
        j = k // 2
        while j >= 1:
            for i in range(len(v)):
                l = i ^ j
                if l > i:
                    if (i & k) == 0 or k == len(v):
                        _exchange(v, i, l)
                    else:
                        _exchange(v, l, i)
            j //= 2
        k *= 2
    for shift in (4, 2, 1):
        other = [pltpu.roll(a, shift, 0) for a in v]
        v = [jnp.maximum(a, other[len(v) - 1 - i]) for i, a in enumerate(v)]
        _bitonic_merge_desc(v)
    return [a[0:1] for a in v]


def _spread_by_rank(s, tops, values, default):
    out = jnp.full(s.shape, default, F32)
    for t, v in zip(tops, values):
        out = jnp.where(s == t, v, out)
    return out


def _score_kernel(h2_ref, wq_ref, keys_ref, r2_ref, p2_ref, cnt_ref, p1_ref, cand_ref):
    def head_group(g, carry):
        for k in range(PEER_HEAD_GROUP):
            select_head(g * PEER_HEAD_GROUP + k, cand_ref.at[k])
        return carry

    def select_head(h, cand_ref):
        qh = jnp.dot(_unpack(h2_ref[...]), wq_ref[h], preferred_element_type=F32).astype(BF16)
        s1_all = lax.dot_general(keys_ref[h, 0], qh[:, :PEER_HALF], _NT, preferred_element_type=F32)
        s2_all = lax.dot_general(keys_ref[h, 1], qh[:, PEER_HALF:], _NT, preferred_element_type=F32)
        for part in range(TILE // 128):
            lanes = slice(part * 128, (part + 1) * 128)
            s1, s2 = s1_all[:, lanes], s2_all[:, lanes]
            t1 = _top16(s1)
            t2 = _top16(s2)
            t2_all = jnp.concatenate(t2, axis=0)
            cand_ref[:, lanes] = jnp.full((PEER_NCAND, 128), -jnp.inf, F32)
            off = 0
            for i, n_i in _staircase():
                cand_ref[off:off + n_i, lanes] = t1[i] + t2_all[0:n_i]
                off += n_i
            cand = cand_ref[:, lanes]
            c_max = t1[0] + t2[0]
            work = cand
            z = jnp.zeros_like(c_max)
            tau = c_max
            for it in range(PEER_TOPK):
                tau = jnp.max(work, axis=0, keepdims=True)
                z = z + jnp.exp(tau - c_max)
                work = jnp.where(work == tau, -jnp.inf, work)
            chosen = (cand >= tau).astype(F32)
            counts, off = [], 0
            for i, n_i in _staircase():
                counts.append(jnp.sum(chosen[off:off + n_i], axis=0, keepdims=True))
                off += n_i
            r2_ref[h, 0, :, lanes] = _pack(_spread_by_rank(s2, t2, [float(i) for i in range(PEER_TOPK)],
                                                           float(PEER_TOPK)))
            p2_ref[h, 0, :, lanes] = _pack(jnp.exp(s2 - t2[0]))
            cnt_ref[h, 0, :, lanes] = _spread_by_rank(s1, t1, counts, 0.0)
            p1_ref[h, 0, :, lanes] = 0.5 * jnp.exp(s1 - t1[0]) / z

    lax.fori_loop(0, PEER_HEADS // PEER_HEAD_GROUP, head_group, 0)


def _peer_scores(h2, w_q, keys):
    ntok = 2 * h2.shape[0]
    blk = lambda rows: pl.BlockSpec((PEER_HEADS, 1, rows, TILE), lambda i: (0, i, 0, 0))
    packed = jax.ShapeDtypeStruct((PEER_HEADS, ntok // TILE, PEER_KEYS // 2, TILE), jnp.uint32)
    plain = jax.ShapeDtypeStruct((PEER_HEADS, ntok // TILE, PEER_KEYS, TILE), F32)
    return pl.pallas_call(
        _score_kernel,
        grid=(ntok // TILE,),
        in_specs=[
            pl.BlockSpec((TILE // 2, D_MODEL), lambda i: (i, 0)),
            pl.BlockSpec(w_q.shape, lambda i: (0, 0, 0)),
            pl.BlockSpec(keys.shape, lambda i: (0, 0, 0, 0)),
        ],
        out_specs=[blk(PEER_KEYS // 2), blk(PEER_KEYS // 2), blk(PEER_KEYS), blk(PEER_KEYS)],
        out_shape=[packed, packed, plain, plain],
        scratch_shapes=[pltpu.VMEM((PEER_HEAD_GROUP, PEER_NCAND, TILE), F32)],
        compiler_params=_params("parallel"),
        name="peer_scores",
    )(h2, w_q, keys)


def _pack_tables_kernel(u_ref, v_ref, up_ref, vtp_ref):
    up_ref[...] = _pack(u_ref[...])
    vtp_ref[...] = _pack(v_ref[...].T)


def _pack_tables(u_tab, v_tab):
    nexp, d = u_tab.shape
    te = _largest_divisor(nexp, 512, 128)
    tile = lambda: pl.BlockSpec((te, d), lambda e: (e, 0))
    return pl.pallas_call(
        _pack_tables_kernel,
        grid=(nexp // te,),
        in_specs=[tile(), tile()],
        out_specs=[pl.BlockSpec((te // 2, d), lambda e: (e, 0)), pl.BlockSpec((d // 2, te), lambda e: (0, e))],
        out_shape=[jax.ShapeDtypeStruct((nexp // 2, d), jnp.uint32), jax.ShapeDtypeStruct((d // 2, nexp), jnp.uint32)],
        compiler_params=_params("parallel"),
        name="pack_tables",
    )(u_tab.astype(F32), v_tab.astype(F32))


def _dense_kernel(h2_ref, u_ref, vt_ref, r2_ref, p2_ref, cnt_ref, p1_ref, o_ref, act0_ref, act1_ref, wa0_ref,
                  wa1_ref, *, n_sub, n_etiles):
    s = pl.program_id(1)

    @pl.when(s == 0)
    def _():
        o_ref[...] = jnp.zeros_like(o_ref)

    n_first = PEER_ETILE // PEER_KEYS
    zero = jnp.zeros((), BF16)
    spread = lambda row: jnp.broadcast_to(row, (PEER_KEYS, 128)).astype(BF16)

    def sub_tile(c, carry, *, stages, act_a, act_b, wa_b, wa_c):
        if "A" in stages:
            h2 = _unpack(h2_ref[pl.ds(pl.multiple_of(c * (TILE // 2), TILE // 2), TILE // 2), :])
            act_a[c] = lax.dot_general(_unpack(u_ref[...]), h2, _NT, preferred_element_type=F32)
        for lc in range(TILE // 128 if "B" in stages else 0):
            lanes = slice(lc * 128, (lc + 1) * 128)
            for j in range(0, n_first, 2):
                w0 = jnp.zeros((PEER_KEYS, 128), BF16)
                w1 = jnp.zeros((PEER_KEYS, 128), BF16)
                for h in range(PEER_HEADS):
                    r2 = _unpack(r2_ref[h, c, :, lanes])
                    p2 = _unpack(p2_ref[h, c, :, lanes])
                    cnt = cnt_ref[h, c, :, lanes]
                    p1 = p1_ref[h, c, :, lanes]
                    w0 += jnp.where(r2 < spread(cnt[j:j + 1]), p2, zero) * spread(p1[j:j + 1])
                    w1 += jnp.where(r2 < spread(cnt[j + 1:j + 2]), p2, zero) * spread(p1[j + 1:j + 2])
                for k, w in ((j, w0), (j + 1, w1)):
                    rows = slice(k * PEER_KEYS, (k + 1) * PEER_KEYS)
                    act = act_b[c, rows, lanes]
                    twice_gelu = act * (1.0 + lax.erf(act * (1.0 / math.sqrt(2.0))))
                    wa_b[c, rows, lanes] = w * twice_gelu.astype(BF16)
        if "C" in stages:
            o_ref[c] += jnp.dot(_unpack(vt_ref[...]), wa_c[c], preferred_element_type=F32)
        return carry

    def run(condition, stages, parity):
        bufs = (dict(act_a=act0_ref, act_b=act1_ref, wa_b=wa1_ref, wa_c=wa0_ref) if parity == 0 else
                dict(act_a=act1_ref, act_b=act0_ref, wa_b=wa0_ref, wa_c=wa1_ref))

        @pl.when(condition)
        def _():
            lax.fori_loop(0, n_sub, functools.partial(sub_tile, stages=stages, **bufs), 0)

    steady = jnp.logical_and(s >= 2, s < n_etiles)
    run(s == 0, "A", 0)
    run(s == 1, "AB", 1)
    run(jnp.logical_and(steady, lax.rem(s, 2) == 0), "ABC", 0)
    run(jnp.logical_and(steady, lax.rem(s, 2) == 1), "ABC", 1)
    run(s == n_etiles, "BC", n_etiles % 2)
    run(s == n_etiles + 1, "C", (n_etiles + 1) % 2)


def _peer_dense(h2, u_tab, vt_tab, r2, p2, cnt, p1):
    ntok = 2 * h2.shape[0]
    n_sub = _largest_divisor(ntok // TILE, PEER_TTILE // TILE)
    tn = n_sub * TILE
    n_etiles = 2 * u_tab.shape[0] // PEER_ETILE
    last = n_etiles - 1
    stage = lambda s, lag: jnp.clip(s - lag, 0, last)
    second = lambda: pl.BlockSpec((PEER_HEADS, n_sub, PEER_KEYS // 2, TILE), lambda t, s: (0, t, 0, 0))
    first = lambda: pl.BlockSpec((PEER_HEADS, n_sub, PEER_ETILE // PEER_KEYS, TILE),
                                 lambda t, s: (0, t, stage(s, 1), 0))
    return pl.pallas_call(
        functools.partial(_dense_kernel, n_sub=n_sub, n_etiles=n_etiles),
        grid=(ntok // tn, n_etiles + 2),
        in_specs=[pl.BlockSpec((tn // 2, D_MODEL), lambda t, s: (t, 0)),
                  pl.BlockSpec((PEER_ETILE // 2, D_MODEL), lambda t, s: (stage(s, 0), 0)),
                  pl.BlockSpec((D_MODEL // 2, PEER_ETILE), lambda t, s: (0, stage(s, 2))),
                  second(), second(), first(), first()],
        out_specs=pl.BlockSpec((n_sub, D_MODEL, TILE), lambda t, s: (t, 0, 0)),
        out_shape=jax.ShapeDtypeStruct((ntok // TILE, D_MODEL, TILE), F32),
        scratch_shapes=[pltpu.VMEM((n_sub, PEER_ETILE, TILE), F32), pltpu.VMEM((n_sub, PEER_ETILE, TILE), F32),
                        pltpu.VMEM((n_sub, PEER_ETILE, TILE), BF16), pltpu.VMEM((n_sub, PEER_ETILE, TILE), BF16)],
        compiler_params=_params("parallel", "arbitrary"),
        name="peer_dense",
    )(h2, u_tab, vt_tab, r2, p2, cnt, p1)


def _final_kernel(ot_ref, hs_ref, g_ref, y_ref):
    hs = hs_ref[...] + ot_ref[0].T
    ms = jnp.mean(hs * hs, axis=-1, keepdims=True)
    y_ref[0] = hs * lax.rsqrt(ms + EPS) * g_ref[...]


def _final(o_t, hs, final_g, first_seq, n_seq, nt):
    tile_of = lambda b, i: (b + first_seq) * nt + i + 1
    return pl.pallas_call(
        _final_kernel,
        grid=(n_seq, nt - 1),
        in_specs=[
            pl.BlockSpec((1, D_MODEL, TILE), lambda b, i: (tile_of(b, i), 0, 0)),
            pl.BlockSpec((TILE, D_MODEL), lambda b, i: (tile_of(b, i), 0)),
            pl.BlockSpec((1, D_MODEL), lambda b, i: (0, 0)),
        ],
        out_specs=pl.BlockSpec((1, TILE, D_MODEL), lambda b, i: (b, i, 0)),
        out_shape=jax.ShapeDtypeStruct((n_seq, (nt - 1) * TILE, D_MODEL), F32),
        compiler_params=_params("parallel", "parallel"),
        name="final",
    )(o_t, hs, final_g)


def kernel(x_prompt, x_sample, meta, norm1_g, w_in, s5_lam_re, s5_lam_im, s5_log_step, s5_b_re, s5_b_im,
           s5_c_re, s5_c_im, s5_d, w_glu, hg_lb, hg_norm_g, w_hg_out, w_out, norm2_g, peer_wq, peer_keys,
           peer_u, peer_v, final_g):
    assert x_prompt.shape[1] == x_sample.shape[1] and x_prompt.shape[1] % TILE == 0
    assert norm1_g.shape[0] == 1, "single-layer trunk"
    n_prompt = x_prompt.shape[0]
    x_prompt, x_sample = x_prompt.astype(F32), x_sample.astype(F32)
    metap = jnp.concatenate([jnp.zeros((FRONT, D_MODEL), F32), meta.astype(F32)], axis=0)
    row = lambda a: a.astype(F32).reshape(1, -1)

    u, q_pre, f_fwd, f_bwd, i_in, gates = _inproj(x_prompt, x_sample, metap, row(norm1_g[0]),
                                                  w_in[0].astype(BF16))

    y_s5 = _s5_mixer(u, *_s5_weights(s5_lam_re[0], s5_lam_im[0], s5_log_step[0], s5_b_re[0], s5_b_im[0],
                                     s5_c_re[0], s5_c_im[0], s5_d[0]))

    lb = jax.nn.softmax(hg_lb.astype(F32), axis=0)[0]
    o_f, o_b = _hgrn2(q_pre, f_fwd, f_bwd, i_in, row(lb))

    hs, h2 = _merge(x_prompt, x_sample, metap, y_s5, o_f, o_b, gates, w_glu[0].astype(BF16), row(hg_norm_g[0]),
                    w_hg_out[0].astype(BF16), w_out[0].astype(BF16), row(norm2_g[0]))

    bsz, p, _ = hs.shape
    hs = hs.reshape(bsz * p, D_MODEL)
    h2 = h2.reshape(bsz * p // 2, D_MODEL)
    w_q = peer_wq[0].astype(BF16).reshape(D_MODEL, PEER_HEADS, PEER_QDIM).transpose(1, 0, 2)
    r2, p2, cnt, p1 = _peer_scores(h2, w_q, peer_keys[0].astype(BF16))
    u_tab, vt_tab = _pack_tables(peer_u[0], peer_v[0])
    o_t = _peer_dense(h2, u_tab, vt_tab, r2, p2, cnt, p1)
    fin = functools.partial(_final, o_t, hs, row(final_g), nt=p // TILE)
    return (fin(first_seq=0, n_seq=n_prompt), fin(first_seq=n_prompt, n_seq=bsz - n_prompt))
```

```python
import functools
import math

import jax
import jax.numpy as jnp
from jax import lax
from jax.experimental import pallas as pl
from jax.experimental.pallas import tpu as pltpu

F32 = jnp.float32
BF16 = jnp.bfloat16

D_MODEL = 1024
N_META = 16
EPS = 1e-6
TILE = 256
FRONT = TILE - N_META

S5_WIDTH = 512
S5_GROUP = 16
S5_GROUPS = 32
S5_STATE = 64
S5_CHUNK = 16
LANES = 128
SUBLANES = 8
S5_LB = LANES // S5_GROUP
S5_NLB = S5_GROUPS // S5_LB
S5_K = S5_CHUNK * LANES

HG_WIDTH = 512
HG_HEADS = 4
HG_DIM = 128
HG_CHUNK = 64

PEER_HEADS = 8
PEER_KEYS = 128
PEER_TOPK = 16
PEER_QDIM = 256
PEER_HALF = 128
PEER_ETILE = 1024
PEER_TTILE = 1280
PEER_NCAND = 56

VMEM_LIMIT = 56 * 1024 * 1024

_NT = (((1,), (1,)), ((), ()))
_TN = (((0,), (0,)), ((), ()))


def _largest_divisor(n, target, multiple=1):
    best = None
    for d in range(multiple, min(n, target) + 1, multiple):
        if n % d == 0:
            best = d
    assert best is not None, (n, target, multiple)
    return best


def _pack(x):
    return pltpu.bitcast(x.astype(BF16), jnp.uint32)


def _unpack(x):
    return pltpu.bitcast(x, BF16)


def _params(*sem):
    return pltpu.CompilerParams(dimension_semantics=sem, vmem_limit_bytes=VMEM_LIMIT)


def _hidden_tile(xp_ref, xs_ref, metap_ref, n_prompt):
    b, i = pl.program_id(0), pl.program_id(1)
    x = jnp.where(b < n_prompt, xp_ref[0], xs_ref[0])
    return jnp.where(i == 0, metap_ref[...], x)


def _x_specs(n_prompt, nt):
    tile = lambda i: jnp.maximum(i - 1, 0)
    prompt = pl.BlockSpec((1, TILE, D_MODEL), lambda b, i: (
        jnp.minimum(b, n_prompt - 1), jnp.where(b < n_prompt, tile(i), nt - 2), 0))
    sample = pl.BlockSpec((1, TILE, D_MODEL), lambda b, i: (
        jnp.maximum(b - n_prompt, 0), jnp.where(b < n_prompt, 0, tile(i)), 0))
    return [prompt, sample]


def _inproj_kernel(xp_ref, xs_ref, metap_ref, g_ref, w_ref, u_ref, q_ref, ff_ref, fb_ref, v_ref, gate_ref, *,
                   n_prompt):
    i = pl.program_id(1)
    hs = _hidden_tile(xp_ref, xs_ref, metap_ref, n_prompt)
    ms = jnp.mean(hs * hs, axis=-1, keepdims=True)
    h = (hs * lax.rsqrt(ms + EPS) * g_ref[...]).astype(BF16)
    row = lax.broadcasted_iota(jnp.int32, (TILE, 1), 0)
    valid = jnp.logical_or(i > 0, row >= FRONT)
    u = jnp.dot(h, w_ref[:, 0:S5_WIDTH], preferred_element_type=F32)
    u_ref[0] = jnp.where(valid, u, 0.0)
    off = S5_WIDTH
    for ref in (q_ref, ff_ref, fb_ref, v_ref):
        ref[0] = jnp.dot(h, w_ref[:, off:off + HG_WIDTH], preferred_element_type=F32)
        off += HG_WIDTH
    gate_ref[0] = jnp.dot(h, w_ref[:, off:], preferred_element_type=F32)


def _inproj(x_prompt, x_sample, metap, g, w_in):
    n_prompt, length, _ = x_prompt.shape
    bsz = n_prompt + x_sample.shape[0]
    nt = length // TILE + 1
    p = nt * TILE
    ncols = w_in.shape[1]
    ngate = ncols - S5_WIDTH - 4 * HG_WIDTH
    tok = lambda width: pl.BlockSpec((1, TILE, width), lambda b, i: (b, i, 0))
    full = lambda shape: pl.BlockSpec(shape, lambda b, i: (0,) * len(shape))
    hg_shape = jax.ShapeDtypeStruct((bsz, p, HG_WIDTH), F32)
    return pl.pallas_call(
        functools.partial(_inproj_kernel, n_prompt=n_prompt),
        grid=(bsz, nt),
        in_specs=_x_specs(n_prompt, nt) + [full((TILE, D_MODEL)), full((1, D_MODEL)), full((D_MODEL, ncols))],
        out_specs=[tok(S5_WIDTH), tok(HG_WIDTH), tok(HG_WIDTH), tok(HG_WIDTH), tok(HG_WIDTH), tok(ngate)],
        out_shape=[hg_shape, hg_shape, hg_shape, hg_shape, hg_shape, jax.ShapeDtypeStruct((bsz, p, ngate), F32)],
        compiler_params=_params("arbitrary", "arbitrary"),
        name="inproj",
    )(x_prompt, x_sample, metap, g, w_in)


def _s5_weights(lam_re, lam_im, log_step, b_re, b_im, c_re, c_im, d_skip):
    hi = lax.Precision.HIGHEST
    f = lambda a: a.astype(F32)
    lam_re, lam_im, log_step = f(lam_re), f(lam_im), f(log_step)
    b_re, b_im, c_re, c_im, d_skip = f(b_re), f(b_im), f(c_re), f(c_im), f(d_skip)
    step = jnp.exp(log_step)[:, :, None]
    pw = jnp.arange(S5_CHUNK + 1, dtype=F32)[:, None, None, None]
    mag = jnp.exp(pw * lam_re[None] * step[None])
    ang = pw * lam_im[None] * step[None]
    a_re, a_im = mag * jnp.cos(ang), mag * jnp.sin(ang)
    den = lam_re * lam_re + lam_im * lam_im
    n_re, n_im = a_re[1] - 1.0, a_im[1]
    coef_re = (n_re * lam_re + n_im * lam_im) / den
    coef_im = (n_im * lam_re - n_re * lam_im) / den
    w_re = coef_re[None] * a_re - coef_im[None] * a_im
    w_im = coef_re[None] * a_im + coef_im[None] * a_re

    def conv_taps(direction):
        wr, wi = w_re[:S5_CHUNK, direction], w_im[:S5_CHUNK, direction]
        m_re = wr[..., None] * b_re[None] - wi[..., None] * b_im[None]
        m_im = wr[..., None] * b_im[None] + wi[..., None] * b_re[None]
        return (jnp.einsum('gon,dgni->dgoi', c_re, m_re, precision=hi)
                - jnp.einsum('gon,dgni->dgoi', c_im, m_im, precision=hi))

    k_f, k_b = conv_taps(0), conv_taps(1)
    center = k_f[0] + k_b[0] + jnp.eye(S5_GROUP, dtype=F32)[None] * d_skip.reshape(S5_GROUPS, S5_GROUP, 1)
    taps = jnp.concatenate([k_b[:0:-1], center[None], k_f[1:]], axis=0)
    t_idx = jnp.arange(S5_CHUNK)
    toe = taps[t_idx[None, :] - t_idx[:, None] + S5_CHUNK - 1]
    split = lambda a, axis: a.reshape(a.shape[:axis] + (S5_NLB, S5_LB) + a.shape[axis + 1:])

    def block_diag(src):
        src = src.astype(BF16)
        width = src.shape[-1]
        lane_pad = lambda g: [(0, 0, 0)] * 4 + [(g * width, (S5_LB - 1 - g) * width, 0)]
        parts = [lax.pad(src[:, g], jnp.zeros((), BF16), lane_pad(g)) for g in range(S5_LB)]
        return jnp.stack(parts, axis=2).reshape(S5_NLB, S5_K, S5_K)

    toe = block_diag(split(toe.transpose(2, 0, 4, 1, 3), 0))

    def end_map(direction, powers):
        pr, pi = a_re[powers, direction], a_im[powers, direction]
        return [pr[..., None] * b_re[None] - pi[..., None] * b_im[None],
                pr[..., None] * b_im[None] + pi[..., None] * b_re[None]]

    ends = jnp.stack(end_map(0, S5_CHUNK - 1 - t_idx) + end_map(1, t_idx))
    ends = block_diag(split(ends.transpose(2, 1, 4, 0, 3), 0))

    def out_map(direction, powers):
        wr, wi = w_re[powers, direction], w_im[powers, direction]
        return [c_re[None] * wr[:, :, None, :] - c_im[None] * wi[:, :, None, :],
                -c_re[None] * wi[:, :, None, :] - c_im[None] * wr[:, :, None, :]]

    outs = jnp.stack(out_map(0, t_idx + 1) + out_map(1, S5_CHUNK - t_idx))
    outs = block_diag(split(outs.transpose(2, 0, 4, 1, 3), 0))

    def decay(direction):
        ar = split(a_re[S5_CHUNK, direction], 0).reshape(S5_NLB, 1, S5_K // 4)
        ai = split(a_im[S5_CHUNK, direction], 0).reshape(S5_NLB, 1, S5_K // 4)
        same = jnp.concatenate([ar, ar], axis=2).reshape(1, S5_NLB * S5_K // 2)
        cross = jnp.concatenate([-ai, ai], axis=2).reshape(1, S5_NLB * S5_K // 2)
        return same, cross

    return toe.astype(BF16), ends.astype(BF16), outs.astype(BF16), decay(0) + decay(1)


def _chunk_rows(u_ref, nc):
    parts = [u_ref[0, pl.ds(s, nc, stride=S5_CHUNK), :] for s in range(S5_CHUNK)]
    return jnp.concatenate(parts, axis=1).astype(BF16)


def _s5_local_kernel(u_ref, w_ref, ef_ref, eb_ref, *, nc):
    e = jnp.dot(_chunk_rows(u_ref, nc), w_ref[0], preferred_element_type=F32)
    ef_ref[0] = e[:, :S5_K // 2]
    eb_ref[0] = e[:, S5_K // 2:]


def _s5_local(u, ends, nc):
    bsz, p, _ = u.shape
    nch = p // S5_CHUNK
    out = lambda: pl.BlockSpec((1, nc, S5_K // 2), lambda lb, b, i: (b, i, lb))
    shape = jax.ShapeDtypeStruct((bsz, nch, S5_NLB * S5_K // 2), F32)
    return pl.pallas_call(
        functools.partial(_s5_local_kernel, nc=nc),
        grid=(S5_NLB, bsz, nch // nc),
        in_specs=[pl.BlockSpec((1, nc * S5_CHUNK, 128), lambda lb, b, i: (b, i, lb)),
                  pl.BlockSpec((1, S5_K, S5_K), lambda lb, b, i: (lb, 0, 0))],
        out_specs=[out(), out()],
        out_shape=[shape, shape],
        compiler_params=_params("arbitrary", "arbitrary", "arbitrary"),
        name="s5_local",
    )(u, ends)


def _swap_re_im(x):
    half = S5_K // 4
    parts = []
    for lb in range(x.shape[1] // (2 * half)):
        parts += [x[:, (2 * lb + 1) * half:(2 * lb + 2) * half], x[:, 2 * lb * half:(2 * lb + 1) * half]]
    return jnp.concatenate(parts, axis=1)


def _s5_scan_kernel(fs_ref, fc_ref, bs_ref, bc_ref, ef_ref, eb_ref, xf_ref, xb_ref, st_ref, *, ct):
    j = pl.program_id(1)

    @pl.when(j == 0)
    def _():
        st_ref[...] = jnp.zeros_like(st_ref)

    width = S5_K
    for part in range(st_ref.shape[2] // width):
        lanes = slice(part * width, (part + 1) * width)
        f_same, f_cross = fs_ref[:, lanes], fc_ref[:, lanes]
        b_same, b_cross = bs_ref[:, lanes], bc_ref[:, lanes]

        def body(c, carry):
            sf, sb = carry
            xf_ref[0, pl.ds(c, 1), lanes] = sf
            sf = f_same * sf + f_cross * _swap_re_im(sf) + ef_ref[0, pl.ds(c, 1), lanes]
            cb = ct - 1 - c
            xb_ref[0, pl.ds(cb, 1), lanes] = sb
            sb = b_same * sb + b_cross * _swap_re_im(sb) + eb_ref[0, pl.ds(cb, 1), lanes]
            return sf, sb

        sf, sb = lax.fori_loop(0, ct, body, (st_ref[0, :, lanes], st_ref[1, :, lanes]))
        st_ref[0, :, lanes] = sf
        st_ref[1, :, lanes] = sb


def _s5_scan(ef, eb, dec):
    bsz, nch, width = ef.shape
    ct = _largest_divisor(nch, 208, 8)
    nj = nch // ct
    fwd = lambda: pl.BlockSpec((1, ct, width), lambda b, j: (b, j, 0))
    bwd = lambda: pl.BlockSpec((1, ct, width), lambda b, j: (b, nj - 1 - j, 0))
    row = lambda: pl.BlockSpec((1, width), lambda b, j: (0, 0))
    shape = jax.ShapeDtypeStruct(ef.shape, F32)
    return pl.pallas_call(
        functools.partial(_s5_scan_kernel, ct=ct),
        grid=(bsz, nj),
        in_specs=[row(), row(), row(), row(), fwd(), bwd()],
        out_specs=[fwd(), bwd()],
        out_shape=[shape, shape],
        scratch_shapes=[pltpu.VMEM((2, 1, width), F32)],
        compiler_params=_params("parallel", "arbitrary"),
        name="s5_scan",
    )(*dec, ef, eb)


def _s5_out_kernel(u_ref, xf_ref, xb_ref, toe_ref, outs_ref, y_ref, *, nc):
    states = jnp.concatenate([xf_ref[0], xb_ref[0]], axis=1).astype(BF16)
    y = jnp.dot(_chunk_rows(u_ref, nc), toe_ref[0], preferred_element_type=F32)
    y += jnp.dot(states, outs_ref[0], preferred_element_type=F32)
    for t in range(S5_CHUNK):
        y_ref[0, pl.ds(t, nc, stride=S5_CHUNK), :] = y[:, t * 128:(t + 1) * 128]


def _s5_out(u, xf, xb, toe, outs, nc):
    bsz, p, _ = u.shape
    nch = p // S5_CHUNK
    tok = lambda: pl.BlockSpec((1, nc * S5_CHUNK, 128), lambda lb, b, i: (b, i, lb))
    state = lambda: pl.BlockSpec((1, nc, S5_K // 2), lambda lb, b, i: (b, i, lb))
    weight = lambda: pl.BlockSpec((1, S5_K, S5_K), lambda lb, b, i: (lb, 0, 0))
    return pl.pallas_call(
        functools.partial(_s5_out_kernel, nc=nc),
        grid=(S5_NLB, bsz, nch // nc),
        in_specs=[tok(), state(), state(), weight(), weight()],
        out_specs=tok(),
        out_shape=jax.ShapeDtypeStruct(u.shape, F32),
        compiler_params=_params("arbitrary", "arbitrary", "arbitrary"),
        name="s5_out",
    )(u, xf, xb, toe, outs)


def _s5_mixer(u, toe, ends, outs, dec):
    nch = u.shape[1] // S5_CHUNK
    nc = _largest_divisor(nch, 260, 8)
    ef, eb = _s5_local(u, ends, nc)
    xf, xb = _s5_scan(ef, eb, dec)
    return _s5_out(u, xf, xb, toe, outs, nc)


def _hg_kernel(lb_ref, qf_ref, ff_ref, vf_ref, qb_ref, fb_ref, vb_ref, of_ref, ob_ref, st_ref):
    i = pl.program_id(1)
    n = pl.num_programs(1)

    @pl.when(i == 0)
    def _():
        st_ref[...] = jnp.zeros_like(st_ref)

    lb = lb_ref[...]
    n_chunks = TILE // HG_CHUNK
    shift = HG_CHUNK.bit_length() - 1
    r = lax.broadcasted_iota(jnp.int32, (TILE, TILE), 0)
    c = lax.broadcasted_iota(jnp.int32, (TILE, TILE), 1)
    same_chunk = jnp.right_shift(r, shift) == jnp.right_shift(c, shift)
    r64 = lax.broadcasted_iota(jnp.int32, (HG_CHUNK, HG_CHUNK), 0)
    c64 = lax.broadcasted_iota(jnp.int32, (HG_CHUNK, HG_CHUNK), 1)
    row = lax.broadcasted_iota(jnp.int32, (TILE, 1), 0)
    passes = (
        (0, qf_ref, ff_ref, vf_ref, of_ref, i, c <= r, c64 <= r64, range(n_chunks), HG_CHUNK - 1),
        (1, qb_ref, fb_ref, vb_ref, ob_ref, n - 1 - i, c >= r, c64 >= r64, reversed(range(n_chunks)), 0),
    )
    for d, q_ref, f_ref, v_ref, o_ref, tile, before, keep, chunks, tot_row in passes:
        q_pre = q_ref[0]
        q = q_pre * jax.nn.sigmoid(q_pre)
        g = lb + (1.0 - lb) * jax.nn.sigmoid(f_ref[0])
        valid = tile * TILE + row >= FRONT
        log_f = jnp.where(valid, jnp.log(g), 0.0)
        k = jnp.where(valid, 1.0 - g, 0.0)
        cum = jnp.logical_and(same_chunk, before).astype(BF16)
        b, rest = None, log_f
        for _ in range(3):
            piece = rest.astype(BF16)
            rest = rest - piece.astype(F32)
            part = jnp.dot(cum, piece, preferred_element_type=F32)
            b = part if b is None else b + part
        totals = [b[ch * HG_CHUNK + tot_row:ch * HG_CHUNK + tot_row + 1] for ch in range(n_chunks)]
        b_tot = jnp.concatenate([jnp.broadcast_to(t, (HG_CHUNK, HG_WIDTH)) for t in totals], axis=0)
        q_dec = (q * jnp.exp(b)).astype(BF16)
        k_inv = (k * jnp.exp(-b)).astype(BF16)
        k_end = (k * jnp.exp(b_tot - b)).astype(BF16)
        v = v_ref[0].astype(BF16)
        states = [st_ref[d, h] for h in range(HG_HEADS)]
        for ch in chunks:
            rows = slice(ch * HG_CHUNK, (ch + 1) * HG_CHUNK)
            decay = jnp.exp(totals[ch])
            for h in range(HG_HEADS):
                ls = slice(h * HG_DIM, (h + 1) * HG_DIM)
                scores = lax.dot_general(q_dec[rows, ls], k_inv[rows, ls], _NT, preferred_element_type=F32)
                scores = jnp.where(keep, scores, 0.0).astype(BF16)
                o = jnp.dot(scores, v[rows, ls], preferred_element_type=F32)
                o += lax.dot_general(q_dec[rows, ls], states[h].astype(BF16), _NT, preferred_element_type=F32)
                kv = lax.dot_general(v[rows, ls], k_end[rows, ls], _TN, preferred_element_type=F32)
                states[h] = decay[:, ls] * states[h] + kv
                o_ref[0, rows, ls] = o
        for h in range(HG_HEADS):
            st_ref[d, h] = states[h]


def _hgrn2(q_pre, f_fwd, f_bwd, i_in, lb):
    bsz, p, _ = q_pre.shape
    nt = p // TILE
    fwd = lambda: pl.BlockSpec((1, TILE, HG_WIDTH), lambda b, i: (b, i, 0))
    bwd = lambda: pl.BlockSpec((1, TILE, HG_WIDTH), lambda b, i: (b, nt - 1 - i, 0))
    shape = jax.ShapeDtypeStruct((bsz, p, HG_WIDTH), F32)
    return pl.pallas_call(
        _hg_kernel,
        grid=(bsz, nt),
        in_specs=[pl.BlockSpec((1, HG_WIDTH), lambda b, i: (0, 0)), fwd(), fwd(), fwd(), bwd(), bwd(), bwd()],
        out_specs=[fwd(), bwd()],
        out_shape=[shape, shape],
        scratch_shapes=[pltpu.VMEM((2, HG_HEADS, HG_DIM, HG_DIM), F32)],
        compiler_params=_params("parallel", "arbitrary"),
        name="hgrn2",
    )(lb, q_pre, f_fwd, i_in, q_pre, f_bwd, i_in)


def _gelu(x):
    return 0.5 * x * (1.0 + lax.erf(x * (1.0 / math.sqrt(2.0))))


def _merge_kernel(xp_ref, xs_ref, metap_ref, ys_ref, of_ref, ob_ref, gate_ref, wglu_ref, hgn_ref, whg_ref,
                  wout_ref, n2_ref, hs_ref, h2_ref, *, n_prompt):
    hs = _hidden_tile(xp_ref, xs_ref, metap_ref, n_prompt)
    glu = jnp.dot(_gelu(ys_ref[0]).astype(BF16), wglu_ref[...], preferred_element_type=F32)
    y_a = glu[:, :D_MODEL] * jax.nn.sigmoid(glu[:, D_MODEL:])
    o = of_ref[0] + ob_ref[0]
    normed = []
    for h in range(HG_HEADS):
        oh = o[:, h * HG_DIM:(h + 1) * HG_DIM]
        normed.append(oh * lax.rsqrt(jnp.mean(oh * oh, axis=-1, keepdims=True) + EPS))
    o_gate = gate_ref[0, :, 0:HG_WIDTH]
    y_hg = jnp.concatenate(normed, axis=-1) * hgn_ref[...] * (o_gate * jax.nn.sigmoid(o_gate))
    y_b = jnp.dot(y_hg.astype(BF16), whg_ref[...], preferred_element_type=F32)
    gate_a = gate_ref[0, :, HG_WIDTH:HG_WIDTH + D_MODEL]
    gate_b = gate_ref[0, :, HG_WIDTH + D_MODEL:]
    mixed = jax.nn.sigmoid(gate_a) * y_a + jax.nn.sigmoid(gate_b) * y_b
    hs = hs + jnp.dot(mixed.astype(BF16), wout_ref[...], preferred_element_type=F32)
    hs_ref[0] = hs
    ms = jnp.mean(hs * hs, axis=-1, keepdims=True)
    h2_ref[0] = _pack(hs * lax.rsqrt(ms + EPS) * n2_ref[...])


def _merge(x_prompt, x_sample, metap, y_s5, o_f, o_b, gates, w_glu, hg_norm_g, w_hg_out, w_out, norm2_g):
    bsz, p, _ = y_s5.shape
    nt = p // TILE
    n_prompt = x_prompt.shape[0]
    tok = lambda width: pl.BlockSpec((1, TILE, width), lambda b, i: (b, i, 0))
    full = lambda a: pl.BlockSpec(a.shape, lambda b, i: (0,) * a.ndim)
    return pl.pallas_call(
        functools.partial(_merge_kernel, n_prompt=n_prompt),
        grid=(bsz, nt),
        in_specs=_x_specs(n_prompt, nt) + [
            full(metap), tok(S5_WIDTH), tok(HG_WIDTH), tok(HG_WIDTH), tok(gates.shape[-1]),
            full(w_glu), full(hg_norm_g), full(w_hg_out), full(w_out), full(norm2_g),
        ],
        out_specs=[tok(D_MODEL), pl.BlockSpec((1, TILE // 2, D_MODEL), lambda b, i: (b, i, 0))],
        out_shape=[jax.ShapeDtypeStruct((bsz, p, D_MODEL), F32),
                   jax.ShapeDtypeStruct((bsz, p // 2, D_MODEL), jnp.uint32)],
        compiler_params=_params("arbitrary", "arbitrary"),
        name="merge",
    )(x_prompt, x_sample, metap, y_s5, o_f, o_b, gates, w_glu, hg_norm_g, w_hg_out, w_out, norm2_g)


def _staircase():
    return [(i, PEER_TOPK // (i + 1)) for i in range(PEER_TOPK)]


def _exchange(v, i, l):
    v[i], v[l] = jnp.maximum(v[i], v[l]), jnp.minimum(v[i], v[l])


def _bitonic_merge_desc(v):
    j = len(v) // 2
    while j >= 1:
        for i in range(len(v)):
            if i ^ j > i:
                _exchange(v, i, i ^ j)
        j //= 2


def _top16(s):
    v = [s[SUBLANES * i:SUBLANES * (i + 1)] for i in range(PEER_KEYS // SUBLANES)]
    k = 2
    while k <= len(v):
        j = k // 2
        while j >= 1:
            for i in range(len(v)):
                l = i ^ j
                if l > i:
                    if (i & k) == 0 or k == len(v):
                        _exchange(v, i, l)
                    else:
                        _exchange(v, l, i)
            j //= 2
        k *= 2
    for shift in (4, 2, 1):
        other = [pltpu.roll(a, shift, 0) for a in v]
        v = [jnp.maximum(a, other[len(v) - 1 - i]) for i, a in enumerate(v)]
        _bitonic_merge_desc(v)
    return [a[0:1] for a in v]


def _spread_by_rank(s, tops, values, default):
    out = jnp.full(s.shape, default, F32)
    for t, v in zip(tops, values):
        out = jnp.where(s == t, v, out)
    return out


def _score_kernel(h2_ref, wq_ref, keys_ref, r2_ref, p2_ref, cnt_ref, p1_ref, cands_ref):
    h2 = _unpack(h2_ref[...])
    for h in range(PEER_HEADS):
        cand_ref = cands_ref.at[h]
        qh = jnp.dot(h2, wq_ref[h], preferred_element_type=F32).astype(BF16)
        s1_all = lax.dot_general(keys_ref[h, 0], qh[:, :PEER_HALF], _NT, preferred_element_type=F32)
        s2_all = lax.dot_general(keys_ref[h, 1], qh[:, PEER_HALF:], _NT, preferred_element_type=F32)
        for part in range(TILE // 128):
            lanes = slice(part * 128, (part + 1) * 128)
            s1, s2 = s1_all[:, lanes], s2_all[:, lanes]
            t1 = _top16(s1)
            t2 = _top16(s2)
            t2_all = jnp.concatenate(t2, axis=0)
            cand_ref[:, lanes] = jnp.full((PEER_NCAND, 128), -jnp.inf, F32)
            off = 0
            for i, n_i in _staircase():
                cand_ref[off:off + n_i, lanes] = t1[i] + t2_all[0:n_i]
                off += n_i
            cand = cand_ref[:, lanes]
            c_max = t1[0] + t2[0]
            work = cand
            z = jnp.zeros_like(c_max)
            tau = c_max
            for it in range(PEER_TOPK):
                tau = jnp.max(work, axis=0, keepdims=True)
                z = z + jnp.exp(tau - c_max)
                work = jnp.where(work == tau, -jnp.inf, work)
            chosen = (cand >= tau).astype(F32)
            counts, off = [], 0
            for i, n_i in _staircase():
                counts.append(jnp.sum(chosen[off:off + n_i], axis=0, keepdims=True))
                off += n_i
            r2_ref[h, 0, :, lanes] = _pack(_spread_by_rank(s2, t2, [float(i) for i in range(PEER_TOPK)],
                                                           float(PEER_TOPK)))
            p2_ref[h, 0, :, lanes] = _pack(jnp.exp(s2 - t2[0]))
            cnt_ref[h, 0, :, lanes] = _spread_by_rank(s1, t1, counts, 0.0)
            p1_ref[h, 0, :, lanes] = 0.5 * jnp.exp(s1 - t1[0]) / z


def _peer_scores(h2, w_q, keys):
    ntok = 2 * h2.shape[0]
    blk = lambda rows: pl.BlockSpec((PEER_HEADS, 1, rows, TILE), lambda i: (0, i, 0, 0))
    packed = jax.ShapeDtypeStruct((PEER_HEADS, ntok // TILE, PEER_KEYS // 2, TILE), jnp.uint32)
    plain = jax.ShapeDtypeStruct((PEER_HEADS, ntok // TILE, PEER_KEYS, TILE), F32)
    return pl.pallas_call(
        _score_kernel,
        grid=(ntok // TILE,),
        in_specs=[
            pl.BlockSpec((TILE // 2, D_MODEL), lambda i: (i, 0)),
            pl.BlockSpec(w_q.shape, lambda i: (0, 0, 0)),
            pl.BlockSpec(keys.shape, lambda i: (0, 0, 0, 0)),
        ],
        out_specs=[blk(PEER_KEYS // 2), blk(PEER_KEYS // 2), blk(PEER_KEYS), blk(PEER_KEYS)],
        out_shape=[packed, packed, plain, plain],
        scratch_shapes=[pltpu.VMEM((PEER_HEADS, PEER_NCAND, TILE), F32)],
        compiler_params=_params("parallel"),
        name="peer_scores",
    )(h2, w_q, keys)


def _pack_tables_kernel(u_ref, v_ref, up_ref, vtp_ref):
    up_ref[...] = _pack(u_ref[...])
    vtp_ref[...] = _pack(v_ref[...].T)


def _pack_tables(u_tab, v_tab):
    nexp, d = u_tab.shape
    te = _largest_divisor(nexp, 512, 128)
    tile = lambda: pl.BlockSpec((te, d), lambda e: (e, 0))
    return pl.pallas_call(
        _pack_tables_kernel,
        grid=(nexp // te,),
        in_specs=[tile(), tile()],
        out_specs=[pl.BlockSpec((te // 2, d), lambda e: (e, 0)), pl.BlockSpec((d // 2, te), lambda e: (0, e))],
        out_shape=[jax.ShapeDtypeStruct((nexp // 2, d), jnp.uint32), jax.ShapeDtypeStruct((d // 2, nexp), jnp.uint32)],
        compiler_params=_params("parallel"),
        name="pack_tables",
    )(u_tab.astype(F32), v_tab.astype(F32))


def _dense_kernel(h2_ref, u_ref, vt_ref, r2_ref, p2_ref, cnt_ref, p1_ref, o_ref, act0_ref, act1_ref, wa0_ref,
                  wa1_ref, *, n_sub, n_etiles):
    s = pl.program_id(1)

    @pl.when(s == 0)
    def _():
        o_ref[...] = jnp.zeros_like(o_ref)

    n_first = PEER_ETILE // PEER_KEYS
    zero = jnp.zeros((), BF16)
    spread = lambda row: jnp.broadcast_to(row, (PEER_KEYS, 128)).astype(BF16)

    def sub_tile(c, carry, *, stages, act_a, act_b, wa_b, wa_c):
        if "A" in stages:
            h2 = _unpack(h2_ref[pl.ds(pl.multiple_of(c * (TILE // 2), TILE // 2), TILE // 2), :])
            act_a[c] = lax.dot_general(_unpack(u_ref[...]), h2, _NT, preferred_element_type=F32)
        for lc in range(TILE // 128 if "B" in stages else 0):
            lanes = slice(lc * 128, (lc + 1) * 128)
            for j in range(0, n_first, 2):
                w0 = jnp.zeros((PEER_KEYS, 128), BF16)
                w1 = jnp.zeros((PEER_KEYS, 128), BF16)
                for h in range(PEER_HEADS):
                    r2 = _unpack(r2_ref[h, c, :, lanes])
                    p2 = _unpack(p2_ref[h, c, :, lanes])
                    cnt = cnt_ref[h, c, :, lanes]
                    p1 = p1_ref[h, c, :, lanes]
                    w0 += jnp.where(r2 < spread(cnt[j:j + 1]), p2, zero) * spread(p1[j:j + 1])
                    w1 += jnp.where(r2 < spread(cnt[j + 1:j + 2]), p2, zero) * spread(p1[j + 1:j + 2])
                for k, w in ((j, w0), (j + 1, w1)):
                    rows = slice(k * PEER_KEYS, (k + 1) * PEER_KEYS)
                    act = act_b[c, rows, lanes]
                    twice_gelu = act * (1.0 + lax.erf(act * (1.0 / math.sqrt(2.0))))
                    wa_b[c, rows, lanes] = w * twice_gelu.astype(BF16)
        if "C" in stages:
            o_ref[c] += jnp.dot(_unpack(vt_ref[...]), wa_c[c], preferred_element_type=F32)
        return carry

    def run(condition, stages, parity):
        bufs = (dict(act_a=act0_ref, act_b=act1_ref, wa_b=wa1_ref, wa_c=wa0_ref) if parity == 0 else
                dict(act_a=act1_ref, act_b=act0_ref, wa_b=wa0_ref, wa_c=wa1_ref))

        @pl.when(condition)
        def _():
            lax.fori_loop(0, n_sub, functools.partial(sub_tile, stages=stages, **bufs), 0)

    steady = jnp.logical_and(s >= 2, s < n_etiles)
    run(s == 0, "A", 0)
    run(s == 1, "AB", 1)
    run(jnp.logical_and(steady, lax.rem(s, 2) == 0), "ABC", 0)
    run(jnp.logical_and(steady, lax.rem(s, 2) == 1), "ABC", 1)
    run(s == n_etiles, "BC", n_etiles % 2)
    run(s == n_etiles + 1, "C", (n_etiles + 1) % 2)


def _peer_dense(h2, u_tab, vt_tab, r2, p2, cnt, p1):
    ntok = 2 * h2.shape[0]
    n_sub = _largest_divisor(ntok // TILE, PEER_TTILE // TILE)
    tn = n_sub * TILE
    n_etiles = 2 * u_tab.shape[0] // PEER_ETILE
    last = n_etiles - 1
    stage = lambda s, lag: jnp.clip(s - lag, 0, last)
    second = lambda: pl.BlockSpec((PEER_HEADS, n_sub, PEER_KEYS // 2, TILE), lambda t, s: (0, t, 0, 0))
    first = lambda: pl.BlockSpec((PEER_HEADS, n_sub, PEER_ETILE // PEER_KEYS, TILE),
                                 lambda t, s: (0, t, stage(s, 1), 0))
    return pl.pallas_call(
        functools.partial(_dense_kernel, n_sub=n_sub, n_etiles=n_etiles),
        grid=(ntok // tn, n_etiles + 2),
        in_specs=[pl.BlockSpec((tn // 2, D_MODEL), lambda t, s: (t, 0)),
                  pl.BlockSpec((PEER_ETILE // 2, D_MODEL), lambda t, s: (stage(s, 0), 0)),
                  pl.BlockSpec((D_MODEL // 2, PEER_ETILE), lambda t, s: (0, stage(s, 2))),
                  second(), second(), first(), first()],
        out_specs=pl.BlockSpec((n_sub, D_MODEL, TILE), lambda t, s: (t, 0, 0)),
        out_shape=jax.ShapeDtypeStruct((ntok // TILE, D_MODEL, TILE), F32),
        scratch_shapes=[pltpu.VMEM((n_sub, PEER_ETILE, TILE), F32), pltpu.VMEM((n_sub, PEER_ETILE, TILE), F32),
                        pltpu.VMEM((n_sub, PEER_ETILE, TILE), BF16), pltpu.VMEM((n_sub, PEER_ETILE, TILE), BF16)],
        compiler_params=_params("parallel", "arbitrary"),
        name="peer_dense",
    )(h2, u_tab, vt_tab, r2, p2, cnt, p1)


def _final_kernel(ot_ref, hs_ref, g_ref, y_ref):
    hs = hs_ref[...] + ot_ref[0].T
    ms = jnp.mean(hs * hs, axis=-1, keepdims=True)
    y_ref[0] = hs * lax.rsqrt(ms + EPS) * g_ref[...]


def _final(o_t, hs, final_g, first_seq, n_seq, nt):
    tile_of = lambda b, i: (b + first_seq) * nt + i + 1
    return pl.pallas_call(
        _final_kernel,
        grid=(n_seq, nt - 1),
        in_specs=[
            pl.BlockSpec((1, D_MODEL, TILE), lambda b, i: (tile_of(b, i), 0, 0)),
            pl.BlockSpec((TILE, D_MODEL), lambda b, i: (tile_of(b, i), 0)),
            pl.BlockSpec((1, D_MODEL), lambda b, i: (0, 0)),
        ],
        out_specs=pl.BlockSpec((1, TILE, D_MODEL), lambda b, i: (b, i, 0)),
        out_shape=jax.ShapeDtypeStruct((n_seq, (nt - 1) * TILE, D_MODEL), F32),
        compiler_params=_params("parallel", "parallel"),
        name="final",
    )(o_t, hs, final_g)


def kernel(x_prompt, x_sample, meta, norm1_g, w_in, s5_lam_re, s5_lam_im, s5_log_step, s5_b_re, s5_b_im,
           s5_c_re, s5_c_im, s5_d, w_glu, hg_lb, hg_norm_g, w_hg_out, w_out, norm2_g, peer_wq, peer_keys,
           peer_u, peer_v, final_g):
    assert x_prompt.shape[1] == x_sample.shape[1] and x_prompt.shape[1] % TILE == 0
    assert norm1_g.shape[0] == 1, "single-layer trunk"
    n_prompt = x_prompt.shape[0]
    x_prompt, x_sample = x_prompt.astype(F32), x_sample.astype(F32)
    metap = jnp.concatenate([jnp.zeros((FRONT, D_MODEL), F32), meta.astype(F32)], axis=0)
    row = lambda a: a.astype(F32).reshape(1, -1)

    u, q_pre, f_fwd, f_bwd, i_in, gates = _inproj(x_prompt, x_sample, metap, row(norm1_g[0]),
                                                  w_in[0].astype(BF16))

    y_s5 = _s5_mixer(u, *_s5_weights(s5_lam_re[0], s5_lam_im[0], s5_log_step[0], s5_b_re[0], s5_b_im[0],
                                     s5_c_re[0], s5_c_im[0], s5_d[0]))

    lb = jax.nn.softmax(hg_lb.astype(F32), axis=0)[0]
    o_f, o_b = _hgrn2(q_pre, f_fwd, f_bwd, i_in, row(lb))

    hs, h2 = _merge(x_prompt, x_sample, metap, y_s5, o_f, o_b, gates, w_glu[0].astype(BF16), row(hg_norm_g[0]),
                    w_hg_out[0].astype(BF16), w_out[0].astype(BF16), row(norm2_g[0]))

    bsz, p, _ = hs.shape
    hs = hs.reshape(bsz * p, D_MODEL)
    h2 = h2.reshape(bsz * p // 2, D_MODEL)
    w_q = peer_wq[0].astype(BF16).reshape(D_MODEL, PEER_HEADS, PEER_QDIM).transpose(1, 0, 2)
    r2, p2, cnt, p1 = _peer_scores(h2, w_q, peer_keys[0].astype(BF16))
    u_tab, vt_tab = _pack_tables(peer_u[0], peer_v[0])
    o_t = _peer_dense(h2, u_tab, vt_tab, r2, p2, cnt, p1)
    fin = functools.partial(_final, o_t, hs, row(final_g), nt=p // TILE)
    return (fin(first_seq=0, n_seq=n_prompt), fin(first_seq=n_prompt, n_seq=bsz - n_prompt))
```

```python
import functools
import math

import jax
import jax.numpy as jnp
from jax import lax
from jax.experimental import pallas as pl
from jax.experimental.pallas import tpu as pltpu

F32 = jnp.float32
BF16 = jnp.bfloat16

D_MODEL = 1024
N_META = 16
EPS = 1e-6
TILE = 256
FRONT = TILE - N_META

S5_WIDTH = 512
S5_GROUP = 16
S5_GROUPS = 32
S5_STATE = 64
S5_CHUNK = 16
LANES = 128
SUBLANES = 8
S5_LB = LANES // S5_GROUP
S5_NLB = S5_GROUPS // S5_LB
S5_K = S5_CHUNK * LANES

HG_WIDTH = 512
HG_HEADS = 4
HG_DIM = 128
HG_CHUNK = 64

PEER_HEADS = 8
PEER_KEYS = 128
PEER_TOPK = 16
PEER_QDIM = 256
PEER_HALF = 128
PEER_ETILE = 1024
PEER_TTILE = 1280
PEER_NCAND = 56

VMEM_LIMIT = 56 * 1024 * 1024

_NT = (((1,), (1,)), ((), ()))
_TN = (((0,), (0,)), ((), ()))


def _largest_divisor(n, target, multiple=1):
    best = None
    for d in range(multiple, min(n, target) + 1, multiple):
        if n % d == 0:
            best = d
    assert best is not None, (n, target, multiple)
    return best


def _pack(x):
    return pltpu.bitcast(x.astype(BF16), jnp.uint32)


def _unpack(x):
    return pltpu.bitcast(x, BF16)


def _params(*sem):
    return pltpu.CompilerParams(dimension_semantics=sem, vmem_limit_bytes=VMEM_LIMIT)


def _hidden_tile(xp_ref, xs_ref, metap_ref, n_prompt):
    b, i = pl.program_id(0), pl.program_id(1)
    x = jnp.where(b < n_prompt, xp_ref[0], xs_ref[0])
    return jnp.where(i == 0, metap_ref[...], x)


def _x_specs(n_prompt, nt):
    tile = lambda i: jnp.maximum(i - 1, 0)
    prompt = pl.BlockSpec((1, TILE, D_MODEL), lambda b, i: (
        jnp.minimum(b, n_prompt - 1), jnp.where(b < n_prompt, tile(i), nt - 2), 0))
    sample = pl.BlockSpec((1, TILE, D_MODEL), lambda b, i: (
        jnp.maximum(b - n_prompt, 0), jnp.where(b < n_prompt, 0, tile(i)), 0))
    return [prompt, sample]


def _inproj_kernel(xp_ref, xs_ref, metap_ref, g_ref, w_ref, u_ref, q_ref, ff_ref, fb_ref, v_ref, gate_ref, *,
                   n_prompt):
    i = pl.program_id(1)
    hs = _hidden_tile(xp_ref, xs_ref, metap_ref, n_prompt)
    ms = jnp.mean(hs * hs, axis=-1, keepdims=True)
    h = (hs * lax.rsqrt(ms + EPS) * g_ref[...]).astype(BF16)
    row = lax.broadcasted_iota(jnp.int32, (TILE, 1), 0)
    valid = jnp.logical_or(i > 0, row >= FRONT)
    u = jnp.dot(h, w_ref[:, 0:S5_WIDTH], preferred_element_type=F32)
    u_ref[0] = jnp.where(valid, u, 0.0)
    off = S5_WIDTH
    for ref in (q_ref, ff_ref, fb_ref, v_ref):
        ref[0] = jnp.dot(h, w_ref[:, off:off + HG_WIDTH], preferred_element_type=F32)
        off += HG_WIDTH
    gate_ref[0] = _pack(jnp.dot(h, w_ref[:, off:], preferred_element_type=F32))


def _inproj(x_prompt, x_sample, metap, g, w_in):
    n_prompt, length, _ = x_prompt.shape
    bsz = n_prompt + x_sample.shape[0]
    nt = length // TILE + 1
    p = nt * TILE
    ncols = w_in.shape[1]
    ngate = ncols - S5_WIDTH - 4 * HG_WIDTH
    tok = lambda width: pl.BlockSpec((1, TILE, width), lambda b, i: (b, i, 0))
    full = lambda shape: pl.BlockSpec(shape, lambda b, i: (0,) * len(shape))
    hg_shape = jax.ShapeDtypeStruct((bsz, p, HG_WIDTH), F32)
    return pl.pallas_call(
        functools.partial(_inproj_kernel, n_prompt=n_prompt),
        grid=(bsz, nt),
        in_specs=_x_specs(n_prompt, nt) + [full((TILE, D_MODEL)), full((1, D_MODEL)), full((D_MODEL, ncols))],
        out_specs=[tok(S5_WIDTH), tok(HG_WIDTH), tok(HG_WIDTH), tok(HG_WIDTH), tok(HG_WIDTH),
                   pl.BlockSpec((1, TILE // 2, ngate), lambda b, i: (b, i, 0))],
        out_shape=[hg_shape, hg_shape, hg_shape, hg_shape, hg_shape,
                   jax.ShapeDtypeStruct((bsz, p // 2, ngate), jnp.uint32)],
        compiler_params=_params("arbitrary", "arbitrary"),
        name="inproj",
    )(x_prompt, x_sample, metap, g, w_in)


def _s5_weights(lam_re, lam_im, log_step, b_re, b_im, c_re, c_im, d_skip):
    hi = lax.Precision.HIGHEST
    f = lambda a: a.astype(F32)
    lam_re, lam_im, log_step = f(lam_re), f(lam_im), f(log_step)
    b_re, b_im, c_re, c_im, d_skip = f(b_re), f(b_im), f(c_re), f(c_im), f(d_skip)
    step = jnp.exp(log_step)[:, :, None]
    pw = jnp.arange(S5_CHUNK + 1, dtype=F32)[:, None, None, None]
    mag = jnp.exp(pw * lam_re[None] * step[None])
    ang = pw * lam_im[None] * step[None]
    a_re, a_im = mag * jnp.cos(ang), mag * jnp.sin(ang)
    den = lam_re * lam_re + lam_im * lam_im
    n_re, n_im = a_re[1] - 1.0, a_im[1]
    coef_re = (n_re * lam_re + n_im * lam_im) / den
    coef_im = (n_im * lam_re - n_re * lam_im) / den
    w_re = coef_re[None] * a_re - coef_im[None] * a_im
    w_im = coef_re[None] * a_im + coef_im[None] * a_re

    def conv_taps(direction):
        wr, wi = w_re[:S5_CHUNK, direction], w_im[:S5_CHUNK, direction]
        m_re = wr[..., None] * b_re[None] - wi[..., None] * b_im[None]
        m_im = wr[..., None] * b_im[None] + wi[..., None] * b_re[None]
        return (jnp.einsum('gon,dgni->dgoi', c_re, m_re, precision=hi)
                - jnp.einsum('gon,dgni->dgoi', c_im, m_im, precision=hi))

    k_f, k_b = conv_taps(0), conv_taps(1)
    center = k_f[0] + k_b[0] + jnp.eye(S5_GROUP, dtype=F32)[None] * d_skip.reshape(S5_GROUPS, S5_GROUP, 1)
    taps = jnp.concatenate([k_b[:0:-1], center[None], k_f[1:]], axis=0)
    t_idx = jnp.arange(S5_CHUNK)
    toe = taps[t_idx[None, :] - t_idx[:, None] + S5_CHUNK - 1]
    split = lambda a, axis: a.reshape(a.shape[:axis] + (S5_NLB, S5_LB) + a.shape[axis + 1:])

    def block_diag(src):
        src = src.astype(BF16)
        width = src.shape[-1]
        lane_pad = lambda g: [(0, 0, 0)] * 4 + [(g * width, (S5_LB - 1 - g) * width, 0)]
        parts = [lax.pad(src[:, g], jnp.zeros((), BF16), lane_pad(g)) for g in range(S5_LB)]
        return jnp.stack(parts, axis=2).reshape(S5_NLB, S5_K, S5_K)

    toe = block_diag(split(toe.transpose(2, 0, 4, 1, 3), 0))

    def end_map(direction, powers):
        pr, pi = a_re[powers, direction], a_im[powers, direction]
        return [pr[..., None] * b_re[None] - pi[..., None] * b_im[None],
                pr[..., None] * b_im[None] + pi[..., None] * b_re[None]]

    ends = jnp.stack(end_map(0, S5_CHUNK - 1 - t_idx) + end_map(1, t_idx))
    ends = block_diag(split(ends.transpose(2, 1, 4, 0, 3), 0))

    def out_map(direction, powers):
        wr, wi = w_re[powers, direction], w_im[powers, direction]
        return [c_re[None] * wr[:, :, None, :] - c_im[None] * wi[:, :, None, :],
                -c_re[None] * wi[:, :, None, :] - c_im[None] * wr[:, :, None, :]]

    outs = jnp.stack(out_map(0, t_idx + 1) + out_map(1, S5_CHUNK - t_idx))
    outs = block_diag(split(outs.transpose(2, 0, 4, 1, 3), 0))

    def decay(direction):
        ar = split(a_re[S5_CHUNK, direction], 0).reshape(S5_NLB, 1, S5_K // 4)
        ai = split(a_im[S5_CHUNK, direction], 0).reshape(S5_NLB, 1, S5_K // 4)
        same = jnp.concatenate([ar, ar], axis=2).reshape(1, S5_NLB * S5_K // 2)
        cross = jnp.concatenate([-ai, ai], axis=2).reshape(1, S5_NLB * S5_K // 2)
        return same, cross

    return toe.astype(BF16), ends.astype(BF16), outs.astype(BF16), decay(0) + decay(1)


def _chunk_rows(u_ref, nc):
    parts = [u_ref[0, pl.ds(s, nc, stride=S5_CHUNK), :] for s in range(S5_CHUNK)]
    return jnp.concatenate(parts, axis=1).astype(BF16)


def _s5_local_kernel(u_ref, w_ref, ef_ref, eb_ref, *, nc):
    e = jnp.dot(_chunk_rows(u_ref, nc), w_ref[0], preferred_element_type=F32)
    ef_ref[0] = e[:, :S5_K // 2]
    eb_ref[0] = e[:, S5_K // 2:]


def _s5_local(u, ends, nc):
    bsz, p, _ = u.shape
    nch = p // S5_CHUNK
    out = lambda: pl.BlockSpec((1, nc, S5_K // 2), lambda lb, b, i: (b, i, lb))
    shape = jax.ShapeDtypeStruct((bsz, nch, S5_NLB * S5_K // 2), F32)
    return pl.pallas_call(
        functools.partial(_s5_local_kernel, nc=nc),
        grid=(S5_NLB, bsz, nch // nc),
        in_specs=[pl.BlockSpec((1, nc * S5_CHUNK, 128), lambda lb, b, i: (b, i, lb)),
                  pl.BlockSpec((1, S5_K, S5_K), lambda lb, b, i: (lb, 0, 0))],
        out_specs=[out(), out()],
        out_shape=[shape, shape],
        compiler_params=_params("arbitrary", "arbitrary", "arbitrary"),
        name="s5_local",
    )(u, ends)


def _swap_re_im(x):
    half = S5_K // 4
    parts = []
    for lb in range(x.shape[1] // (2 * half)):
        parts += [x[:, (2 * lb + 1) * half:(2 * lb + 2) * half], x[:, 2 * lb * half:(2 * lb + 1) * half]]
    return jnp.concatenate(parts, axis=1)


def _s5_scan_kernel(fs_ref, fc_ref, bs_ref, bc_ref, ef_ref, eb_ref, xf_ref, xb_ref, st_ref, *, ct):
    j = pl.program_id(1)

    @pl.when(j == 0)
    def _():
        st_ref[...] = jnp.zeros_like(st_ref)

    width = S5_K
    for part in range(st_ref.shape[2] // width):
        lanes = slice(part * width, (part + 1) * width)
        f_same, f_cross = fs_ref[:, lanes], fc_ref[:, lanes]
        b_same, b_cross = bs_ref[:, lanes], bc_ref[:, lanes]

        def body(c, carry):
            sf, sb = carry
            xf_ref[0, pl.ds(c, 1), lanes] = sf
            sf = f_same * sf + f_cross * _swap_re_im(sf) + ef_ref[0, pl.ds(c, 1), lanes]
            cb = ct - 1 - c
            xb_ref[0, pl.ds(cb, 1), lanes] = sb
            sb = b_same * sb + b_cross * _swap_re_im(sb) + eb_ref[0, pl.ds(cb, 1), lanes]
            return sf, sb

        sf, sb = lax.fori_loop(0, ct, body, (st_ref[0, :, lanes], st_ref[1, :, lanes]))
        st_ref[0, :, lanes] = sf
        st_ref[1, :, lanes] = sb


def _s5_scan(ef, eb, dec):
    bsz, nch, width = ef.shape
    ct = _largest_divisor(nch, 208, 8)
    nj = nch // ct
    fwd = lambda: pl.BlockSpec((1, ct, width), lambda b, j: (b, j, 0))
    bwd = lambda: pl.BlockSpec((1, ct, width), lambda b, j: (b, nj - 1 - j, 0))
    row = lambda: pl.BlockSpec((1, width), lambda b, j: (0, 0))
    shape = jax.ShapeDtypeStruct(ef.shape, F32)
    return pl.pallas_call(
        functools.partial(_s5_scan_kernel, ct=ct),
        grid=(bsz, nj),
        in_specs=[row(), row(), row(), row(), fwd(), bwd()],
        out_specs=[fwd(), bwd()],
        out_shape=[shape, shape],
        scratch_shapes=[pltpu.VMEM((2, 1, width), F32)],
        compiler_params=_params("parallel", "arbitrary"),
        name="s5_scan",
    )(*dec, ef, eb)


def _s5_out_kernel(u_ref, xf_ref, xb_ref, toe_ref, outs_ref, y_ref, *, nc):
    states = jnp.concatenate([xf_ref[0], xb_ref[0]], axis=1).astype(BF16)
    y = jnp.dot(_chunk_rows(u_ref, nc), toe_ref[0], preferred_element_type=F32)
    y += jnp.dot(states, outs_ref[0], preferred_element_type=F32)
    for t in range(S5_CHUNK):
        y_ref[0, pl.ds(t, nc, stride=S5_CHUNK), :] = y[:, t * 128:(t + 1) * 128]


def _s5_out(u, xf, xb, toe, outs, nc):
    bsz, p, _ = u.shape
    nch = p // S5_CHUNK
    tok = lambda: pl.BlockSpec((1, nc * S5_CHUNK, 128), lambda lb, b, i: (b, i, lb))
    state = lambda: pl.BlockSpec((1, nc, S5_K // 2), lambda lb, b, i: (b, i, lb))
    weight = lambda: pl.BlockSpec((1, S5_K, S5_K), lambda lb, b, i: (lb, 0, 0))
    return pl.pallas_call(
        functools.partial(_s5_out_kernel, nc=nc),
        grid=(S5_NLB, bsz, nch // nc),
        in_specs=[tok(), state(), state(), weight(), weight()],
        out_specs=tok(),
        out_shape=jax.ShapeDtypeStruct(u.shape, F32),
        compiler_params=_params("arbitrary", "arbitrary", "arbitrary"),
        name="s5_out",
    )(u, xf, xb, toe, outs)


def _s5_mixer(u, toe, ends, outs, dec):
    nch = u.shape[1] // S5_CHUNK
    nc = _largest_divisor(nch, 260, 8)
    ef, eb = _s5_local(u, ends, nc)
    xf, xb = _s5_scan(ef, eb, dec)
    return _s5_out(u, xf, xb, toe, outs, nc)


def _hg_kernel(lb_ref, qf_ref, ff_ref, vf_ref, qb_ref, fb_ref, vb_ref, of_ref, ob_ref, st_ref):
    i = pl.program_id(1)
    n = pl.num_programs(1)

    @pl.when(i == 0)
    def _():
        st_ref[...] = jnp.zeros_like(st_ref)

    lb = lb_ref[...]
    n_chunks = TILE // HG_CHUNK
    shift = HG_CHUNK.bit_length() - 1
    r = lax.broadcasted_iota(jnp.int32, (TILE, TILE), 0)
    c = lax.broadcasted_iota(jnp.int32, (TILE, TILE), 1)
    same_chunk = jnp.right_shift(r, shift) == jnp.right_shift(c, shift)
    r64 = lax.broadcasted_iota(jnp.int32, (HG_CHUNK, HG_CHUNK), 0)
    c64 = lax.broadcasted_iota(jnp.int32, (HG_CHUNK, HG_CHUNK), 1)
    row = lax.broadcasted_iota(jnp.int32, (TILE, 1), 0)
    passes = (
        (0, qf_ref, ff_ref, vf_ref, of_ref, i, c <= r, c64 <= r64, range(n_chunks), HG_CHUNK - 1),
        (1, qb_ref, fb_ref, vb_ref, ob_ref, n - 1 - i, c >= r, c64 >= r64, reversed(range(n_chunks)), 0),
    )
    for d, q_ref, f_ref, v_ref, o_ref, tile, before, keep, chunks, tot_row in passes:
        q_pre = q_ref[0]
        q = q_pre * jax.nn.sigmoid(q_pre)
        g = lb + (1.0 - lb) * jax.nn.sigmoid(f_ref[0])
        valid = tile * TILE + row >= FRONT
        log_f = jnp.where(valid, jnp.log(g), 0.0)
        k = jnp.where(valid, 1.0 - g, 0.0)
        cum = jnp.logical_and(same_chunk, before).astype(BF16)
        b, rest = None, log_f
        for _ in range(3):
            piece = rest.astype(BF16)
            rest = rest - piece.astype(F32)
            part = jnp.dot(cum, piece, preferred_element_type=F32)
            b = part if b is None else b + part
        totals = [b[ch * HG_CHUNK + tot_row:ch * HG_CHUNK + tot_row + 1] for ch in range(n_chunks)]
        b_tot = jnp.concatenate([jnp.broadcast_to(t, (HG_CHUNK, HG_WIDTH)) for t in totals], axis=0)
        q_dec = (q * jnp.exp(b)).astype(BF16)
        k_inv = (k * jnp.exp(-b)).astype(BF16)
        k_end = (k * jnp.exp(b_tot - b)).astype(BF16)
        v = v_ref[0].astype(BF16)
        states = [st_ref[d, h] for h in range(HG_HEADS)]
        for ch in chunks:
            rows = slice(ch * HG_CHUNK, (ch + 1) * HG_CHUNK)
            decay = jnp.exp(totals[ch])
            for h in range(HG_HEADS):
                ls = slice(h * HG_DIM, (h + 1) * HG_DIM)
                scores = lax.dot_general(q_dec[rows, ls], k_inv[rows, ls], _NT, preferred_element_type=F32)
                scores = jnp.where(keep, scores, 0.0).astype(BF16)
                o = jnp.dot(scores, v[rows, ls], preferred_element_type=F32)
                o += lax.dot_general(q_dec[rows, ls], states[h].astype(BF16), _NT, preferred_element_type=F32)
                kv = lax.dot_general(v[rows, ls], k_end[rows, ls], _TN, preferred_element_type=F32)
                states[h] = decay[:, ls] * states[h] + kv
                o_ref[0, rows, ls] = o
        for h in range(HG_HEADS):
            st_ref[d, h] = states[h]


def _hgrn2(q_pre, f_fwd, f_bwd, i_in, lb):
    bsz, p, _ = q_pre.shape
    nt = p // TILE
    fwd = lambda: pl.BlockSpec((1, TILE, HG_WIDTH), lambda b, i: (b, i, 0))
    bwd = lambda: pl.BlockSpec((1, TILE, HG_WIDTH), lambda b, i: (b, nt - 1 - i, 0))
    shape = jax.ShapeDtypeStruct((bsz, p, HG_WIDTH), F32)
    return pl.pallas_call(
        _hg_kernel,
        grid=(bsz, nt),
        in_specs=[pl.BlockSpec((1, HG_WIDTH), lambda b, i: (0, 0)), fwd(), fwd(), fwd(), bwd(), bwd(), bwd()],
        out_specs=[fwd(), bwd()],
        out_shape=[shape, shape],
        scratch_shapes=[pltpu.VMEM((2, HG_HEADS, HG_DIM, HG_DIM), F32)],
        compiler_params=_params("parallel", "arbitrary"),
        name="hgrn2",
    )(lb, q_pre, f_fwd, i_in, q_pre, f_bwd, i_in)


def _gelu(x):
    return 0.5 * x * (1.0 + lax.erf(x * (1.0 / math.sqrt(2.0))))


def _merge_kernel(xp_ref, xs_ref, metap_ref, ys_ref, of_ref, ob_ref, gate_ref, wglu_ref, hgn_ref, whg_ref,
                  wout_ref, n2_ref, hs_ref, h2_ref, *, n_prompt):
    hs = _hidden_tile(xp_ref, xs_ref, metap_ref, n_prompt)
    glu = jnp.dot(_gelu(ys_ref[0]).astype(BF16), wglu_ref[...], preferred_element_type=F32)
    y_a = glu[:, :D_MODEL] * jax.nn.sigmoid(glu[:, D_MODEL:])
    o = of_ref[0] + ob_ref[0]
    normed = []
    for h in range(HG_HEADS):
        oh = o[:, h * HG_DIM:(h + 1) * HG_DIM]
        normed.append(oh * lax.rsqrt(jnp.mean(oh * oh, axis=-1, keepdims=True) + EPS))
    unpacked = lambda cols: _unpack(gate_ref[0, :, cols]).astype(F32)
    o_gate = unpacked(slice(0, HG_WIDTH))
    y_hg = jnp.concatenate(normed, axis=-1) * hgn_ref[...] * (o_gate * jax.nn.sigmoid(o_gate))
    y_b = jnp.dot(y_hg.astype(BF16), whg_ref[...], preferred_element_type=F32)
    gate_a = unpacked(slice(HG_WIDTH, HG_WIDTH + D_MODEL))
    gate_b = unpacked(slice(HG_WIDTH + D_MODEL, None))
    mixed = jax.nn.sigmoid(gate_a) * y_a + jax.nn.sigmoid(gate_b) * y_b
    hs = hs + jnp.dot(mixed.astype(BF16), wout_ref[...], preferred_element_type=F32)
    hs_ref[0] = hs
    ms = jnp.mean(hs * hs, axis=-1, keepdims=True)
    h2_ref[0] = _pack(hs * lax.rsqrt(ms + EPS) * n2_ref[...])


def _merge(x_prompt, x_sample, metap, y_s5, o_f, o_b, gates, w_glu, hg_norm_g, w_hg_out, w_out, norm2_g):
    bsz, p, _ = y_s5.shape
    nt = p // TILE
    n_prompt = x_prompt.shape[0]
    tok = lambda width: pl.BlockSpec((1, TILE, width), lambda b, i: (b, i, 0))
    full = lambda a: pl.BlockSpec(a.shape, lambda b, i: (0,) * a.ndim)
    return pl.pallas_call(
        functools.partial(_merge_kernel, n_prompt=n_prompt),
        grid=(bsz, nt),
        in_specs=_x_specs(n_prompt, nt) + [
            full(metap), tok(S5_WIDTH), tok(HG_WIDTH), tok(HG_WIDTH),
            pl.BlockSpec((1, TILE // 2, gates.shape[-1]), lambda b, i: (b, i, 0)),
            full(w_glu), full(hg_norm_g), full(w_hg_out), full(w_out), full(norm2_g),
        ],
        out_specs=[tok(D_MODEL), pl.BlockSpec((1, TILE // 2, D_MODEL), lambda b, i: (b, i, 0))],
        out_shape=[jax.ShapeDtypeStruct((bsz, p, D_MODEL), F32),
                   jax.ShapeDtypeStruct((bsz, p // 2, D_MODEL), jnp.uint32)],
        compiler_params=_params("arbitrary", "arbitrary"),
        name="merge",
    )(x_prompt, x_sample, metap, y_s5, o_f, o_b, gates, w_glu, hg_norm_g, w_hg_out, w_out, norm2_g)


def _staircase():
    return [(i, PEER_TOPK // (i + 1)) for i in range(PEER_TOPK)]


def _exchange(v, i, l):
    v[i], v[l] = jnp.maximum(v[i], v[l]), jnp.minimum(v[i], v[l])


def _bitonic_merge_desc(v):
    j = len(v) // 2
    while j >= 1:
        for i in range(len(v)):
            if i ^ j > i:
                _exchange(v, i, i ^ j)
        j //= 2


def _top16(s):
    v = [s[SUBLANES * i:SUBLANES * (i + 1)] for i in range(PEER_KEYS // SUBLANES)]
    k = 2
    while k <= len(v):
        j = k // 2
        while j >= 1:
            for i in range(len(v)):
                l = i ^ j
                if l > i:
                    if (i & k) == 0 or k == len(v):
                        _exchange(v, i, l)
                    else:
                        _exchange(v, l, i)
            j //= 2
        k *= 2
    for shift in (4, 2, 1):
        other = [pltpu.roll(a, shift, 0) for a in v]
        v = [jnp.maximum(a, other[len(v) - 1 - i]) for i, a in enumerate(v)]
        _bitonic_merge_desc(v)
    return [a[0:1] for a in v]


def _spread_by_rank(s, tops, values, default):
    out = jnp.full(s.shape, default, F32)
    for t, v in zip(tops, values):
        out = jnp.where(s == t, v, out)
    return out


def _score_kernel(h2_ref, wq_ref, keys_ref, r2_ref, p2_ref, cnt_ref, p1_ref, cands_ref):
    h2 = _unpack(h2_ref[...])
    for h in range(PEER_HEADS):
        cand_ref = cands_ref.at[h]
        qh = jnp.dot(h2, wq_ref[h], preferred_element_type=F32).astype(BF16)
        s1_all = lax.dot_general(keys_ref[h, 0], qh[:, :PEER_HALF], _NT, preferred_element_type=F32)
        s2_all = lax.dot_general(keys_ref[h, 1], qh[:, PEER_HALF:], _NT, preferred_element_type=F32)
        for part in range(TILE // 128):
            lanes = slice(part * 128, (part + 1) * 128)
            s1, s2 = s1_all[:, lanes], s2_all[:, lanes]
            t1 = _top16(s1)
            t2 = _top16(s2)
            t2_all = jnp.concatenate(t2, axis=0)
            cand_ref[:, lanes] = jnp.full((PEER_NCAND, 128), -jnp.inf, F32)
            off = 0
            for i, n_i in _staircase():
                cand_ref[off:off + n_i, lanes] = t1[i] + t2_all[0:n_i]
                off += n_i
            cand = cand_ref[:, lanes]
            c_max = t1[0] + t2[0]
            work = cand
            z = jnp.zeros_like(c_max)
            tau = c_max
            for it in range(PEER_TOPK):
                tau = jnp.max(work, axis=0, keepdims=True)
                z = z + jnp.exp(tau - c_max)
                work = jnp.where(work == tau, -jnp.inf, work)
            chosen = (cand >= tau).astype(F32)
            counts, off = [], 0
            for i, n_i in _staircase():
                counts.append(jnp.sum(chosen[off:off + n_i], axis=0, keepdims=True))
                off += n_i
            r2_ref[h, 0, :, lanes] = _pack(_spread_by_rank(s2, t2, [float(i) for i in range(PEER_TOPK)],
                                                           float(PEER_TOPK)))
            p2_ref[h, 0, :, lanes] = _pack(jnp.exp(s2 - t2[0]))
            cnt_ref[h, 0, :, lanes] = _spread_by_rank(s1, t1, counts, 0.0)
            p1_ref[h, 0, :, lanes] = 0.5 * jnp.exp(s1 - t1[0]) / z


def _peer_scores(h2, w_q, keys):
    ntok = 2 * h2.shape[0]
    blk = lambda rows: pl.BlockSpec((PEER_HEADS, 1, rows, TILE), lambda i: (0, i, 0, 0))
    packed = jax.ShapeDtypeStruct((PEER_HEADS, ntok // TILE, PEER_KEYS // 2, TILE), jnp.uint32)
    plain = jax.ShapeDtypeStruct((PEER_HEADS, ntok // TILE, PEER_KEYS, TILE), F32)
    return pl.pallas_call(
        _score_kernel,
        grid=(ntok // TILE,),
        in_specs=[
            pl.BlockSpec((TILE // 2, D_MODEL), lambda i: (i, 0)),
            pl.BlockSpec(w_q.shape, lambda i: (0, 0, 0)),
            pl.BlockSpec(keys.shape, lambda i: (0, 0, 0, 0)),
        ],
        out_specs=[blk(PEER_KEYS // 2), blk(PEER_KEYS // 2), blk(PEER_KEYS), blk(PEER_KEYS)],
        out_shape=[packed, packed, plain, plain],
        scratch_shapes=[pltpu.VMEM((PEER_HEADS, PEER_NCAND, TILE), F32)],
        compiler_params=_params("parallel"),
        name="peer_scores",
    )(h2, w_q, keys)


def _pack_tables_kernel(u_ref, v_ref, up_ref, vtp_ref):
    up_ref[...] = _pack(u_ref[...])
    vtp_ref[...] = _pack(v_ref[...].T)


def _pack_tables(u_tab, v_tab):
    nexp, d = u_tab.shape
    te = _largest_divisor(nexp, 512, 128)
    tile = lambda: pl.BlockSpec((te, d), lambda e: (e, 0))
    return pl.pallas_call(
        _pack_tables_kernel,
        grid=(nexp // te,),
        in_specs=[tile(), tile()],
        out_specs=[pl.BlockSpec((te // 2, d), lambda e: (e, 0)), pl.BlockSpec((d // 2, te), lambda e: (0, e))],
        out_shape=[jax.ShapeDtypeStruct((nexp // 2, d), jnp.uint32), jax.ShapeDtypeStruct((d // 2, nexp), jnp.uint32)],
        compiler_params=_params("parallel"),
        name="pack_tables",
    )(u_tab.astype(F32), v_tab.astype(F32))


def _dense_kernel(h2_ref, u_ref, vt_ref, r2_ref, p2_ref, cnt_ref, p1_ref, o_ref, act0_ref, act1_ref, wa0_ref,
                  wa1_ref, *, n_sub, n_etiles):
    s = pl.program_id(1)

    @pl.when(s == 0)
    def _():
        o_ref[...] = jnp.zeros_like(o_ref)

    n_first = PEER_ETILE // PEER_KEYS
    zero = jnp.zeros((), BF16)
    spread = lambda row: jnp.broadcast_to(row, (PEER_KEYS, 128)).astype(BF16)

    def sub_tile(c, carry, *, stages, act_a, act_b, wa_b, wa_c):
        if "A" in stages:
            h2 = _unpack(h2_ref[pl.ds(pl.multiple_of(c * (TILE // 2), TILE // 2), TILE // 2), :])
            act_a[c] = lax.dot_general(_unpack(u_ref[...]), h2, _NT, preferred_element_type=F32)
        for lc in range(TILE // 128 if "B" in stages else 0):
            lanes = slice(lc * 128, (lc + 1) * 128)
            for j in range(0, n_first, 2):
                w0 = jnp.zeros((PEER_KEYS, 128), BF16)
                w1 = jnp.zeros((PEER_KEYS, 128), BF16)
                for h in range(PEER_HEADS):
                    r2 = _unpack(r2_ref[h, c, :, lanes])
                    p2 = _unpack(p2_ref[h, c, :, lanes])
                    cnt = cnt_ref[h, c, :, lanes]
                    p1 = p1_ref[h, c, :, lanes]
                    w0 += jnp.where(r2 < spread(cnt[j:j + 1]), p2, zero) * spread(p1[j:j + 1])
                    w1 += jnp.where(r2 < spread(cnt[j + 1:j + 2]), p2, zero) * spread(p1[j + 1:j + 2])
                for k, w in ((j, w0), (j + 1, w1)):
                    rows = slice(k * PEER_KEYS, (k + 1) * PEER_KEYS)
                    act = act_b[c, rows, lanes]
                    twice_gelu = act * (1.0 + lax.erf(act * (1.0 / math.sqrt(2.0))))
                    wa_b[c, rows, lanes] = w * twice_gelu.astype(BF16)
        if "C" in stages:
            o_ref[c] += jnp.dot(_unpack(vt_ref[...]), wa_c[c], preferred_element_type=F32)
        return carry

    def run(condition, stages, parity):
        bufs = (dict(act_a=act0_ref, act_b=act1_ref, wa_b=wa1_ref, wa_c=wa0_ref) if parity == 0 else
                dict(act_a=act1_ref, act_b=act0_ref, wa_b=wa0_ref, wa_c=wa1_ref))

        @pl.when(condition)
        def _():
            lax.fori_loop(0, n_sub, functools.partial(sub_tile, stages=stages, **bufs), 0)

    steady = jnp.logical_and(s >= 2, s < n_etiles)
    run(s == 0, "A", 0)
    run(s == 1, "AB", 1)
    run(jnp.logical_and(steady, lax.rem(s, 2) == 0), "ABC", 0)
    run(jnp.logical_and(steady, lax.rem(s, 2) == 1), "ABC", 1)
    run(s == n_etiles, "BC", n_etiles % 2)
    run(s == n_etiles + 1, "C", (n_etiles + 1) % 2)


def _peer_dense(h2, u_tab, vt_tab, r2, p2, cnt, p1):
    ntok = 2 * h2.shape[0]
    n_sub = _largest_divisor(ntok // TILE, PEER_TTILE // TILE)
    tn = n_sub * TILE
    n_etiles = 2 * u_tab.shape[0] // PEER_ETILE
    last = n_etiles - 1
    stage = lambda s, lag: jnp.clip(s - lag, 0, last)
    second = lambda: pl.BlockSpec((PEER_HEADS, n_sub, PEER_KEYS // 2, TILE), lambda t, s: (0, t, 0, 0))
    first = lambda: pl.BlockSpec((PEER_HEADS, n_sub, PEER_ETILE // PEER_KEYS, TILE),
                                 lambda t, s: (0, t, stage(s, 1), 0))
    return pl.pallas_call(
        functools.partial(_dense_kernel, n_sub=n_sub, n_etiles=n_etiles),
        grid=(ntok // tn, n_etiles + 2),
        in_specs=[pl.BlockSpec((tn // 2, D_MODEL), lambda t, s: (t, 0)),
                  pl.BlockSpec((PEER_ETILE // 2, D_MODEL), lambda t, s: (stage(s, 0), 0)),
                  pl.BlockSpec((D_MODEL // 2, PEER_ETILE), lambda t, s: (0, stage(s, 2))),
                  second(), second(), first(), first()],
        out_specs=pl.BlockSpec((n_sub, D_MODEL, TILE), lambda t, s: (t, 0, 0)),
        out_shape=jax.ShapeDtypeStruct((ntok // TILE, D_MODEL, TILE), F32),
        scratch_shapes=[pltpu.VMEM((n_sub, PEER_ETILE, TILE), F32), pltpu.VMEM((n_sub, PEER_ETILE, TILE), F32),
                        pltpu.VMEM((n_sub, PEER_ETILE, TILE), BF16), pltpu.VMEM((n_sub, PEER_ETILE, TILE), BF16)],
        compiler_params=_params("parallel", "arbitrary"),
        name="peer_dense",
    )(h2, u_tab, vt_tab, r2, p2, cnt, p1)


def _final_kernel(ot_ref, hs_ref, g_ref, y_ref):
    hs = hs_ref[...] + ot_ref[0].T
    ms = jnp.mean(hs * hs, axis=-1, keepdims=True)
    y_ref[0] = hs * lax.rsqrt(ms + EPS) * g_ref[...]


def _final(o_t, hs, final_g, first_seq, n_seq, nt):
    tile_of = lambda b, i: (b + first_seq) * nt + i + 1
    return pl.pallas_call(
        _final_kernel,
        grid=(n_seq, nt - 1),
        in_specs=[
            pl.BlockSpec((1, D_MODEL, TILE), lambda b, i: (tile_of(b, i), 0, 0)),
            pl.BlockSpec((TILE, D_MODEL), lambda b, i: (tile_of(b, i), 0)),
            pl.BlockSpec((1, D_MODEL), lambda b, i: (0, 0)),
        ],
        out_specs=pl.BlockSpec((1, TILE, D_MODEL), lambda b, i: (b, i, 0)),
        out_shape=jax.ShapeDtypeStruct((n_seq, (nt - 1) * TILE, D_MODEL), F32),
        compiler_params=_params("parallel", "parallel"),
        name="final",
    )(o_t, hs, final_g)


def kernel(x_prompt, x_sample, meta, norm1_g, w_in, s5_lam_re, s5_lam_im, s5_log_step, s5_b_re, s5_b_im,
           s5_c_re, s5_c_im, s5_d, w_glu, hg_lb, hg_norm_g, w_hg_out, w_out, norm2_g, peer_wq, peer_keys,
           peer_u, peer_v, final_g):
    assert x_prompt.shape[1] == x_sample.shape[1] and x_prompt.shape[1] % TILE == 0
    assert norm1_g.shape[0] == 1, "single-layer trunk"
    n_prompt = x_prompt.shape[0]
    x_prompt, x_sample = x_prompt.astype(F32), x_sample.astype(F32)
    metap = jnp.concatenate([jnp.zeros((FRONT, D_MODEL), F32), meta.astype(F32)], axis=0)
    row = lambda a: a.astype(F32).reshape(1, -1)

    u, q_pre, f_fwd, f_bwd, i_in, gates = _inproj(x_prompt, x_sample, metap, row(norm1_g[0]),
                                                  w_in[0].astype(BF16))

    y_s5 = _s5_mixer(u, *_s5_weights(s5_lam_re[0], s5_lam_im[0], s5_log_step[0], s5_b_re[0], s5_b_im[0],
                                     s5_c_re[0], s5_c_im[0], s5_d[0]))

    lb = jax.nn.softmax(hg_lb.astype(F32), axis=0)[0]
    o_f, o_b = _hgrn2(q_pre, f_fwd, f_bwd, i_in, row(lb))

    hs, h2 = _merge(x_prompt, x_sample, metap, y_s5, o_f, o_b, gates, w_glu[0].astype(BF16), row(hg_norm_g[0]),
                    w_hg_out[0].astype(BF16), w_out[0].astype(BF16), row(norm2_g[0]))

    bsz, p, _ = hs.shape
    hs = hs.reshape(bsz * p, D_MODEL)
    h2 = h2.reshape(bsz * p // 2, D_MODEL)
    w_q = peer_wq[0].astype(BF16).reshape(D_MODEL, PEER_HEADS, PEER_QDIM).transpose(1, 0, 2)
    r2, p2, cnt, p1 = _peer_scores(h2, w_q, peer_keys[0].astype(BF16))
    u_tab, vt_tab = _pack_tables(peer_u[0], peer_v[0])
    o_t = _peer_dense(h2, u_tab, vt_tab, r2, p2, cnt, p1)
    fin = functools.partial(_final, o_t, hs, row(final_g), nt=p // TILE)
    return (fin(first_seq=0, n_seq=n_prompt), fin(first_seq=n_prompt, n_seq=bsz - n_prompt))
```
